```python
import jax, jax.numpy as jnp
from jax import lax
import numpy as np

D_MODEL = 1024
BATCH = 2
SEQ = 8192
DEPTH = 2

HEAD_DIM = 64
ROPE_THETA = 10000.0
NORM_EPS = 1e-6
BLOCK = 128
A_CONFIGS = ((128, 1), (512, 4), (2048, 16))
A_HEADS_PER_GROUP = 8
A_HEADS = A_HEADS_PER_GROUP * len(A_CONFIGS)
A_WIDTH = A_HEADS_PER_GROUP * HEAD_DIM
B_Q_HEADS = 8
B_KV_HEADS = 2
B_GROUP = B_Q_HEADS // B_KV_HEADS
B_WINDOW = 128
B_WIDTH = B_Q_HEADS * HEAD_DIM
C_HEADS = 4
C_DK = 64
C_DV = 128
C_GATE_RANK = 16
C_TAU = 16.0
C_CHUNK = 64
C_WIDTH = C_HEADS * C_DV
N_BRANCH = 3
BRANCH_WIDTH = 512
D_FF = 2816
IN_WIDTHS = (3 * A_HEADS * HEAD_DIM,
             B_Q_HEADS * HEAD_DIM,
             2 * B_KV_HEADS * HEAD_DIM,
             C_HEADS * C_DK,
             C_HEADS * C_DK,
             C_HEADS * C_DV,
             C_GATE_RANK,
             C_WIDTH,
             N_BRANCH * D_MODEL)
IN_WIDTH = sum(IN_WIDTHS)

kernel_name = 'hybrid_gated_dilated_swa_gla'


def rmsnorm(x, g):
    x32 = x.astype(jnp.float32)
    y = x32 * lax.rsqrt(jnp.mean(x32 * x32, axis=-1, keepdims=True) + NORM_EPS)
    return (y * g.astype(jnp.float32)).astype(x.dtype)


def swiglu(h, w_in, w_out):
    g, u = jnp.split(h @ w_in, 2, axis=-1)
    return (jax.nn.silu(g) * u) @ w_out


def rope_tables(positions):
    inv = ROPE_THETA ** (-jnp.arange(0, HEAD_DIM, 2, dtype=jnp.float32) / HEAD_DIM)
    ang = positions.astype(jnp.float32)[..., None] * inv
    return jnp.cos(ang)[:, :, None, :], jnp.sin(ang)[:, :, None, :]


def apply_rope(x, cos, sin):
    x1, x2 = jnp.split(x.astype(jnp.float32), 2, axis=-1)
    return jnp.concatenate([x1 * cos - x2 * sin, x2 * cos + x1 * sin], axis=-1).astype(x.dtype)


def band_attention(q, k, v, window, sink):
    bq, n, hkv, grp, hd = q.shape
    nb = -(-n // BLOCK)
    pad = nb * BLOCK - n
    qb = jnp.pad(q, ((0, 0), (0, pad), (0, 0), (0, 0), (0, 0))).reshape(bq, nb, BLOCK, hkv, grp, hd)

    def key_windows(t):
        tb = jnp.pad(t, ((0, 0), (BLOCK, pad), (0, 0), (0, 0))).reshape(bq, nb + 1, BLOCK, hkv, hd)
        return jnp.concatenate([tb[:, :-1], tb[:, 1:]], axis=2)

    kw, vw = key_windows(k), key_windows(v)
    s = jnp.einsum('bnqkgd,bnskd->bnkgqs', qb, kw).astype(jnp.float32) * (hd ** -0.5)
    qi = jnp.arange(BLOCK)[:, None]
    kj = jnp.arange(2 * BLOCK)[None, :]
    dist = qi + BLOCK - kj
    key_pos = jnp.arange(nb)[:, None, None] * BLOCK + kj - BLOCK
    valid = (dist >= 0) & (dist <= window) & (key_pos >= 0)
    s = jnp.where(valid[None, :, None, None], s, -jnp.inf)
    m = jnp.max(s, axis=-1)
    if sink is not None:
        sk = sink.astype(jnp.float32)[None, None, :, :, None]
        m = jnp.maximum(m, sk)
    p = jnp.exp(s - m[..., None])
    denom = jnp.sum(p, axis=-1)
    if sink is not None:
        denom = denom + jnp.exp(sk - m)
    o = jnp.einsum('bnkgqs,bnskd->bnqkgd', p / denom[..., None], vw.astype(jnp.float32))
    o = o.reshape(bq, nb * BLOCK, hkv, grp, hd)[:, :n].astype(q.dtype)
    lse = (m + jnp.log(denom)).transpose(0, 1, 4, 2, 3).reshape(bq, nb * BLOCK, hkv, grp)[:, :n]
    return o, lse


def dilated_group(q, k, v, window, dilation):
    bn, s, hh, hd = q.shape
    n = s // dilation

    def to_classes(t):
        return t.reshape(bn, n, dilation, hh, hd).transpose(0, 2, 1, 3, 4).reshape(bn * dilation, n, hh, hd)

    o, lse = band_attention(to_classes(q)[:, :, :, None, :], to_classes(k), to_classes(v),
                            window // dilation, None)
    o = o[:, :, :, 0].reshape(bn, dilation, n, hh, hd).transpose(0, 2, 1, 3, 4).reshape(bn, s, hh, hd)
    lse = lse[:, :, :, 0].reshape(bn, dilation, n, hh).transpose(0, 2, 1, 3).reshape(bn, s, hh)
    return o, lse


def dilated_mixer(q, k, v):
    bn, s = q.shape[:2]
    outs, lses = [], []
    for g, (window, dilation) in enumerate(A_CONFIGS):
        sl = slice(g * A_HEADS_PER_GROUP, (g + 1) * A_HEADS_PER_GROUP)
        o, lse = dilated_group(q[:, :, sl], k[:, :, sl], v[:, :, sl], window, dilation)
        outs.append(o)
        lses.append(lse)
    w = jax.nn.softmax(jnp.stack(lses, axis=2), axis=2)
    o = jnp.sum(w[..., None] * jnp.stack(outs, axis=2).astype(jnp.float32), axis=2)
    return o.reshape(bn, s, A_WIDTH).astype(q.dtype)


def gla_chunked(q, k, v, log_g):
    bn, s, hh, dk = q.shape
    dv = v.shape[-1]
    nc = s // C_CHUNK

    def chunks(t):
        return t.astype(jnp.float32).reshape(bn, nc, C_CHUNK, hh, t.shape[-1]).transpose(1, 0, 3, 2, 4)

    qc = chunks(q) * (dk ** -0.5)
    kc, vc, gc = chunks(k), chunks(v), chunks(log_g)
    causal = jnp.tril(jnp.ones((C_CHUNK, C_CHUNK), dtype=bool))[:, :, None]

    def step(state, inp):
        qi, ki, vi, gi = inp
        b = jnp.cumsum(gi, axis=-2)
        o_inter = jnp.einsum('bhtd,bhde->bhte', qi * jnp.exp(b), state)
        diff = b[:, :, :, None, :] - b[:, :, None, :, :]
        decay = jnp.where(causal, jnp.exp(jnp.where(causal, diff, 0.0)), 0.0)
        att = jnp.einsum('bhtd,bhsd,bhtsd->bhts', qi, ki, decay)
        o_intra = jnp.einsum('bhts,bhse->bhte', att, vi)
        b_last = b[:, :, -1:, :]
        new_state = (jnp.exp(b_last[:, :, 0, :])[..., None] * state
                     + jnp.einsum('bhsd,bhse->bhde', ki * jnp.exp(b_last - b), vi))
        return new_state, o_inter + o_intra

    state0 = jnp.zeros((bn, hh, dk, dv), jnp.float32)
    _, o = lax.scan(step, state0, (qc, kc, vc, gc))
    return o.transpose(1, 0, 3, 2, 4).reshape(bn, s, hh, dv).astype(q.dtype)


def hybrid_mixer(h, cos, sin, w_in, a_q_norm, a_k_norm, b_q_norm, b_k_norm, b_sinks,
                 c_gate_up, c_gate_bias, c_out_norm, w_branch, w_out):
    bn, s, _ = h.shape
    splits = np.cumsum(IN_WIDTHS)[:-1].tolist()
    a_qkv, b_q, b_kv, c_q, c_k, c_v, c_glow, c_r, gate_pre = jnp.split(h @ w_in, splits, axis=-1)
    a_qkv = a_qkv.reshape(bn, s, 3, A_HEADS, HEAD_DIM)
    a_q = apply_rope(rmsnorm(a_qkv[:, :, 0], a_q_norm), cos, sin)
    a_k = apply_rope(rmsnorm(a_qkv[:, :, 1], a_k_norm), cos, sin)
    y_a = dilated_mixer(a_q, a_k, a_qkv[:, :, 2])
    b_q = apply_rope(rmsnorm(b_q.reshape(bn, s, B_Q_HEADS, HEAD_DIM), b_q_norm), cos, sin)
    b_kv = b_kv.reshape(bn, s, 2, B_KV_HEADS, HEAD_DIM)
    b_k = apply_rope(rmsnorm(b_kv[:, :, 0], b_k_norm), cos, sin)
    o_b, _ = band_attention(b_q.reshape(bn, s, B_KV_HEADS, B_GROUP, HEAD_DIM), b_k, b_kv[:, :, 1],
                            B_WINDOW - 1, b_sinks.reshape(B_KV_HEADS, B_GROUP))
    y_b = o_b.reshape(bn, s, B_WIDTH)
    log_g = jax.nn.log_sigmoid((c_glow @ c_gate_up + c_gate_bias).astype(jnp.float32)) / C_TAU
    o_c = gla_chunked(c_q.reshape(bn, s, C_HEADS, C_DK), c_k.reshape(bn, s, C_HEADS, C_DK),
                      c_v.reshape(bn, s, C_HEADS, C_DV), log_g.reshape(bn, s, C_HEADS, C_DK))
    y_c = (rmsnorm(o_c, c_out_norm) * jax.nn.silu(c_r.reshape(bn, s, C_HEADS, C_DV))).reshape(bn, s, C_WIDTH)
    gates = jax.nn.sigmoid(gate_pre.reshape(bn, s, N_BRANCH, D_MODEL))
    merged = (gates[:, :, 0] * (y_a @ w_branch[0])
              + gates[:, :, 1] * (y_b @ w_branch[1])
              + gates[:, :, 2] * (y_c @ w_branch[2]))
    return merged @ w_out


def setup_inputs(seed: int = 0) -> dict:
    key = jax.random.key(seed)
    ks = jax.random.split(key, 20)

    def nrm(k, shape, scale):
        return jax.random.normal(k, shape, jnp.float32) * scale

    def gain(k, shape):
        return 1.0 + nrm(k, shape, 0.02)

    L = DEPTH
    return {
        'x': nrm(ks[0], (BATCH, SEQ, D_MODEL), 1.0),
        'positions': jnp.broadcast_to(jnp.arange(SEQ, dtype=jnp.int32), (BATCH, SEQ)),
        'norm_ffn1': gain(ks[1], (L, D_MODEL)),
        'w_ffn1_in': nrm(ks[2], (L, D_MODEL, 2 * D_FF), D_MODEL ** -0.5),
        'w_ffn1_out': nrm(ks[3], (L, D_FF, D_MODEL), D_FF ** -0.5),
        'norm_mix': gain(ks[4], (L, D_MODEL)),
        'w_in': nrm(ks[5], (L, D_MODEL, IN_WIDTH), D_MODEL ** -0.5),
        'a_q_norm': gain(ks[6], (L, HEAD_DIM)),
        'a_k_norm': gain(ks[7], (L, HEAD_DIM)),
        'b_q_norm': gain(ks[8], (L, HEAD_DIM)),
        'b_k_norm': gain(ks[9], (L, HEAD_DIM)),
        'b_sinks': nrm(ks[10], (L, B_Q_HEADS), 1.0),
        'c_gate_up': nrm(ks[11], (L, C_GATE_RANK, C_HEADS * C_DK), C_GATE_RANK ** -0.5),
        'c_gate_bias': nrm(ks[12], (L, C_HEADS * C_DK), 0.1),
        'c_out_norm': gain(ks[13], (L, C_DV)),
        'w_branch': nrm(ks[14], (L, N_BRANCH, BRANCH_WIDTH, D_MODEL), BRANCH_WIDTH ** -0.5),
        'w_out': nrm(ks[15], (L, D_MODEL, D_MODEL), D_MODEL ** -0.5),
        'norm_ffn2': gain(ks[16], (L, D_MODEL)),
        'w_ffn2_in': nrm(ks[17], (L, D_MODEL, 2 * D_FF), D_MODEL ** -0.5),
        'w_ffn2_out': nrm(ks[18], (L, D_FF, D_MODEL), D_FF ** -0.5),
    }


def reference(x, positions, norm_ffn1, w_ffn1_in, w_ffn1_out, norm_mix, w_in, a_q_norm, a_k_norm,
              b_q_norm, b_k_norm, b_sinks, c_gate_up, c_gate_bias, c_out_norm, w_branch, w_out,
              norm_ffn2, w_ffn2_in, w_ffn2_out):
    cos, sin = rope_tables(positions)
    for l in range(DEPTH):
        x = x + 0.5 * swiglu(rmsnorm(x, norm_ffn1[l]), w_ffn1_in[l], w_ffn1_out[l])
        h = rmsnorm(x, norm_mix[l])
        x = x + hybrid_mixer(h, cos, sin, w_in[l], a_q_norm[l], a_k_norm[l], b_q_norm[l], b_k_norm[l],
                             b_sinks[l], c_gate_up[l], c_gate_bias[l], c_out_norm[l], w_branch[l], w_out[l])
        x = x + 0.5 * swiglu(rmsnorm(x, norm_ffn2[l]), w_ffn2_in[l], w_ffn2_out[l])
    return x
```

```python
import functools

import numpy as np
import jax
import jax.numpy as jnp
from jax import lax
from jax.experimental import pallas as pl
from jax.experimental.pallas import tpu as pltpu

F32 = jnp.float32
BF16 = jnp.bfloat16

D_MODEL = 1024
D_FF = 2816
HEAD_DIM = 64
NORM_EPS = 1e-6
ROPE_THETA = 10000.0
LANES = 128
BLOCK = 128
A_CONFIGS = ((128, 1), (512, 4), (2048, 16))
A_GROUP_WIDTH = 512
B_WINDOW = 128
C_HEADS = 4
C_DK = 64
C_DV = 128
C_GATE_RANK = 16
C_TAU = 16.0
C_CHUNK = 64
N_BRANCH = 3

A_SEG = 3 * A_GROUP_WIDTH
OFF_A = 0
OFF_B = 3 * A_SEG
B_SEG = 1024
OFF_C = OFF_B + B_SEG
C_SEG = 2176
W_PACKED = OFF_C + C_SEG

TOKEN_TILE = 512
ATTN_CHUNK = 2048
FFN_CHUNK = 1408
VMEM_LIMIT = 56 * 2**20


def _params(n_axes):
    return pltpu.CompilerParams(dimension_semantics=("arbitrary",) * n_axes, vmem_limit_bytes=VMEM_LIMIT)


def _rms(x, gain):
    return x * lax.rsqrt(jnp.mean(x * x, axis=-1, keepdims=True) + NORM_EPS) * gain


def _resident(block, index_map):
    return pl.BlockSpec(block, index_map, pipeline_mode=pl.Buffered(1))


def _rope_kernel(pos_ref, inv_ref, sgn_ref, cos_ref, sin_ref):
    ang = pos_ref[...].astype(F32) * inv_ref[...]
    cos_ref[...] = jnp.cos(ang)
    sin_ref[...] = jnp.sin(ang) * sgn_ref[...]


def _rope_tables(positions):
    t = positions.size
    inv = ROPE_THETA ** (-jnp.arange(0, HEAD_DIM, 2, dtype=F32) / HEAD_DIM)
    inv = jnp.tile(inv, LANES // (HEAD_DIM // 2))[None, :]
    sgn = jnp.tile(jnp.concatenate([-jnp.ones(HEAD_DIM // 2, F32), jnp.ones(HEAD_DIM // 2, F32)]),
                   LANES // HEAD_DIM)[None, :]
    tm = TOKEN_TILE
    tab = jax.ShapeDtypeStruct((t, LANES), F32)
    return pl.pallas_call(
        _rope_kernel,
        grid=(t // tm,),
        in_specs=[pl.BlockSpec((tm, 1), lambda i: (i, 0)),
                  pl.BlockSpec((1, LANES), lambda i: (0, 0)),
                  pl.BlockSpec((1, LANES), lambda i: (0, 0))],
        out_specs=[pl.BlockSpec((tm, LANES), lambda i: (i, 0))] * 2,
        out_shape=[tab, tab],
        name="rope",
        compiler_params=_params(1),
    )(positions.reshape(t, 1), inv, sgn)


def _ffn_kernel(x_ref, g_ref, wg_ref, wu_ref, wo_ref, o_ref):
    x = x_ref[...]
    h = _rms(x, g_ref[...]).astype(BF16)
    acc = None
    for c in range(D_FF // FFN_CHUNK):
        sl = slice(c * FFN_CHUNK, (c + 1) * FFN_CHUNK)
        g = jnp.dot(h, wg_ref[:, sl], preferred_element_type=F32)
        u = jnp.dot(h, wu_ref[:, sl], preferred_element_type=F32)
        a = (g * jax.nn.sigmoid(g) * u).astype(BF16)
        part = jnp.dot(a, wo_ref[sl, :], preferred_element_type=F32)
        acc = part if acc is None else acc + part
    o_ref[...] = x + 0.5 * acc


def _ffn(x, gain, w_in, w_out, layer):
    t = x.shape[0]
    tm = TOKEN_TILE
    return pl.pallas_call(
        _ffn_kernel,
        grid=(t // tm,),
        in_specs=[pl.BlockSpec((tm, D_MODEL), lambda i: (i, 0)),
                  pl.BlockSpec((None, 1, D_MODEL), lambda i: (layer, 0, 0)),
                  _resident((None, D_MODEL, D_FF), lambda i: (layer, 0, 0)),
                  _resident((None, D_MODEL, D_FF), lambda i: (layer, 0, 1)),
                  _resident((None, D_FF, D_MODEL), lambda i: (layer, 0, 0))],
        out_specs=pl.BlockSpec((tm, D_MODEL), lambda i: (i, 0)),
        out_shape=jax.ShapeDtypeStruct((t, D_MODEL), F32),
        name="ffn",
        compiler_params=_params(1),
    )(x, gain, w_in, w_in, w_out)


def _inproj_kernel(x_ref, gm_ref, cos_ref, sin_ref, w_ref, gains_ref, ones_ref,
                   qa1, ka1, va1, qa2, ka2, va2, qa3, ka3, va3, qb, kb, vb, qc, kc, vc, gl, cr,
                   hs_ref, hp_ref, tb_ref):
    tm = x_ref.shape[0]
    n_slab = D_MODEL // LANES
    h = _rms(x_ref[...], gm_ref[...])
    h_bf = h.astype(BF16)
    cos0 = cos_ref[...]
    sin0 = sin_ref[...]
    ones = ones_ref[...]
    gains = gains_ref[...]
    first_half = (lax.broadcasted_iota(jnp.int32, (1, LANES), 1) % HEAD_DIM) < (HEAD_DIM // 2)

    def norm_rope(y, gain, cos, sin):
        ss = jnp.dot((y * y).astype(BF16), ones, preferred_element_type=F32)
        yn = y * lax.rsqrt(ss * (1.0 / HEAD_DIM) + NORM_EPS) * gain
        partner = jnp.where(first_half, pltpu.roll(yn, LANES - HEAD_DIM // 2, 1), pltpu.roll(yn, HEAD_DIM // 2, 1))
        return yn * cos + partner * sin

    for s in range(n_slab):
        hs_ref[s] = h[:, s * LANES:(s + 1) * LANES]
    tb_ref[0] = cos0
    tb_ref[1] = sin0

    def class_major(d):
        if d == 1:
            return h_bf, cos0, sin0
        rows = tm // d
        for c in range(d):
            for s in range(n_slab):
                hp_ref[c * rows:(c + 1) * rows, s * LANES:(s + 1) * LANES] = (
                    hs_ref[s, pl.ds(c, rows, stride=d), :].astype(BF16))
        cos = jnp.concatenate([tb_ref[0, pl.ds(c, rows, stride=d), :] for c in range(d)], axis=0)
        sin = jnp.concatenate([tb_ref[1, pl.ds(c, rows, stride=d), :] for c in range(d)], axis=0)
        return hp_ref[...], cos, sin

    def store_classes(ref, col, val, d):
        rows = tm // d
        for c in range(d):
            ref[c, :, col:col + LANES] = val[c * rows:(c + 1) * rows, :]

    for gi, (qo, ko, vo) in enumerate(((qa1, ka1, va1), (qa2, ka2, va2), (qa3, ka3, va3))):
        d = A_CONFIGS[gi][1]
        hp, cos, sin = class_major(d)
        y = jnp.dot(hp, w_ref[:, OFF_A + gi * A_SEG:OFF_A + (gi + 1) * A_SEG], preferred_element_type=F32)
        for j in range(A_GROUP_WIDTH // LANES):
            c0 = j * LANES
            store_classes(qo, c0, norm_rope(y[:, c0:c0 + LANES], gains[0:1], cos, sin).astype(BF16), d)
            c1 = A_GROUP_WIDTH + c0
            store_classes(ko, c0, norm_rope(y[:, c1:c1 + LANES], gains[1:2], cos, sin).astype(BF16), d)
            c2 = 2 * A_GROUP_WIDTH + c0
            store_classes(vo, c0, y[:, c2:c2 + LANES].astype(BF16), d)

    y = jnp.dot(h_bf, w_ref[:, OFF_B:OFF_B + B_SEG], preferred_element_type=F32)
    for j in range(4):
        c0 = j * LANES
        qb[0, :, c0:c0 + LANES] = norm_rope(y[:, c0:c0 + LANES], gains[2:3], cos0, sin0).astype(BF16)
    for j in range(2):
        c0 = j * LANES
        kb[0, :, c0:c0 + LANES] = norm_rope(y[:, 512 + c0:512 + c0 + LANES], gains[3:4], cos0, sin0).astype(BF16)
        vb[0, :, c0:c0 + LANES] = y[:, 768 + c0:768 + c0 + LANES].astype(BF16)

    y = jnp.dot(h_bf, w_ref[:, OFF_C:OFF_C + C_SEG], preferred_element_type=F32)
    qc[...] = y[:, 0:512].astype(BF16)
    kc[...] = y[:, 512:1024].astype(BF16)
    vc[...] = y[:, 1024:1536].astype(BF16)
    gl[...] = y[:, 1536:1664].astype(BF16)
    cr[...] = y[:, 1664:2176].astype(BF16)


def _inproj(x, gain, cos, sin, w_packed, head_gains, ones_bd, layer, batch, seq):
    t = x.shape[0]
    tm = TOKEN_TILE
    tiles_per_seq = seq // tm

    def cm_spec(d, width):
        return pl.BlockSpec((None, d, tm // d, width),
                            lambda i: (i // tiles_per_seq, 0, i % tiles_per_seq, 0))

    def cm_shape(d, width):
        return jax.ShapeDtypeStruct((batch, d, seq // d, width), BF16)

    def tok_spec(width):
        return pl.BlockSpec((tm, width), lambda i: (i, 0))

    out_specs, out_shape = [], []
    for _, d in A_CONFIGS:
        for _ in range(3):
            out_specs.append(cm_spec(d, A_GROUP_WIDTH))
            out_shape.append(cm_shape(d, A_GROUP_WIDTH))
    for width in (512, 256, 256):
        out_specs.append(cm_spec(1, width))
        out_shape.append(cm_shape(1, width))
    for width in (512, 512, 512, 128, 512):
        out_specs.append(tok_spec(width))
        out_shape.append(jax.ShapeDtypeStruct((t, width), BF16))

    return pl.pallas_call(
        _inproj_kernel,
        grid=(t // tm,),
        in_specs=[tok_spec(D_MODEL),
                  pl.BlockSpec((None, 1, D_MODEL), lambda i: (layer, 0, 0)),
                  tok_spec(LANES), tok_spec(LANES),
                  _resident((None, D_MODEL, W_PACKED), lambda i: (layer, 0, 0)),
                  pl.BlockSpec((None, 4, LANES), lambda i: (layer, 0, 0)),
                  pl.BlockSpec((LANES, LANES), lambda i: (0, 0))],
        out_specs=out_specs,
        out_shape=out_shape,
        scratch_shapes=[pltpu.VMEM((D_MODEL // LANES, tm, LANES), F32),
                        pltpu.VMEM((tm, D_MODEL), BF16),
                        pltpu.VMEM((2, tm, LANES), F32)],
        name="inproj",
        compiler_params=_params(1),
    )(x, gain, cos, sin, w_packed, head_gains, ones_bd)


def _attn_kernel(d, nb, window, kv_div, use_sink, *refs):
    if use_sink:
        sink_ref, q_ref, kc_ref, kp_ref, vc_ref, vp_ref, o_ref, lse_ref, osc, lsc = refs
    else:
        q_ref, kc_ref, kp_ref, vc_ref, vp_ref, o_ref, lse_ref, osc, lsc = refs
    chunk = pl.program_id(1)
    lane = lax.broadcasted_iota(jnp.int32, (1, LANES), 1)
    qi = lax.broadcasted_iota(jnp.int32, (BLOCK, 2 * BLOCK), 0)
    kj = lax.broadcasted_iota(jnp.int32, (BLOCK, 2 * BLOCK), 1)
    dist = qi + BLOCK - kj
    band = (dist >= 0) & (dist <= window)
    scale = HEAD_DIM ** -0.5
    qmask = [jnp.where((lane // HEAD_DIM) == e, scale, 0.0).astype(BF16) for e in range(2)]

    def unit(u, carry):
        c = u // nb if d > 1 else 0
        j = u % nb if nb > 1 else 0
        row0 = j * BLOCK
        if nb > 1:
            row0 = pl.multiple_of(row0, BLOCK)
            prow = pl.multiple_of(jnp.maximum(j - 1, 0) * BLOCK, BLOCK)
            is_j0 = j == 0
            seq_start = jnp.logical_and(chunk == 0, is_j0)
        else:
            seq_start = chunk == 0
        valid = band & jnp.logical_or(kj >= BLOCK, jnp.logical_not(seq_start))
        lslab = jnp.zeros((BLOCK, LANES), F32)
        for p in range(4):
            qp = q_ref[c, pl.ds(row0, BLOCK), p * LANES:(p + 1) * LANES]
            kcol = (p // kv_div) * LANES
            ksl = slice(kcol, kcol + LANES)

            def window_rows(cur_ref, prev_ref):
                cur = cur_ref[c, pl.ds(row0, BLOCK), ksl]
                prev = prev_ref[c, :, ksl]
                if nb > 1:
                    prev = jnp.where(is_j0, prev, cur_ref[c, pl.ds(prow, BLOCK), ksl])
                return jnp.concatenate([prev, cur], axis=0)

            kwin = window_rows(kc_ref, kp_ref)
            vwin = window_rows(vc_ref, vp_ref)
            outs = []
            for e in range(2):
                s = lax.dot_general(qp * qmask[e], kwin, (((1,), (1,)), ((), ())),
                                    preferred_element_type=F32)
                s = jnp.where(valid, s, -jnp.inf)
                m = jnp.max(s, axis=1, keepdims=True)
                if use_sink:
                    sk = sink_ref[2 * p + e]
                    m = jnp.maximum(m, sk)
                pr = jnp.exp(s - m)
                den = jnp.sum(pr, axis=1, keepdims=True)
                if use_sink:
                    den = den + jnp.exp(sk - m)
                o = jnp.dot(pr.astype(BF16), vwin, preferred_element_type=F32)
                outs.append(o * (1.0 / den))
                lslab = jnp.where(lane == 2 * p + e, m + jnp.log(den), lslab)
            opair = jnp.where(lane < HEAD_DIM, outs[0], outs[1])
            start = row0 * d + c
            rows = pl.ds(start, BLOCK, stride=d) if d > 1 else pl.ds(start, BLOCK)
            osc[p, rows, :] = opair
        lsc[rows, :] = lslab
        return carry

    lax.fori_loop(0, d * nb, unit, 0)
    for p in range(4):
        o_ref[:, p * LANES:(p + 1) * LANES] = osc[p].astype(BF16)
    lse_ref[...] = lsc[...]


def _attention(q, k, v, sinks, *, window, kv_div, batch, seq):
    d = q.shape[1]
    nb = ATTN_CHUNK // (d * BLOCK)
    kw = k.shape[-1]
    rows = nb * BLOCK
    cur = lambda b, i: (b, 0, i, 0)
    prev = lambda b, i: (b, 0, jnp.maximum(i * nb - 1, 0), 0)
    in_specs = [pl.BlockSpec((None, d, rows, A_GROUP_WIDTH), cur),
                pl.BlockSpec((None, d, rows, kw), cur),
                pl.BlockSpec((None, d, BLOCK, kw), prev),
                pl.BlockSpec((None, d, rows, kw), cur),
                pl.BlockSpec((None, d, BLOCK, kw), prev)]
    args = [q, k, k, v, v]
    use_sink = sinks is not None
    if use_sink:
        in_specs = [pl.BlockSpec(memory_space=pltpu.SMEM)] + in_specs
        args = [sinks] + args
    return pl.pallas_call(
        functools.partial(_attn_kernel, d, nb, window, kv_div, use_sink),
        grid=(batch, seq // ATTN_CHUNK),
        in_specs=in_specs,
        out_specs=[pl.BlockSpec((None, ATTN_CHUNK, A_GROUP_WIDTH), lambda b, i: (b, i, 0)),
                   pl.BlockSpec((None, ATTN_CHUNK, LANES), lambda b, i: (b, i, 0))],
        out_shape=[jax.ShapeDtypeStruct((batch, seq, A_GROUP_WIDTH), BF16),
                   jax.ShapeDtypeStruct((batch, seq, LANES), F32)],
        scratch_shapes=[pltpu.VMEM((4, ATTN_CHUNK, LANES), F32),
                        pltpu.VMEM((ATTN_CHUNK, LANES), F32)],
        name=f"attn_d{d}_w{window}",
        compiler_params=_params(2),
    )(*args)


def _gla_kernel(q_ref, k_ref, v_ref, gl_ref, cr_ref, gu_ref, gb_ref, on_ref, tri_ref, y_ref, st_ref, lg_ref):
    tm = q_ref.shape[0]
    width = C_HEADS * LANES

    @pl.when(pl.program_id(1) == 0)
    def _():
        st_ref[...] = jnp.zeros_like(st_ref)

    x = jnp.dot(gl_ref[...], gu_ref[...], preferred_element_type=F32) + gb_ref[...]
    lg_ref[...] = (jnp.minimum(x, 0.0) - jnp.log1p(jnp.exp(-jnp.abs(x)))) * (1.0 / C_TAU)
    tri = tri_ref[...]
    ti = lax.broadcasted_iota(jnp.int32, (C_CHUNK, C_CHUNK), 0)
    si = lax.broadcasted_iota(jnp.int32, (C_CHUNK, C_CHUNK), 1)
    causal = ti >= si
    on_gain = on_ref[...]
    contract_last = (((1,), (1,)), ((), ()))
    contract_first = (((0,), (0,)), ((), ()))

    def chunk(ci, carry):
        rows = pl.ds(pl.multiple_of(ci * C_CHUNK, C_CHUNK), C_CHUNK)
        g = lg_ref[rows, :]
        g1 = g.astype(BF16)
        r1 = g - g1.astype(F32)
        g2 = r1.astype(BF16)
        g3 = (r1 - g2.astype(F32)).astype(BF16)
        bs = jnp.dot(tri, jnp.concatenate([g1, g2, g3], axis=1), preferred_element_type=F32)
        b_all = bs[:, 0:width] + bs[:, width:2 * width] + bs[:, 2 * width:3 * width]
        for hh in range(C_HEADS):
            cols = slice(hh * LANES, (hh + 1) * LANES)
            b = b_all[:, cols]
            b_last = b[C_CHUNK - 1:C_CHUNK, :]
            b_mid = b[C_CHUNK // 2 - 1:C_CHUNK // 2, :]
            qh = q_ref[rows, cols].astype(F32) * (C_DK ** -0.5)
            kh = k_ref[rows, cols].astype(F32)
            vh = v_ref[rows, cols]
            st = st_ref[hh]
            o = lax.dot_general((qh * jnp.exp(b)).astype(BF16), st.astype(BF16), contract_last,
                                preferred_element_type=F32)
            qs = (qh * jnp.exp(b - b_mid)).astype(BF16)
            ks = (kh * jnp.exp(b_mid - b)).astype(BF16)
            att = lax.dot_general(qs, ks, contract_last, preferred_element_type=F32)
            att = jnp.where(causal, att, 0.0).astype(BF16)
            o = o + jnp.dot(att, vh, preferred_element_type=F32)
            kd = (kh * jnp.exp(b_last - b)).astype(BF16)
            st_ref[hh] = st * jnp.exp(b_last) + lax.dot_general(vh, kd, contract_first,
                                                                 preferred_element_type=F32)
            gate = cr_ref[rows, cols].astype(F32)
            y_ref[rows, cols] = (_rms(o, on_gain) * (gate * jax.nn.sigmoid(gate))).astype(BF16)
        return carry

    lax.fori_loop(0, tm // C_CHUNK, chunk, 0)


def _gla(q, k, v, gl, cr, gate_up, gate_bias, out_norm, tri, layer, batch, seq):
    t = q.shape[0]
    tm = TOKEN_TILE
    per_seq = seq // tm
    width = C_HEADS * LANES
    tok = lambda w: pl.BlockSpec((tm, w), lambda b, i: (b * per_seq + i, 0))
    return pl.pallas_call(
        _gla_kernel,
        grid=(batch, per_seq),
        in_specs=[tok(width), tok(width), tok(width), tok(LANES), tok(width),
                  pl.BlockSpec((None, LANES, width), lambda b, i: (layer, 0, 0)),
                  pl.BlockSpec((None, 1, width), lambda b, i: (layer, 0, 0)),
                  pl.BlockSpec((None, 1, C_DV), lambda b, i: (layer, 0, 0)),
                  pl.BlockSpec((C_CHUNK, C_CHUNK), lambda b, i: (0, 0))],
        out_specs=tok(width),
        out_shape=jax.ShapeDtypeStruct((t, width), BF16),
        scratch_shapes=[pltpu.VMEM((C_HEADS, C_DV, LANES), F32),
                        pltpu.VMEM((tm, width), F32)],
        name="gla",
        compiler_params=_params(2),
    )(q, k, v, gl, cr, gate_up, gate_bias, out_norm, tri)


def _merge_kernel(x_ref, gm_ref, o1, o2, o3, l1, l2, l3, yb_ref, yc_ref, wg_ref, wb_ref, wo_ref, ex_ref, out_ref):
    x = x_ref[...]
    h = _rms(x, gm_ref[...]).astype(BF16)
    la, lb, lc = l1[...], l2[...], l3[...]
    top = jnp.maximum(jnp.maximum(la, lb), lc)
    ea, eb, ec = jnp.exp(la - top), jnp.exp(lb - top), jnp.exp(lc - top)
    inv = 1.0 / (ea + eb + ec)
    expand = ex_ref[...]
    ya = None
    for e, o in ((ea, o1), (eb, o2), (ec, o3)):
        w = e * inv
        w_hi = w.astype(BF16)
        w_lo = (w - w_hi.astype(F32)).astype(BF16)
        wx = (jnp.dot(w_hi, expand, preferred_element_type=F32)
              + jnp.dot(w_lo, expand, preferred_element_type=F32))
        term = wx * o[...].astype(F32)
        ya = term if ya is None else ya + term
    merged = None
    for i, y in enumerate((ya.astype(BF16), yb_ref[...], yc_ref[...])):
        gate = jax.nn.sigmoid(jnp.dot(h, wg_ref[:, i * D_MODEL:(i + 1) * D_MODEL], preferred_element_type=F32))
        term = gate * jnp.dot(y, wb_ref[i], preferred_element_type=F32)
        merged = term if merged is None else merged + term
    out_ref[...] = x + jnp.dot(merged.astype(BF16), wo_ref[...], preferred_element_type=F32)


def _merge(x, gain, oa, la, yb, yc, w_gate, w_branch, w_out, expand, layer):
    t = x.shape[0]
    tm = TOKEN_TILE
    tok = lambda w: pl.BlockSpec((tm, w), lambda i: (i, 0))
    bw = A_GROUP_WIDTH
    return pl.pallas_call(
        _merge_kernel,
        grid=(t // tm,),
        in_specs=[tok(D_MODEL),
                  pl.BlockSpec((None, 1, D_MODEL), lambda i: (layer, 0, 0)),
                  tok(bw), tok(bw), tok(bw), tok(LANES), tok(LANES), tok(LANES), tok(bw), tok(bw),
                  _resident((None, D_MODEL, N_BRANCH * D_MODEL), lambda i: (layer, 0, 0)),
                  _resident((None, N_BRANCH, bw, D_MODEL), lambda i: (layer, 0, 0, 0)),
                  _resident((None, D_MODEL, D_MODEL), lambda i: (layer, 0, 0)),
                  pl.BlockSpec((LANES, bw), lambda i: (0, 0))],
        out_specs=tok(D_MODEL),
        out_shape=jax.ShapeDtypeStruct((t, D_MODEL), F32),
        name="merge",
        compiler_params=_params(1),
    )(x, gain, *oa, *la, yb, yc, w_gate, w_branch, w_out, expand)


def _pack_w_in(w_in):
    n_layer = w_in.shape[0]
    lead = w_in.shape[:2]
    edges = np.cumsum([0, 4608, 512, 256, 256, 256, 512, C_GATE_RANK, 512, N_BRANCH * D_MODEL])
    seg = [w_in[:, :, edges[i]:edges[i + 1]] for i in range(9)]
    a_qkv, b_q, b_kv, c_q, c_k, c_v, c_glow, c_r, gates = seg
    a = a_qkv.reshape(lead + (3, 3, A_GROUP_WIDTH)).transpose(0, 1, 3, 2, 4).reshape(lead + (3 * A_SEG,))
    b_kv = b_kv.reshape(lead + (2, 2, 1, HEAD_DIM))
    b_kv = jnp.broadcast_to(b_kv, lead + (2, 2, 2, HEAD_DIM)).reshape(lead + (512,))

    def pad_heads(w):
        w = w.reshape(lead + (C_HEADS, C_DK))
        return jnp.pad(w, ((0, 0), (0, 0), (0, 0), (0, LANES - C_DK))).reshape(lead + (C_HEADS * LANES,))

    c_glow = jnp.pad(c_glow, ((0, 0), (0, 0), (0, LANES - C_GATE_RANK)))
    packed = jnp.concatenate([a, b_q, b_kv, pad_heads(c_q), pad_heads(c_k), c_v, c_glow, c_r], axis=-1)
    assert packed.shape == (n_layer, D_MODEL, W_PACKED)
    return packed.astype(BF16), gates.astype(BF16)


def kernel(x, positions, norm_ffn1, w_ffn1_in, w_ffn1_out, norm_mix, w_in, a_q_norm, a_k_norm, b_q_norm, b_k_norm,
           b_sinks, c_gate_up, c_gate_bias, c_out_norm, w_branch, w_out, norm_ffn2, w_ffn2_in, w_ffn2_out):
    batch, seq, _ = x.shape
    n_layer = w_in.shape[0]
    assert seq % ATTN_CHUNK == 0 and x.shape[2] == D_MODEL

    w_packed, w_gate = _pack_w_in(w_in)
    w1_in, w1_out = w_ffn1_in.astype(BF16), w_ffn1_out.astype(BF16)
    w2_in, w2_out = w_ffn2_in.astype(BF16), w_ffn2_out.astype(BF16)
    wb, wo = w_branch.astype(BF16), w_out.astype(BF16)
    row = lambda g: g[:, None, :]
    head_gains = jnp.stack([jnp.tile(g, (1, LANES // HEAD_DIM)) for g in (a_q_norm, a_k_norm, b_q_norm, b_k_norm)],
                           axis=1)
    gate_up = c_gate_up.reshape(n_layer, C_GATE_RANK, C_HEADS, C_DK)
    gate_up = jnp.pad(gate_up, ((0, 0), (0, LANES - C_GATE_RANK), (0, 0), (0, LANES - C_DK)))
    gate_up = gate_up.reshape(n_layer, LANES, C_HEADS * LANES).astype(BF16)
    gate_bias = jnp.pad(c_gate_bias.reshape(n_layer, C_HEADS, C_DK), ((0, 0), (0, 0), (0, LANES - C_DK)))
    gate_bias = gate_bias.reshape(n_layer, 1, C_HEADS * LANES)

    lane_head = np.arange(LANES) // HEAD_DIM
    ones_bd = jnp.asarray(lane_head[:, None] == lane_head[None, :], BF16)
    expand = jnp.asarray(np.arange(LANES)[:, None] == (np.arange(A_GROUP_WIDTH) // HEAD_DIM)[None, :], BF16)
    tri = jnp.asarray(np.tril(np.ones((C_CHUNK, C_CHUNK))), BF16)

    cos, sin = _rope_tables(positions)
    xt = x.reshape(batch * seq, D_MODEL)
    for l in range(n_layer):
        xt = _ffn(xt, row(norm_ffn1), w1_in, w1_out, l)
        (qa1, ka1, va1, qa2, ka2, va2, qa3, ka3, va3, qb, kb, vb, qc, kc, vc, gl, cr) = _inproj(
            xt, row(norm_mix), cos, sin, w_packed, head_gains, ones_bd, l, batch, seq)
        oa, la = [], []
        for (window, d), (q, k, v) in zip(A_CONFIGS, ((qa1, ka1, va1), (qa2, ka2, va2), (qa3, ka3, va3))):
            o, lse = _attention(q, k, v, None, window=window // d, kv_div=1, batch=batch, seq=seq)
            oa.append(o.reshape(batch * seq, A_GROUP_WIDTH))
            la.append(lse.reshape(batch * seq, LANES))
        yb, _ = _attention(qb, kb, vb, b_sinks[l], window=B_WINDOW - 1, kv_div=2, batch=batch, seq=seq)
        yb = yb.reshape(batch * seq, A_GROUP_WIDTH)
        yc = _gla(qc, kc, vc, gl, cr, gate_up, gate_bias, row(c_out_norm), tri, l, batch, seq)
        xt = _merge(xt, row(norm_mix), oa, la, yb, yc, w_gate, wb, wo, expand, l)
        xt = _ffn(xt, row(norm_ffn2), w2_in, w2_out, l)
    return xt.reshape(batch, seq, D_MODEL)
```

```python
import functools

import numpy as np
import jax
import jax.numpy as jnp
from jax import lax
from jax.experimental import pallas as pl
from jax.experimental.pallas import tpu as pltpu

F32 = jnp.float32
BF16 = jnp.bfloat16

D_MODEL = 1024
D_FF = 2816
HEAD_DIM = 64
NORM_EPS = 1e-6
ROPE_THETA = 10000.0
LOG2E = float(np.log2(np.e))
Q_SCALE = HEAD_DIM ** -0.5 * LOG2E
LANES = 128
BLOCK = 128
A_CONFIGS = ((128, 1), (512, 4), (2048, 16))
A_GROUP_WIDTH = 512
B_WINDOW = 128
C_HEADS = 4
C_DK = 64
C_DV = 128
C_GATE_RANK = 16
C_TAU = 16.0
C_CHUNK = 64
N_BRANCH = 3

A_SEG = 3 * A_GROUP_WIDTH
OFF_A = 0
OFF_B = 3 * A_SEG
B_SEG = 1024
OFF_C = OFF_B + B_SEG
C_SEG = 2176
W_PACKED = OFF_C + C_SEG

TOKEN_TILE = 512
ATTN_CHUNK = 2048
FFN_CHUNK = 1408
VMEM_LIMIT = 56 * 2**20


def _params(n_axes):
    return pltpu.CompilerParams(dimension_semantics=("arbitrary",) * n_axes, vmem_limit_bytes=VMEM_LIMIT)


def _rms(x, gain):
    return x * lax.rsqrt(jnp.mean(x * x, axis=-1, keepdims=True) + NORM_EPS) * gain


def _resident(block, index_map):
    return pl.BlockSpec(block, index_map, pipeline_mode=pl.Buffered(1))


def _rope_kernel(pos_ref, inv_ref, sgn_ref, cos_ref, sin_ref):
    ang = pos_ref[...].astype(F32) * inv_ref[...]
    cos_ref[...] = jnp.cos(ang)
    sin_ref[...] = jnp.sin(ang) * sgn_ref[...]


def _rope_tables(positions):
    t = positions.size
    inv = ROPE_THETA ** (-jnp.arange(0, HEAD_DIM, 2, dtype=F32) / HEAD_DIM)
    inv = jnp.tile(inv, LANES // (HEAD_DIM // 2))[None, :]
    sgn = jnp.tile(jnp.concatenate([-jnp.ones(HEAD_DIM // 2, F32), jnp.ones(HEAD_DIM // 2, F32)]),
                   LANES // HEAD_DIM)[None, :]
    tm = TOKEN_TILE
    tab = jax.ShapeDtypeStruct((t, LANES), F32)
    return pl.pallas_call(
        _rope_kernel,
        grid=(t // tm,),
        in_specs=[pl.BlockSpec((tm, 1), lambda i: (i, 0)),
                  pl.BlockSpec((1, LANES), lambda i: (0, 0)),
                  pl.BlockSpec((1, LANES), lambda i: (0, 0))],
        out_specs=[pl.BlockSpec((tm, LANES), lambda i: (i, 0))] * 2,
        out_shape=[tab, tab],
        name="rope",
        compiler_params=_params(1),
    )(positions.reshape(t, 1), inv, sgn)


def _ffn_kernel(x_ref, g_ref, wg_ref, wu_ref, wo_ref, o_ref):
    x = x_ref[...]
    h = _rms(x, g_ref[...]).astype(BF16)
    acc = None
    for c in range(D_FF // FFN_CHUNK):
        sl = slice(c * FFN_CHUNK, (c + 1) * FFN_CHUNK)
        g = jnp.dot(h, wg_ref[:, sl], preferred_element_type=F32)
        u = jnp.dot(h, wu_ref[:, sl], preferred_element_type=F32)
        a = (g * jax.nn.sigmoid(g) * u).astype(BF16)
        part = jnp.dot(a, wo_ref[sl, :], preferred_element_type=F32)
        acc = part if acc is None else acc + part
    o_ref[...] = x + 0.5 * acc


def _ffn(x, gain, w_in, w_out, layer):
    t = x.shape[0]
    tm = TOKEN_TILE
    return pl.pallas_call(
        _ffn_kernel,
        grid=(t // tm,),
        in_specs=[pl.BlockSpec((tm, D_MODEL), lambda i: (i, 0)),
                  pl.BlockSpec((None, 1, D_MODEL), lambda i: (layer, 0, 0)),
                  _resident((None, D_MODEL, D_FF), lambda i: (layer, 0, 0)),
                  _resident((None, D_MODEL, D_FF), lambda i: (layer, 0, 1)),
                  _resident((None, D_FF, D_MODEL), lambda i: (layer, 0, 0))],
        out_specs=pl.BlockSpec((tm, D_MODEL), lambda i: (i, 0)),
        out_shape=jax.ShapeDtypeStruct((t, D_MODEL), F32),
        name="ffn",
        compiler_params=_params(1),
    )(x, gain, w_in, w_in, w_out)


def _inproj_kernel(x_ref, gm_ref, cos_ref, sin_ref, w_ref, gains_ref, ones_ref,
                   qa1, ka1, va1, qa2, ka2, va2, qa3, ka3, va3, qb, kb, vb, qc, kc, vc, gl, cr,
                   hs_ref, hp_ref, tb_ref):
    tm = x_ref.shape[0]
    n_slab = D_MODEL // LANES
    h = _rms(x_ref[...], gm_ref[...])
    h_bf = h.astype(BF16)
    cos0 = cos_ref[...]
    sin0 = sin_ref[...]
    ones = ones_ref[...]
    gains = gains_ref[...]
    first_half = (lax.broadcasted_iota(jnp.int32, (1, LANES), 1) % HEAD_DIM) < (HEAD_DIM // 2)

    def norm_rope(y, gain, cos, sin):
        ss = jnp.dot((y * y).astype(BF16), ones, preferred_element_type=F32)
        yn = y * lax.rsqrt(ss * (1.0 / HEAD_DIM) + NORM_EPS) * gain
        partner = jnp.where(first_half, pltpu.roll(yn, LANES - HEAD_DIM // 2, 1), pltpu.roll(yn, HEAD_DIM // 2, 1))
        return yn * cos + partner * sin

    for s in range(n_slab):
        hs_ref[s] = h[:, s * LANES:(s + 1) * LANES]
    tb_ref[0] = cos0
    tb_ref[1] = sin0

    def class_major(d):
        if d == 1:
            return h_bf, cos0, sin0
        rows = tm // d
        for c in range(d):
            for s in range(n_slab):
                hp_ref[c * rows:(c + 1) * rows, s * LANES:(s + 1) * LANES] = (
                    hs_ref[s, pl.ds(c, rows, stride=d), :].astype(BF16))
        cos = jnp.concatenate([tb_ref[0, pl.ds(c, rows, stride=d), :] for c in range(d)], axis=0)
        sin = jnp.concatenate([tb_ref[1, pl.ds(c, rows, stride=d), :] for c in range(d)], axis=0)
        return hp_ref[...], cos, sin

    def store_classes(ref, col, val, d):
        rows = tm // d
        for c in range(d):
            ref[c, :, col:col + LANES] = val[c * rows:(c + 1) * rows, :]

    for gi, (qo, ko, vo) in enumerate(((qa1, ka1, va1), (qa2, ka2, va2), (qa3, ka3, va3))):
        d = A_CONFIGS[gi][1]
        hp, cos, sin = class_major(d)
        yq, yk, yv = [jnp.dot(hp, w_ref[:, pl.ds(OFF_A + part * A_SEG + gi * A_GROUP_WIDTH, A_GROUP_WIDTH)],
                              preferred_element_type=F32) for part in range(3)]
        for j in range(A_GROUP_WIDTH // LANES):
            c0 = j * LANES
            store_classes(qo, c0, (norm_rope(yq[:, c0:c0 + LANES], gains[0:1], cos, sin) * Q_SCALE).astype(BF16), d)
            store_classes(ko, c0, norm_rope(yk[:, c0:c0 + LANES], gains[1:2], cos, sin).astype(BF16), d)
            store_classes(vo, c0, yv[:, c0:c0 + LANES].astype(BF16), d)

    y = jnp.dot(h_bf, w_ref[:, OFF_B:OFF_B + B_SEG], preferred_element_type=F32)
    for j in range(4):
        c0 = j * LANES
        qb[0, :, c0:c0 + LANES] = (norm_rope(y[:, c0:c0 + LANES], gains[2:3], cos0, sin0) * Q_SCALE).astype(BF16)
    for j in range(2):
        c0 = j * LANES
        kb[0, :, c0:c0 + LANES] = norm_rope(y[:, 512 + c0:512 + c0 + LANES], gains[3:4], cos0, sin0).astype(BF16)
        vb[0, :, c0:c0 + LANES] = y[:, 768 + c0:768 + c0 + LANES].astype(BF16)

    y = jnp.dot(h_bf, w_ref[:, OFF_C:OFF_C + C_SEG], preferred_element_type=F32)
    qc[...] = y[:, 0:512].astype(BF16)
    kc[...] = y[:, 512:1024].astype(BF16)
    vc[...] = y[:, 1024:1536].astype(BF16)
    gl[...] = y[:, 1536:1664].astype(BF16)
    cr[...] = y[:, 1664:2176].astype(BF16)


def _inproj(x, gain, cos, sin, w_packed, head_gains, ones_bd, layer, batch, seq):
    t = x.shape[0]
    tm = TOKEN_TILE
    tiles_per_seq = seq // tm

    def cm_spec(d, width):
        return pl.BlockSpec((None, d, tm // d, width),
                            lambda i: (i // tiles_per_seq, 0, i % tiles_per_seq, 0))

    def cm_shape(d, width):
        return jax.ShapeDtypeStruct((batch, d, seq // d, width), BF16)

    def tok_spec(width):
        return pl.BlockSpec((tm, width), lambda i: (i, 0))

    out_specs, out_shape = [], []
    for _, d in A_CONFIGS:
        for _ in range(3):
            out_specs.append(cm_spec(d, A_GROUP_WIDTH))
            out_shape.append(cm_shape(d, A_GROUP_WIDTH))
    for width in (512, 256, 256):
        out_specs.append(cm_spec(1, width))
        out_shape.append(cm_shape(1, width))
    for width in (512, 512, 512, 128, 512):
        out_specs.append(tok_spec(width))
        out_shape.append(jax.ShapeDtypeStruct((t, width), BF16))

    return pl.pallas_call(
        _inproj_kernel,
        grid=(t // tm,),
        in_specs=[tok_spec(D_MODEL),
                  pl.BlockSpec((None, 1, D_MODEL), lambda i: (layer, 0, 0)),
                  tok_spec(LANES), tok_spec(LANES),
                  _resident((None, D_MODEL, W_PACKED), lambda i: (layer, 0, 0)),
                  pl.BlockSpec((None, 4, LANES), lambda i: (layer, 0, 0)),
                  pl.BlockSpec((LANES, LANES), lambda i: (0, 0))],
        out_specs=out_specs,
        out_shape=out_shape,
        scratch_shapes=[pltpu.VMEM((D_MODEL // LANES, tm, LANES), F32),
                        pltpu.VMEM((tm, D_MODEL), BF16),
                        pltpu.VMEM((2, tm, LANES), F32)],
        name="inproj",
        compiler_params=_params(1),
    )(x, gain, cos, sin, w_packed, head_gains, ones_bd)


def _attn_kernel(d, nb, window, kv_div, use_sink, *refs):
    if use_sink:
        sink_ref, q_ref, kc_ref, kp_ref, vc_ref, vp_ref, o_ref, lse_ref, osc, lsc, s_scr, p_scr, st_scr = refs
    else:
        q_ref, kc_ref, kp_ref, vc_ref, vp_ref, o_ref, lse_ref, osc, lsc, s_scr, p_scr, st_scr = refs
    chunk = pl.program_id(1)
    lane = lax.broadcasted_iota(jnp.int32, (1, LANES), 1)
    head_of_lane = lane // HEAD_DIM
    key = lax.broadcasted_iota(jnp.int32, (2 * BLOCK, 2 * BLOCK), 0)
    col = lax.broadcasted_iota(jnp.int32, (2 * BLOCK, 2 * BLOCK), 1)
    dist = (col % BLOCK) + BLOCK - key
    band = (dist >= 0) & (dist <= window)
    col_row = lax.broadcasted_iota(jnp.int32, (1, 2 * BLOCK), 1)
    sub8 = lax.broadcasted_iota(jnp.int32, (8, LANES), 0)
    nt = (((1,), (1,)), ((), ()))
    tn = (((0,), (0,)), ((), ()))

    n_units = d * nb

    def locate(u):
        c = u // nb if d > 1 else 0
        j = u % nb if nb > 1 else 0
        if isinstance(j, int):
            return c, j, j * BLOCK, max(j - 1, 0) * BLOCK
        return c, j, pl.multiple_of(j * BLOCK, BLOCK), pl.multiple_of(jnp.maximum(j - 1, 0) * BLOCK, BLOCK)

    def window_rows(cur_ref, prev_ref, u, p):
        c, j, row0, prow = locate(u)
        kcol = (p // kv_div) * LANES
        ksl = slice(kcol, kcol + LANES)
        cur = cur_ref[c, pl.ds(row0, BLOCK), ksl]
        prev = prev_ref[c, :, ksl]
        if nb > 1:
            prev = jnp.where(j == 0, prev, cur_ref[c, pl.ds(prow, BLOCK), ksl])
        return jnp.concatenate([prev, cur], axis=0)

    def scores(u):
        c, j, row0, _ = locate(u)
        seq_start = jnp.logical_and(chunk == 0, j == 0)
        valid = band & jnp.logical_or(key >= BLOCK, jnp.logical_not(seq_start))
        for p in range(4):
            qp = q_ref[c, pl.ds(row0, BLOCK), p * LANES:(p + 1) * LANES]
            q2 = jnp.concatenate([jnp.where(head_of_lane == e, qp, jnp.zeros_like(qp)) for e in range(2)], axis=0)
            s = lax.dot_general(window_rows(kc_ref, kp_ref, u, p), q2, nt,
                                preferred_element_type=F32)
            s_scr[p] = jnp.where(valid, s, -jnp.inf)

    def softmax(u):
        for p in range(4):
            s = s_scr[p]
            m = jnp.max(s, axis=0, keepdims=True)
            if use_sink:
                sk = jnp.where(col_row < BLOCK, sink_ref[2 * p], sink_ref[2 * p + 1]) * LOG2E
                m = jnp.maximum(m, sk)
            pr = jnp.exp2(s - m)
            den = jnp.sum(pr, axis=0, keepdims=True)
            if use_sink:
                den = den + jnp.exp2(sk - m)
            p_scr[p] = pr.astype(BF16)
            st_scr[p, 0:1, :] = m
            st_scr[p, 1:2, :] = den

    def outputs(u):
        c, _, row0, _ = locate(u)
        start = row0 * d + c
        rows = pl.ds(start, BLOCK, stride=d) if d > 1 else pl.ds(start, BLOCK)
        lse8 = jnp.zeros((8, LANES), F32)
        for p in range(4):
            ob = lax.dot_general(window_rows(vc_ref, vp_ref, u, p), p_scr[p], tn,
                                 preferred_element_type=F32)
            m = st_scr[p, 0:1, :]
            den = st_scr[p, 1:2, :]
            inv = 1.0 / den
            o_t = jnp.concatenate([ob[0:HEAD_DIM, 0:BLOCK] * inv[:, 0:BLOCK],
                                   ob[HEAD_DIM:LANES, BLOCK:2 * BLOCK] * inv[:, BLOCK:2 * BLOCK]], axis=0)
            osc[p, rows, :] = o_t.T
            l2 = m + jnp.log2(den)
            lse8 = jnp.where(sub8 == 2 * p, l2[:, 0:BLOCK], lse8)
            lse8 = jnp.where(sub8 == 2 * p + 1, l2[:, BLOCK:2 * BLOCK], lse8)
        lse_t = jnp.concatenate([lse8, jnp.zeros((LANES - 8, LANES), F32)], axis=0)
        lsc[rows, :] = lse_t.T

    scores(0)
    softmax(0)
    scores(1)

    def body(u, carry):
        outputs(u - 2)
        softmax(u - 1)
        scores(u)
        return carry

    lax.fori_loop(2, n_units, body, 0)
    outputs(n_units - 2)
    softmax(n_units - 1)
    outputs(n_units - 1)
    for p in range(4):
        o_ref[:, p * LANES:(p + 1) * LANES] = osc[p].astype(BF16)
    lse_ref[...] = lsc[...]


def _attention(q, k, v, sinks, *, window, kv_div, batch, seq):
    d = q.shape[1]
    nb = ATTN_CHUNK // (d * BLOCK)
    kw = k.shape[-1]
    rows = nb * BLOCK
    cur = lambda b, i: (b, 0, i, 0)
    prev = lambda b, i: (b, 0, jnp.maximum(i * nb - 1, 0), 0)
    in_specs = [pl.BlockSpec((None, d, rows, A_GROUP_WIDTH), cur),
                pl.BlockSpec((None, d, rows, kw), cur),
                pl.BlockSpec((None, d, BLOCK, kw), prev),
                pl.BlockSpec((None, d, rows, kw), cur),
                pl.BlockSpec((None, d, BLOCK, kw), prev)]
    args = [q, k, k, v, v]
    use_sink = sinks is not None
    if use_sink:
        in_specs = [pl.BlockSpec(memory_space=pltpu.SMEM)] + in_specs
        args = [sinks] + args
    return pl.pallas_call(
        functools.partial(_attn_kernel, d, nb, window, kv_div, use_sink),
        grid=(batch, seq // ATTN_CHUNK),
        in_specs=in_specs,
        out_specs=[pl.BlockSpec((None, ATTN_CHUNK, A_GROUP_WIDTH), lambda b, i: (b, i, 0)),
                   pl.BlockSpec((None, ATTN_CHUNK, LANES), lambda b, i: (b, i, 0))],
        out_shape=[jax.ShapeDtypeStruct((batch, seq, A_GROUP_WIDTH), BF16),
                   jax.ShapeDtypeStruct((batch, seq, LANES), F32)],
        scratch_shapes=[pltpu.VMEM((4, ATTN_CHUNK, LANES), F32),
                        pltpu.VMEM((ATTN_CHUNK, LANES), F32),
                        pltpu.VMEM((4, 2 * BLOCK, 2 * BLOCK), F32),
                        pltpu.VMEM((4, 2 * BLOCK, 2 * BLOCK), BF16),
                        pltpu.VMEM((4, 8, 2 * BLOCK), F32)],
        name=f"attn_d{d}_w{window}",
        compiler_params=_params(2),
    )(*args)


def _gla_kernel(q_ref, k_ref, v_ref, gl_ref, cr_ref, gu_ref, gb_ref, on_ref, tri_ref, y_ref, st_ref, lg_ref):
    tm = q_ref.shape[0]
    width = C_HEADS * LANES

    @pl.when(pl.program_id(1) == 0)
    def _():
        st_ref[...] = jnp.zeros_like(st_ref)

    x = jnp.dot(gl_ref[...], gu_ref[...], preferred_element_type=F32) + gb_ref[...]
    lg_ref[...] = (jnp.minimum(x, 0.0) - jnp.log1p(jnp.exp(-jnp.abs(x)))) * (1.0 / C_TAU)
    tri = tri_ref[...]
    ti = lax.broadcasted_iota(jnp.int32, (C_CHUNK, C_CHUNK), 0)
    si = lax.broadcasted_iota(jnp.int32, (C_CHUNK, C_CHUNK), 1)
    causal = ti >= si
    on_gain = on_ref[...]
    contract_last = (((1,), (1,)), ((), ()))
    contract_first = (((0,), (0,)), ((), ()))

    def chunk(ci, carry):
        rows = pl.ds(pl.multiple_of(ci * C_CHUNK, C_CHUNK), C_CHUNK)
        g = lg_ref[rows, :]
        g1 = g.astype(BF16)
        r1 = g - g1.astype(F32)
        g2 = r1.astype(BF16)
        g3 = (r1 - g2.astype(F32)).astype(BF16)
        bs = jnp.dot(tri, jnp.concatenate([g1, g2, g3], axis=1), preferred_element_type=F32)
        b_all = bs[:, 0:width] + bs[:, width:2 * width] + bs[:, 2 * width:3 * width]
        for hh in range(C_HEADS):
            cols = slice(hh * LANES, (hh + 1) * LANES)
            b = b_all[:, cols]
            b_last = b[C_CHUNK - 1:C_CHUNK, :]
            b_mid = b[C_CHUNK // 2 - 1:C_CHUNK // 2, :]
            qh = q_ref[rows, cols].astype(F32) * (C_DK ** -0.5)
            kh = k_ref[rows, cols].astype(F32)
            vh = v_ref[rows, cols]
            st = st_ref[hh]
            o = lax.dot_general((qh * jnp.exp(b)).astype(BF16), st.astype(BF16), contract_last,
                                preferred_element_type=F32)
            qs = (qh * jnp.exp(b - b_mid)).astype(BF16)
            ks = (kh * jnp.exp(b_mid - b)).astype(BF16)
            att = lax.dot_general(qs, ks, contract_last, preferred_element_type=F32)
            att = jnp.where(causal, att, 0.0).astype(BF16)
            o = o + jnp.dot(att, vh, preferred_element_type=F32)
            kd = (kh * jnp.exp(b_last - b)).astype(BF16)
            st_ref[hh] = st * jnp.exp(b_last) + lax.dot_general(vh, kd, contract_first,
                                                                 preferred_element_type=F32)
            gate = cr_ref[rows, cols].astype(F32)
            y_ref[rows, cols] = (_rms(o, on_gain) * (gate * jax.nn.sigmoid(gate))).astype(BF16)
        return carry

    lax.fori_loop(0, tm // C_CHUNK, chunk, 0)


def _gla(q, k, v, gl, cr, gate_up, gate_bias, out_norm, tri, layer, batch, seq):
    t = q.shape[0]
    tm = TOKEN_TILE
    per_seq = seq // tm
    width = C_HEADS * LANES
    tok = lambda w: pl.BlockSpec((tm, w), lambda b, i: (b * per_seq + i, 0))
    return pl.pallas_call(
        _gla_kernel,
        grid=(batch, per_seq),
        in_specs=[tok(width), tok(width), tok(width), tok(LANES), tok(width),
                  pl.BlockSpec((None, LANES, width), lambda b, i: (layer, 0, 0)),
                  pl.BlockSpec((None, 1, width), lambda b, i: (layer, 0, 0)),
                  pl.BlockSpec((None, 1, C_DV), lambda b, i: (layer, 0, 0)),
                  pl.BlockSpec((C_CHUNK, C_CHUNK), lambda b, i: (0, 0))],
        out_specs=tok(width),
        out_shape=jax.ShapeDtypeStruct((t, width), BF16),
        scratch_shapes=[pltpu.VMEM((C_HEADS, C_DV, LANES), F32),
                        pltpu.VMEM((tm, width), F32)],
        name="gla",
        compiler_params=_params(2),
    )(q, k, v, gl, cr, gate_up, gate_bias, out_norm, tri)


def _merge_kernel(x_ref, gm_ref, o1, o2, o3, l1, l2, l3, yb_ref, yc_ref, wg_ref, wb_ref, wo_ref, ex_ref, out_ref):
    x = x_ref[...]
    h = _rms(x, gm_ref[...]).astype(BF16)
    la, lb, lc = l1[...], l2[...], l3[...]
    top = jnp.maximum(jnp.maximum(la, lb), lc)
    ea, eb, ec = jnp.exp2(la - top), jnp.exp2(lb - top), jnp.exp2(lc - top)
    inv = 1.0 / (ea + eb + ec)
    expand = ex_ref[...]
    ya = None
    for e, o in ((ea, o1), (eb, o2), (ec, o3)):
        w = e * inv
        w_hi = w.astype(BF16)
        w_lo = (w - w_hi.astype(F32)).astype(BF16)
        wx = (jnp.dot(w_hi, expand, preferred_element_type=F32)
              + jnp.dot(w_lo, expand, preferred_element_type=F32))
        term = wx * o[...].astype(F32)
        ya = term if ya is None else ya + term
    merged = None
    for i, y in enumerate((ya.astype(BF16), yb_ref[...], yc_ref[...])):
        gate = jax.nn.sigmoid(jnp.dot(h, wg_ref[:, i * D_MODEL:(i + 1) * D_MODEL], preferred_element_type=F32))
        term = gate * jnp.dot(y, wb_ref[i], preferred_element_type=F32)
        merged = term if merged is None else merged + term
    out_ref[...] = x + jnp.dot(merged.astype(BF16), wo_ref[...], preferred_element_type=F32)


def _merge(x, gain, oa, la, yb, yc, w_gate, w_branch, w_out, expand, layer):
    t = x.shape[0]
    tm = TOKEN_TILE
    tok = lambda w: pl.BlockSpec((tm, w), lambda i: (i, 0))
    bw = A_GROUP_WIDTH
    return pl.pallas_call(
        _merge_kernel,
        grid=(t // tm,),
        in_specs=[tok(D_MODEL),
                  pl.BlockSpec((None, 1, D_MODEL), lambda i: (layer, 0, 0)),
                  tok(bw), tok(bw), tok(bw), tok(LANES), tok(LANES), tok(LANES), tok(bw), tok(bw),
                  _resident((None, D_MODEL, N_BRANCH * D_MODEL), lambda i: (layer, 0, 0)),
                  _resident((None, N_BRANCH, bw, D_MODEL), lambda i: (layer, 0, 0, 0)),
                  _resident((None, D_MODEL, D_MODEL), lambda i: (layer, 0, 0)),
                  pl.BlockSpec((LANES, bw), lambda i: (0, 0))],
        out_specs=tok(D_MODEL),
        out_shape=jax.ShapeDtypeStruct((t, D_MODEL), F32),
        name="merge",
        compiler_params=_params(1),
    )(x, gain, *oa, *la, yb, yc, w_gate, w_branch, w_out, expand)


def _pack_w_in(w_in):
    n_layer = w_in.shape[0]
    lead = w_in.shape[:2]
    edges = np.cumsum([0, 4608, 512, 256, 256, 256, 512, C_GATE_RANK, 512, N_BRANCH * D_MODEL])
    seg = [w_in[:, :, edges[i]:edges[i + 1]] for i in range(9)]
    a, b_q, b_kv, c_q, c_k, c_v, c_glow, c_r, gates = seg
    b_kv = b_kv.reshape(lead + (2, 2, 1, HEAD_DIM))
    b_kv = jnp.broadcast_to(b_kv, lead + (2, 2, 2, HEAD_DIM)).reshape(lead + (512,))

    def pad_heads(w):
        w = w.reshape(lead + (C_HEADS, C_DK))
        return jnp.pad(w, ((0, 0), (0, 0), (0, 0), (0, LANES - C_DK))).reshape(lead + (C_HEADS * LANES,))

    c_glow = jnp.pad(c_glow, ((0, 0), (0, 0), (0, LANES - C_GATE_RANK)))
    packed = jnp.concatenate([a, b_q, b_kv, pad_heads(c_q), pad_heads(c_k), c_v, c_glow, c_r], axis=-1)
    assert packed.shape == (n_layer, D_MODEL, W_PACKED)
    return packed.astype(BF16), gates.astype(BF16)


def kernel(x, positions, norm_ffn1, w_ffn1_in, w_ffn1_out, norm_mix, w_in, a_q_norm, a_k_norm, b_q_norm, b_k_norm,
           b_sinks, c_gate_up, c_gate_bias, c_out_norm, w_branch, w_out, norm_ffn2, w_ffn2_in, w_ffn2_out):
    batch, seq, _ = x.shape
    n_layer = w_in.shape[0]
    assert seq % ATTN_CHUNK == 0 and x.shape[2] == D_MODEL

    w_packed, w_gate = _pack_w_in(w_in)
    w1_in, w1_out = w_ffn1_in.astype(BF16), w_ffn1_out.astype(BF16)
    w2_in, w2_out = w_ffn2_in.astype(BF16), w_ffn2_out.astype(BF16)
    wb, wo = w_branch.astype(BF16), w_out.astype(BF16)
    row = lambda g: g[:, None, :]
    head_gains = jnp.stack([jnp.tile(g, (1, LANES // HEAD_DIM)) for g in (a_q_norm, a_k_norm, b_q_norm, b_k_norm)],
                           axis=1)
    gate_up = c_gate_up.reshape(n_layer, C_GATE_RANK, C_HEADS, C_DK)
    gate_up = jnp.pad(gate_up, ((0, 0), (0, LANES - C_GATE_RANK), (0, 0), (0, LANES - C_DK)))
    gate_up = gate_up.reshape(n_layer, LANES, C_HEADS * LANES).astype(BF16)
    gate_bias = jnp.pad(c_gate_bias.reshape(n_layer, C_HEADS, C_DK), ((0, 0), (0, 0), (0, LANES - C_DK)))
    gate_bias = gate_bias.reshape(n_layer, 1, C_HEADS * LANES)

    lane_head = np.arange(LANES) // HEAD_DIM
    ones_bd = jnp.asarray(lane_head[:, None] == lane_head[None, :], BF16)
    expand = jnp.asarray(np.arange(LANES)[:, None] == (np.arange(A_GROUP_WIDTH) // HEAD_DIM)[None, :], BF16)
    tri = jnp.asarray(np.tril(np.ones((C_CHUNK, C_CHUNK))), BF16)

    cos, sin = _rope_tables(positions)
    xt = x.reshape(batch * seq, D_MODEL)
    for l in range(n_layer):
        xt = _ffn(xt, row(norm_ffn1), w1_in, w1_out, l)
        (qa1, ka1, va1, qa2, ka2, va2, qa3, ka3, va3, qb, kb, vb, qc, kc, vc, gl, cr) = _inproj(
            xt, row(norm_mix), cos, sin, w_packed, head_gains, ones_bd, l, batch, seq)
        oa, la = [], []
        for (window, d), (q, k, v) in zip(A_CONFIGS, ((qa1, ka1, va1), (qa2, ka2, va2), (qa3, ka3, va3))):
            o, lse = _attention(q, k, v, None, window=window // d, kv_div=1, batch=batch, seq=seq)
            oa.append(o.reshape(batch * seq, A_GROUP_WIDTH))
            la.append(lse.reshape(batch * seq, LANES))
        yb, _ = _attention(qb, kb, vb, b_sinks[l], window=B_WINDOW - 1, kv_div=2, batch=batch, seq=seq)
        yb = yb.reshape(batch * seq, A_GROUP_WIDTH)
        yc = _gla(qc, kc, vc, gl, cr, gate_up, gate_bias, row(c_out_norm), tri, l, batch, seq)
        xt = _merge(xt, row(norm_mix), oa, la, yb, yc, w_gate, wb, wo, expand, l)
        xt = _ffn(xt, row(norm_ffn2), w2_in, w2_out, l)
    return xt.reshape(batch, seq, D_MODEL)
```

```python
import functools

import numpy as np
import jax
import jax.numpy as jnp
from jax import lax
from jax.experimental import pallas as pl
from jax.experimental.pallas import tpu as pltpu

F32 = jnp.float32
BF16 = jnp.bfloat16

D_MODEL = 1024
D_FF = 2816
HEAD_DIM = 64
NORM_EPS = 1e-6
ROPE_THETA = 10000.0
LOG2E = float(np.log2(np.e))
Q_SCALE = HEAD_DIM ** -0.5 * LOG2E
LANES = 128
BLOCK = 128
A_CONFIGS = ((128, 1), (512, 4), (2048, 16))
A_GROUP_WIDTH = 512
B_WINDOW = 128
C_HEADS = 4
C_DK = 64
C_DV = 128
C_GATE_RANK = 16
C_TAU = 16.0
C_CHUNK = 64
GLA_SUB = 256
N_BRANCH = 3

A_SEG = 3 * A_GROUP_WIDTH
OFF_A = 0
OFF_B = 3 * A_SEG
B_SEG = 1024
OFF_C = OFF_B + B_SEG
C_SEG = 1664
W_PACKED = OFF_C + C_SEG

TOKEN_TILE = 512
ATTN_CHUNK = 2048
FFN_CHUNK = 1408
VMEM_LIMIT = 56 * 2**20


def _params(n_axes):
    return pltpu.CompilerParams(dimension_semantics=("arbitrary",) * n_axes, vmem_limit_bytes=VMEM_LIMIT)


def _rms(x, gain):
    return x * lax.rsqrt(jnp.mean(x * x, axis=-1, keepdims=True) + NORM_EPS) * gain


def _resident(block, index_map):
    return pl.BlockSpec(block, index_map, pipeline_mode=pl.Buffered(1))


def _rope_kernel(pos_ref, inv_ref, sgn_ref, cos_ref, sin_ref):
    ang = pos_ref[...].astype(F32) * inv_ref[...]
    cos_ref[...] = jnp.cos(ang)
    sin_ref[...] = jnp.sin(ang) * sgn_ref[...]


def _rope_tables(positions):
    t = positions.size
    inv = ROPE_THETA ** (-jnp.arange(0, HEAD_DIM, 2, dtype=F32) / HEAD_DIM)
    inv = jnp.tile(inv, LANES // (HEAD_DIM // 2))[None, :]
    sgn = jnp.tile(jnp.concatenate([-jnp.ones(HEAD_DIM // 2, F32), jnp.ones(HEAD_DIM // 2, F32)]),
                   LANES // HEAD_DIM)[None, :]
    tm = TOKEN_TILE
    tab = jax.ShapeDtypeStruct((t, LANES), F32)
    return pl.pallas_call(
        _rope_kernel,
        grid=(t // tm,),
        in_specs=[pl.BlockSpec((tm, LANES), lambda i: (i, 0)),
                  pl.BlockSpec((1, LANES), lambda i: (0, 0)),
                  pl.BlockSpec((1, LANES), lambda i: (0, 0))],
        out_specs=[pl.BlockSpec((tm, LANES), lambda i: (i, 0))] * 2,
        out_shape=[tab, tab],
        name="rope",
        compiler_params=_params(1),
    )(jnp.broadcast_to(positions.reshape(t, 1), (t, LANES)), inv, sgn)


def _ffn_kernel(x_ref, g_ref, wg_ref, wu_ref, wo_ref, o_ref):
    x = x_ref[...]
    h = _rms(x, g_ref[...]).astype(BF16)
    acc = None
    for c in range(D_FF // FFN_CHUNK):
        sl = slice(c * FFN_CHUNK, (c + 1) * FFN_CHUNK)
        g = jnp.dot(h, wg_ref[:, sl], preferred_element_type=F32)
        u = jnp.dot(h, wu_ref[:, sl], preferred_element_type=F32)
        a = (g * jax.nn.sigmoid(g) * u).astype(BF16)
        part = jnp.dot(a, wo_ref[sl, :], preferred_element_type=F32)
        acc = part if acc is None else acc + part
    o_ref[...] = x + 0.5 * acc


def _ffn(x, gain, w_in, w_out, layer):
    t = x.shape[0]
    tm = TOKEN_TILE
    return pl.pallas_call(
        _ffn_kernel,
        grid=(t // tm,),
        in_specs=[pl.BlockSpec((tm, D_MODEL), lambda i: (i, 0)),
                  pl.BlockSpec((None, 1, D_MODEL), lambda i: (layer, 0, 0)),
                  _resident((None, D_MODEL, D_FF), lambda i: (layer, 0, 0)),
                  _resident((None, D_MODEL, D_FF), lambda i: (layer, 0, 1)),
                  _resident((None, D_FF, D_MODEL), lambda i: (layer, 0, 0))],
        out_specs=pl.BlockSpec((tm, D_MODEL), lambda i: (i, 0)),
        out_shape=jax.ShapeDtypeStruct((t, D_MODEL), F32),
        name="ffn",
        compiler_params=_params(1),
    )(x, gain, w_in, w_in, w_out)


def _inproj_kernel(x_ref, gm_ref, cos_ref, sin_ref, w_ref, gains_ref, ones_ref,
                   qa1, ka1, va1, qa2, ka2, va2, qa3, ka3, va3, qb, kb, vb, qc, kc, vc, gl, cr,
                   hs_ref, hp_ref, tb_ref):
    tm = x_ref.shape[0]
    n_slab = D_MODEL // LANES
    h = _rms(x_ref[...], gm_ref[...])
    h_bf = h.astype(BF16)
    cos0 = cos_ref[...]
    sin0 = sin_ref[...]
    ones = ones_ref[...]
    gains = gains_ref[...]
    first_half = (lax.broadcasted_iota(jnp.int32, (1, LANES), 1) % HEAD_DIM) < (HEAD_DIM // 2)

    def norm_rope(y, gain, cos, sin):
        ss = jnp.dot((y * y).astype(BF16), ones, preferred_element_type=F32)
        yn = y * lax.rsqrt(ss * (1.0 / HEAD_DIM) + NORM_EPS) * gain
        partner = jnp.where(first_half, pltpu.roll(yn, LANES - HEAD_DIM // 2, 1), pltpu.roll(yn, HEAD_DIM // 2, 1))
        return yn * cos + partner * sin

    for s in range(n_slab):
        hs_ref[s] = h[:, s * LANES:(s + 1) * LANES]
    tb_ref[0] = cos0
    tb_ref[1] = sin0

    def class_major(d):
        if d == 1:
            return h_bf, cos0, sin0
        rows = tm // d
        for c in range(d):
            for s in range(n_slab):
                hp_ref[c * rows:(c + 1) * rows, s * LANES:(s + 1) * LANES] = (
                    hs_ref[s, pl.ds(c, rows, stride=d), :].astype(BF16))
        cos = jnp.concatenate([tb_ref[0, pl.ds(c, rows, stride=d), :] for c in range(d)], axis=0)
        sin = jnp.concatenate([tb_ref[1, pl.ds(c, rows, stride=d), :] for c in range(d)], axis=0)
        return hp_ref[...], cos, sin

    def store_classes(ref, col, val, d):
        rows = tm // d
        for c in range(d):
            ref[c, :, col:col + LANES] = val[c * rows:(c + 1) * rows, :]

    for gi, (qo, ko, vo) in enumerate(((qa1, ka1, va1), (qa2, ka2, va2), (qa3, ka3, va3))):
        d = A_CONFIGS[gi][1]
        hp, cos, sin = class_major(d)
        yq, yk, yv = [jnp.dot(hp, w_ref[:, pl.ds(OFF_A + part * A_SEG + gi * A_GROUP_WIDTH, A_GROUP_WIDTH)],
                              preferred_element_type=F32) for part in range(3)]
        for j in range(A_GROUP_WIDTH // LANES):
            c0 = j * LANES
            store_classes(qo, c0, (norm_rope(yq[:, c0:c0 + LANES], gains[0:1], cos, sin) * Q_SCALE).astype(BF16), d)
            store_classes(ko, c0, norm_rope(yk[:, c0:c0 + LANES], gains[1:2], cos, sin).astype(BF16), d)
            store_classes(vo, c0, yv[:, c0:c0 + LANES].astype(BF16), d)

    y = jnp.dot(h_bf, w_ref[:, OFF_B:OFF_B + B_SEG], preferred_element_type=F32)
    for j in range(4):
        c0 = j * LANES
        qb[0, :, c0:c0 + LANES] = (norm_rope(y[:, c0:c0 + LANES], gains[2:3], cos0, sin0) * Q_SCALE).astype(BF16)
    for j in range(2):
        c0 = j * LANES
        kb[0, :, c0:c0 + LANES] = norm_rope(y[:, 512 + c0:512 + c0 + LANES], gains[3:4], cos0, sin0).astype(BF16)
        vb[0, :, c0:c0 + LANES] = y[:, 768 + c0:768 + c0 + LANES].astype(BF16)

    y = jnp.dot(h_bf, w_ref[:, OFF_C:OFF_C + C_SEG], preferred_element_type=F32)
    qc[...] = y[:, 0:256].astype(BF16)
    kc[...] = y[:, 256:512].astype(BF16)
    vc[...] = y[:, 512:1024].astype(BF16)
    gl[...] = y[:, 1024:1152].astype(BF16)
    cr[...] = y[:, 1152:1664].astype(BF16)


def _inproj(x, gain, cos, sin, w_packed, head_gains, ones_bd, layer, batch, seq):
    t = x.shape[0]
    tm = TOKEN_TILE
    tiles_per_seq = seq // tm

    def cm_spec(d, width):
        return pl.BlockSpec((None, d, tm // d, width),
                            lambda i: (i // tiles_per_seq, 0, i % tiles_per_seq, 0))

    def cm_shape(d, width):
        return jax.ShapeDtypeStruct((batch, d, seq // d, width), BF16)

    def tok_spec(width):
        return pl.BlockSpec((tm, width), lambda i: (i, 0))

    out_specs, out_shape = [], []
    for _, d in A_CONFIGS:
        for _ in range(3):
            out_specs.append(cm_spec(d, A_GROUP_WIDTH))
            out_shape.append(cm_shape(d, A_GROUP_WIDTH))
    for width in (512, 256, 256):
        out_specs.append(cm_spec(1, width))
        out_shape.append(cm_shape(1, width))
    for width in (256, 256, 512, 128, 512):
        out_specs.append(tok_spec(width))
        out_shape.append(jax.ShapeDtypeStruct((t, width), BF16))

    return pl.pallas_call(
        _inproj_kernel,
        grid=(t // tm,),
        in_specs=[tok_spec(D_MODEL),
                  pl.BlockSpec((None, 1, D_MODEL), lambda i: (layer, 0, 0)),
                  tok_spec(LANES), tok_spec(LANES),
                  _resident((None, D_MODEL, W_PACKED), lambda i: (layer, 0, 0)),
                  pl.BlockSpec((None, 4, LANES), lambda i: (layer, 0, 0)),
                  pl.BlockSpec((LANES, LANES), lambda i: (0, 0))],
        out_specs=out_specs,
        out_shape=out_shape,
        scratch_shapes=[pltpu.VMEM((D_MODEL // LANES, tm, LANES), F32),
                        pltpu.VMEM((tm, D_MODEL), BF16),
                        pltpu.VMEM((2, tm, LANES), F32)],
        name="inproj",
        compiler_params=_params(1),
    )(x, gain, cos, sin, w_packed, head_gains, ones_bd)


def _attn_kernel(d, nb, window, kv_div, use_sink, *refs):
    if use_sink:
        sink_ref, q_ref, kc_ref, kp_ref, vc_ref, vp_ref, o_ref, lse_ref, osc, lsc, s_scr, p_scr, st_scr = refs
    else:
        q_ref, kc_ref, kp_ref, vc_ref, vp_ref, o_ref, lse_ref, osc, lsc, s_scr, p_scr, st_scr = refs
    chunk = pl.program_id(1)
    lane = lax.broadcasted_iota(jnp.int32, (1, LANES), 1)
    head_of_lane = lane // HEAD_DIM
    key = lax.broadcasted_iota(jnp.int32, (2 * BLOCK, 2 * BLOCK), 0)
    col = lax.broadcasted_iota(jnp.int32, (2 * BLOCK, 2 * BLOCK), 1)
    dist = (col % BLOCK) + BLOCK - key
    band = (dist >= 0) & (dist <= window)
    col_row = lax.broadcasted_iota(jnp.int32, (1, 2 * BLOCK), 1)
    sub8 = lax.broadcasted_iota(jnp.int32, (8, LANES), 0)
    nt = (((1,), (1,)), ((), ()))
    tn = (((0,), (0,)), ((), ()))

    n_units = d * nb

    def locate(u):
        c = u // nb if d > 1 else 0
        j = u % nb if nb > 1 else 0
        if isinstance(j, int):
            return c, j, j * BLOCK, max(j - 1, 0) * BLOCK
        return c, j, pl.multiple_of(j * BLOCK, BLOCK), pl.multiple_of(jnp.maximum(j - 1, 0) * BLOCK, BLOCK)

    def window_rows(cur_ref, prev_ref, u, p):
        c, j, row0, prow = locate(u)
        kcol = (p // kv_div) * LANES
        ksl = slice(kcol, kcol + LANES)
        cur = cur_ref[c, pl.ds(row0, BLOCK), ksl]
        prev = prev_ref[c, :, ksl]
        if nb > 1:
            prev = jnp.where(j == 0, prev, cur_ref[c, pl.ds(prow, BLOCK), ksl])
        return jnp.concatenate([prev, cur], axis=0)

    def scores(u):
        c, j, row0, _ = locate(u)
        seq_start = jnp.logical_and(chunk == 0, j == 0)
        valid = band & jnp.logical_or(key >= BLOCK, jnp.logical_not(seq_start))
        for p in range(4):
            qp = q_ref[c, pl.ds(row0, BLOCK), p * LANES:(p + 1) * LANES]
            q2 = jnp.concatenate([jnp.where(head_of_lane == e, qp, jnp.zeros_like(qp)) for e in range(2)], axis=0)
            s = lax.dot_general(window_rows(kc_ref, kp_ref, u, p), q2, nt,
                                preferred_element_type=F32)
            s_scr[p] = jnp.where(valid, s, -jnp.inf)

    def softmax(u):
        for p in range(4):
            s = s_scr[p]
            m = jnp.max(s, axis=0, keepdims=True)
            if use_sink:
                sk = jnp.where(col_row < BLOCK, sink_ref[2 * p], sink_ref[2 * p + 1]) * LOG2E
                m = jnp.maximum(m, sk)
            pr = jnp.exp2(s - m)
            den = jnp.sum(pr, axis=0, keepdims=True)
            if use_sink:
                den = den + jnp.exp2(sk - m)
            p_scr[p] = pr.astype(BF16)
            st_scr[p, 0:1, :] = m
            st_scr[p, 1:2, :] = den

    def outputs(u):
        c, _, row0, _ = locate(u)
        start = row0 * d + c
        rows = pl.ds(start, BLOCK, stride=d) if d > 1 else pl.ds(start, BLOCK)
        lse8 = jnp.zeros((8, LANES), F32)
        for p in range(4):
            ob = lax.dot_general(window_rows(vc_ref, vp_ref, u, p), p_scr[p], tn,
                                 preferred_element_type=F32)
            m = st_scr[p, 0:1, :]
            den = st_scr[p, 1:2, :]
            inv = 1.0 / den
            o_t = jnp.concatenate([ob[0:HEAD_DIM, 0:BLOCK] * inv[:, 0:BLOCK],
                                   ob[HEAD_DIM:LANES, BLOCK:2 * BLOCK] * inv[:, BLOCK:2 * BLOCK]], axis=0)
            osc[p, rows, :] = o_t.T
            l2 = m + jnp.log2(den)
            lse8 = jnp.where(sub8 == 2 * p, l2[:, 0:BLOCK], lse8)
            lse8 = jnp.where(sub8 == 2 * p + 1, l2[:, BLOCK:2 * BLOCK], lse8)
        lse_t = jnp.concatenate([lse8, jnp.zeros((LANES - 8, LANES), F32)], axis=0)
        lsc[rows, :] = lse_t.T

    scores(0)
    softmax(0)
    scores(1)

    def body(u, carry):
        outputs(u - 2)
        softmax(u - 1)
        scores(u)
        return carry

    lax.fori_loop(2, n_units, body, 0)
    outputs(n_units - 2)
    softmax(n_units - 1)
    outputs(n_units - 1)
    for p in range(4):
        o_ref[:, p * LANES:(p + 1) * LANES] = osc[p].astype(BF16)
    lse_ref[...] = lsc[...]


def _attention(q, k, v, sinks, *, window, kv_div, batch, seq):
    d = q.shape[1]
    nb = ATTN_CHUNK // (d * BLOCK)
    kw = k.shape[-1]
    rows = nb * BLOCK
    cur = lambda b, i: (b, 0, i, 0)
    prev = lambda b, i: (b, 0, jnp.maximum(i * nb - 1, 0), 0)
    in_specs = [pl.BlockSpec((None, d, rows, A_GROUP_WIDTH), cur),
                pl.BlockSpec((None, d, rows, kw), cur),
                pl.BlockSpec((None, d, BLOCK, kw), prev),
                pl.BlockSpec((None, d, rows, kw), cur),
                pl.BlockSpec((None, d, BLOCK, kw), prev)]
    args = [q, k, k, v, v]
    use_sink = sinks is not None
    if use_sink:
        in_specs = [pl.BlockSpec(memory_space=pltpu.SMEM)] + in_specs
        args = [sinks] + args
    return pl.pallas_call(
        functools.partial(_attn_kernel, d, nb, window, kv_div, use_sink),
        grid=(batch, seq // ATTN_CHUNK),
        in_specs=in_specs,
        out_specs=[pl.BlockSpec((None, ATTN_CHUNK, A_GROUP_WIDTH), lambda b, i: (b, i, 0)),
                   pl.BlockSpec((None, ATTN_CHUNK, LANES), lambda b, i: (b, i, 0))],
        out_shape=[jax.ShapeDtypeStruct((batch, seq, A_GROUP_WIDTH), BF16),
                   jax.ShapeDtypeStruct((batch, seq, LANES), F32)],
        scratch_shapes=[pltpu.VMEM((4, ATTN_CHUNK, LANES), F32),
                        pltpu.VMEM((ATTN_CHUNK, LANES), F32),
                        pltpu.VMEM((4, 2 * BLOCK, 2 * BLOCK), F32),
                        pltpu.VMEM((4, 2 * BLOCK, 2 * BLOCK), BF16),
                        pltpu.VMEM((4, 8, 2 * BLOCK), F32)],
        name=f"attn_d{d}_w{window}",
        compiler_params=_params(2),
    )(*args)


def _gla_kernel(q_ref, k_ref, v_ref, gl_ref, cr_ref, gu_ref, gb_ref, on_ref, tri_ref, y_ref, st_ref):
    tm = q_ref.shape[0]
    width = C_HEADS * C_DK
    n_chunk = GLA_SUB // C_CHUNK

    @pl.when(pl.program_id(1) == 0)
    def _():
        st_ref[...] = jnp.zeros_like(st_ref)

    tri = tri_ref[...]
    keep = tri > 0
    on_gain = on_ref[...]
    head_of_lane = lax.broadcasted_iota(jnp.int32, (1, LANES), 1) // C_DK
    nt = (((1,), (1,)), ((), ()))
    tn = (((0,), (0,)), ((), ()))

    def chunked(a):
        return a.reshape(n_chunk, C_CHUNK, width)

    for sub in range(tm // GLA_SUB):
        rows = slice(sub * GLA_SUB, (sub + 1) * GLA_SUB)
        x = jnp.dot(gl_ref[rows, :], gu_ref[...], preferred_element_type=F32) + gb_ref[...]
        lg = (jnp.minimum(x, 0.0) - jnp.log(1.0 + jnp.exp(-jnp.abs(x)))) * (1.0 / C_TAU)
        g1 = lg.astype(BF16)
        g2 = (lg - g1.astype(F32)).astype(BF16)
        bs = jnp.dot(tri, jnp.concatenate([g1, g2], axis=1), preferred_element_type=F32)
        b = chunked(bs[:, 0:width] + bs[:, width:2 * width])
        b_mid = b[:, C_CHUNK // 2 - 1:C_CHUNK // 2, :]
        b_last = b[:, C_CHUNK - 1:C_CHUNK, :]
        q = chunked(q_ref[rows, :].astype(F32) * (C_DK ** -0.5))
        k = chunked(k_ref[rows, :].astype(F32))
        v = v_ref[rows, :]
        qd = (q * jnp.exp(b)).astype(BF16)
        qs = (q * jnp.exp(b - b_mid)).astype(BF16)
        ks = (k * jnp.exp(b_mid - b)).astype(BF16)
        kd = (k * jnp.exp(b_last - b)).astype(BF16)
        decay = jnp.exp(b_last)
        for hh in range(C_HEADS):
            cols = slice(hh * LANES, (hh + 1) * LANES)
            kcols = slice((hh // 2) * LANES, (hh // 2 + 1) * LANES)
            mine = head_of_lane == (hh % 2)
            qs_h = jnp.where(mine, qs[:, :, kcols], jnp.zeros((), BF16))
            qd_h = jnp.where(mine, qd[:, :, kcols], jnp.zeros((), BF16))
            att = lax.dot_general(qs_h.reshape(GLA_SUB, LANES), ks[:, :, kcols].reshape(GLA_SUB, LANES),
                                  nt, preferred_element_type=F32)
            att = jnp.where(keep, att, 0.0).astype(BF16)
            o = jnp.dot(att, v[:, cols], preferred_element_type=F32)
            st = st_ref[hh]
            inter = []
            for c in range(n_chunk):
                inter.append(lax.dot_general(qd_h[c], st.astype(BF16), nt, preferred_element_type=F32))
                inc = lax.dot_general(v[c * C_CHUNK:(c + 1) * C_CHUNK, cols], kd[c, :, kcols], tn,
                                      preferred_element_type=F32)
                st = st * decay[c, :, kcols] + inc
            st_ref[hh] = st
            o = o + jnp.concatenate(inter, axis=0)
            gate = cr_ref[rows, cols].astype(F32)
            y_ref[rows, cols] = (_rms(o, on_gain) * (gate * jax.nn.sigmoid(gate))).astype(BF16)


def _gla(q, k, v, gl, cr, gate_up, gate_bias, out_norm, tri, layer, batch, seq):
    t = q.shape[0]
    tm = TOKEN_TILE
    per_seq = seq // tm
    width = C_HEADS * C_DV
    kwidth = C_HEADS * C_DK
    tok = lambda w: pl.BlockSpec((tm, w), lambda b, i: (b * per_seq + i, 0))
    return pl.pallas_call(
        _gla_kernel,
        grid=(batch, per_seq),
        in_specs=[tok(kwidth), tok(kwidth), tok(width), tok(LANES), tok(width),
                  pl.BlockSpec((None, LANES, kwidth), lambda b, i: (layer, 0, 0)),
                  pl.BlockSpec((None, 1, kwidth), lambda b, i: (layer, 0, 0)),
                  pl.BlockSpec((None, 1, C_DV), lambda b, i: (layer, 0, 0)),
                  pl.BlockSpec((GLA_SUB, GLA_SUB), lambda b, i: (0, 0))],
        out_specs=tok(width),
        out_shape=jax.ShapeDtypeStruct((t, width), BF16),
        scratch_shapes=[pltpu.VMEM((C_HEADS, C_DV, LANES), F32)],
        name="gla",
        compiler_params=_params(2),
    )(q, k, v, gl, cr, gate_up, gate_bias, out_norm, tri)


def _merge_kernel(x_ref, gm_ref, o1, o2, o3, l1, l2, l3, yb_ref, yc_ref, wg_ref, wb_ref, wo_ref, ex_ref, out_ref):
    x = x_ref[...]
    h = _rms(x, gm_ref[...]).astype(BF16)
    la, lb, lc = l1[...], l2[...], l3[...]
    top = jnp.maximum(jnp.maximum(la, lb), lc)
    ea, eb, ec = jnp.exp2(la - top), jnp.exp2(lb - top), jnp.exp2(lc - top)
    inv = 1.0 / (ea + eb + ec)
    expand = ex_ref[...]
    ya = None
    for e, o in ((ea, o1), (eb, o2), (ec, o3)):
        w = e * inv
        w_hi = w.astype(BF16)
        w_lo = (w - w_hi.astype(F32)).astype(BF16)
        wx = (jnp.dot(w_hi, expand, preferred_element_type=F32)
              + jnp.dot(w_lo, expand, preferred_element_type=F32))
        term = wx * o[...].astype(F32)
        ya = term if ya is None else ya + term
    merged = None
    for i, y in enumerate((ya.astype(BF16), yb_ref[...], yc_ref[...])):
        gate = jax.nn.sigmoid(jnp.dot(h, wg_ref[:, i * D_MODEL:(i + 1) * D_MODEL], preferred_element_type=F32))
        term = gate * jnp.dot(y, wb_ref[i], preferred_element_type=F32)
        merged = term if merged is None else merged + term
    out_ref[...] = x + jnp.dot(merged.astype(BF16), wo_ref[...], preferred_element_type=F32)


def _merge(x, gain, oa, la, yb, yc, w_gate, w_branch, w_out, expand, layer):
    t = x.shape[0]
    tm = TOKEN_TILE
    tok = lambda w: pl.BlockSpec((tm, w), lambda i: (i, 0))
    bw = A_GROUP_WIDTH
    return pl.pallas_call(
        _merge_kernel,
        grid=(t // tm,),
        in_specs=[tok(D_MODEL),
                  pl.BlockSpec((None, 1, D_MODEL), lambda i: (layer, 0, 0)),
                  tok(bw), tok(bw), tok(bw), tok(LANES), tok(LANES), tok(LANES), tok(bw), tok(bw),
                  _resident((None, D_MODEL, N_BRANCH * D_MODEL), lambda i: (layer, 0, 0)),
                  _resident((None, N_BRANCH, bw, D_MODEL), lambda i: (layer, 0, 0, 0)),
                  _resident((None, D_MODEL, D_MODEL), lambda i: (layer, 0, 0)),
                  pl.BlockSpec((LANES, bw), lambda i: (0, 0))],
        out_specs=tok(D_MODEL),
        out_shape=jax.ShapeDtypeStruct((t, D_MODEL), F32),
        name="merge",
        compiler_params=_params(1),
    )(x, gain, *oa, *la, yb, yc, w_gate, w_branch, w_out, expand)


def _pack_w_in(w_in):
    n_layer = w_in.shape[0]
    lead = w_in.shape[:2]
    edges = np.cumsum([0, 4608, 512, 256, 256, 256, 512, C_GATE_RANK, 512, N_BRANCH * D_MODEL])
    seg = [w_in[:, :, edges[i]:edges[i + 1]] for i in range(9)]
    a, b_q, b_kv, c_q, c_k, c_v, c_glow, c_r, gates = seg
    b_kv = b_kv.reshape(lead + (2, 2, 1, HEAD_DIM))
    b_kv = jnp.broadcast_to(b_kv, lead + (2, 2, 2, HEAD_DIM)).reshape(lead + (512,))
    c_glow = jnp.pad(c_glow, ((0, 0), (0, 0), (0, LANES - C_GATE_RANK)))
    packed = jnp.concatenate([a, b_q, b_kv, c_q, c_k, c_v, c_glow, c_r], axis=-1)
    assert packed.shape == (n_layer, D_MODEL, W_PACKED)
    return packed.astype(BF16), gates.astype(BF16)


def kernel(x, positions, norm_ffn1, w_ffn1_in, w_ffn1_out, norm_mix, w_in, a_q_norm, a_k_norm, b_q_norm, b_k_norm,
           b_sinks, c_gate_up, c_gate_bias, c_out_norm, w_branch, w_out, norm_ffn2, w_ffn2_in, w_ffn2_out):
    batch, seq, _ = x.shape
    n_layer = w_in.shape[0]
    assert seq % ATTN_CHUNK == 0 and x.shape[2] == D_MODEL

    w_packed, w_gate = _pack_w_in(w_in)
    w1_in, w1_out = w_ffn1_in.astype(BF16), w_ffn1_out.astype(BF16)
    w2_in, w2_out = w_ffn2_in.astype(BF16), w_ffn2_out.astype(BF16)
    wb, wo = w_branch.astype(BF16), w_out.astype(BF16)
    row = lambda g: g[:, None, :]
    head_gains = jnp.stack([jnp.tile(g, (1, LANES // HEAD_DIM)) for g in (a_q_norm, a_k_norm, b_q_norm, b_k_norm)],
                           axis=1)
    gate_up = jnp.pad(c_gate_up, ((0, 0), (0, LANES - C_GATE_RANK), (0, 0))).astype(BF16)
    gate_bias = row(c_gate_bias)

    lane_head = np.arange(LANES) // HEAD_DIM
    ones_bd = jnp.asarray(lane_head[:, None] == lane_head[None, :], BF16)
    expand = jnp.asarray(np.arange(LANES)[:, None] == (np.arange(A_GROUP_WIDTH) // HEAD_DIM)[None, :], BF16)
    idx = np.arange(GLA_SUB)
    tri = jnp.asarray((idx[:, None] >= idx[None, :]) & (idx[:, None] // C_CHUNK == idx[None, :] // C_CHUNK), BF16)

    cos, sin = _rope_tables(positions)
    xt = x.reshape(batch * seq, D_MODEL)
    for l in range(n_layer):
        xt = _ffn(xt, row(norm_ffn1), w1_in, w1_out, l)
        (qa1, ka1, va1, qa2, ka2, va2, qa3, ka3, va3, qb, kb, vb, qc, kc, vc, gl, cr) = _inproj(
            xt, row(norm_mix), cos, sin, w_packed, head_gains, ones_bd, l, batch, seq)
        oa, la = [], []
        for (window, d), (q, k, v) in zip(A_CONFIGS, ((qa1, ka1, va1), (qa2, ka2, va2), (qa3, ka3, va3))):
            o, lse = _attention(q, k, v, None, window=window // d, kv_div=1, batch=batch, seq=seq)
            oa.append(o.reshape(batch * seq, A_GROUP_WIDTH))
            la.append(lse.reshape(batch * seq, LANES))
        yb, _ = _attention(qb, kb, vb, b_sinks[l], window=B_WINDOW - 1, kv_div=2, batch=batch, seq=seq)
        yb = yb.reshape(batch * seq, A_GROUP_WIDTH)
        yc = _gla(qc, kc, vc, gl, cr, gate_up, gate_bias, row(c_out_norm), tri, l, batch, seq)
        xt = _merge(xt, row(norm_mix), oa, la, yb, yc, w_gate, wb, wo, expand, l)
        xt = _ffn(xt, row(norm_ffn2), w2_in, w2_out, l)
    return xt.reshape(batch, seq, D_MODEL)
```

```python
import functools

import numpy as np
import jax
import jax.numpy as jnp
from jax import lax
from jax.experimental import pallas as pl
from jax.experimental.pallas import tpu as pltpu

F32 = jnp.float32
BF16 = jnp.bfloat16

D_MODEL = 1024
D_FF = 2816
HEAD_DIM = 64
NORM_EPS = 1e-6
ROPE_THETA = 10000.0
LOG2E = float(np.log2(np.e))
Q_SCALE = HEAD_DIM ** -0.5 * LOG2E
LANES = 128
BLOCK = 128
A_CONFIGS = ((128, 1), (512, 4), (2048, 16))
A_GROUP_WIDTH = 512
B_WINDOW = 128
C_HEADS = 4
C_DK = 64
C_DV = 128
C_GATE_RANK = 16
C_TAU = 16.0
C_CHUNK = 64
GLA_SUB = 256
N_BRANCH = 3

A_SEG = 3 * A_GROUP_WIDTH
OFF_A = 0
OFF_B = 3 * A_SEG
B_SEG = 1024
OFF_C = OFF_B + B_SEG
C_SEG = 1664
W_PACKED = OFF_C + C_SEG

TOKEN_TILE = 512
ATTN_CHUNK = 2048
MXU_COLS = 256
FFN_CHUNKS = ((0, 1536), (1536, D_FF))
VMEM_LIMIT = 56 * 2**20


def _params(n_axes):
    return pltpu.CompilerParams(dimension_semantics=("arbitrary",) * n_axes, vmem_limit_bytes=VMEM_LIMIT)


def _rms(x, gain):
    return x * lax.rsqrt(jnp.mean(x * x, axis=-1, keepdims=True) + NORM_EPS) * gain


def _resident(block, index_map):
    return pl.BlockSpec(block, index_map, pipeline_mode=pl.Buffered(1))


def _rope_kernel(pos_ref, inv_ref, sgn_ref, cos_ref, sin_ref):
    ang = pos_ref[...].astype(F32) * inv_ref[...]
    cos_ref[...] = jnp.cos(ang)
    sin_ref[...] = jnp.sin(ang) * sgn_ref[...]


def _rope_tables(positions):
    t = positions.size
    inv = ROPE_THETA ** (-jnp.arange(0, HEAD_DIM, 2, dtype=F32) / HEAD_DIM)
    inv = jnp.tile(inv, LANES // (HEAD_DIM // 2))[None, :]
    sgn = jnp.concatenate([-jnp.ones(LANES // 2, F32), jnp.ones(LANES // 2, F32)])[None, :]
    tm = TOKEN_TILE
    tab = jax.ShapeDtypeStruct((t, LANES), F32)
    return pl.pallas_call(
        _rope_kernel,
        grid=(t // tm,),
        in_specs=[pl.BlockSpec((tm, LANES), lambda i: (i, 0)),
                  pl.BlockSpec((1, LANES), lambda i: (0, 0)),
                  pl.BlockSpec((1, LANES), lambda i: (0, 0))],
        out_specs=[pl.BlockSpec((tm, LANES), lambda i: (i, 0))] * 2,
        out_shape=[tab, tab],
        name="rope",
        compiler_params=_params(1),
    )(jnp.broadcast_to(positions.reshape(t, 1), (t, LANES)), inv, sgn)


def _ffn_kernel(x_ref, g_ref, wg_ref, wu_ref, wo_ref, o_ref):
    x = x_ref[...]
    h = _rms(x, g_ref[...]).astype(BF16)
    acc = None
    for lo, hi in FFN_CHUNKS:
        sl = slice(lo, hi)
        g = jnp.dot(h, wg_ref[:, sl], preferred_element_type=F32)
        u = jnp.dot(h, wu_ref[:, sl], preferred_element_type=F32)
        a = (g * jax.nn.sigmoid(g) * u).astype(BF16)
        part = jnp.dot(a, wo_ref[sl, :], preferred_element_type=F32)
        acc = part if acc is None else acc + part
    o_ref[...] = x + 0.5 * acc


def _ffn(x, gain, w_in, w_out, layer):
    t = x.shape[0]
    tm = TOKEN_TILE
    return pl.pallas_call(
        _ffn_kernel,
        grid=(t // tm,),
        in_specs=[pl.BlockSpec((tm, D_MODEL), lambda i: (i, 0)),
                  pl.BlockSpec((None, 1, D_MODEL), lambda i: (layer, 0, 0)),
                  _resident((None, D_MODEL, D_FF), lambda i: (layer, 0, 0)),
                  _resident((None, D_MODEL, D_FF), lambda i: (layer, 0, 1)),
                  _resident((None, D_FF, D_MODEL), lambda i: (layer, 0, 0))],
        out_specs=pl.BlockSpec((tm, D_MODEL), lambda i: (i, 0)),
        out_shape=jax.ShapeDtypeStruct((t, D_MODEL), F32),
        name="ffn",
        compiler_params=_params(1),
    )(x, gain, w_in, w_in, w_out)


def _inproj_kernel(x_ref, gm_ref, cos_ref, sin_ref, w_ref, gains_ref, ones_ref,
                   qa1, ka1, va1, qa2, ka2, va2, qa3, ka3, va3, qb, kb, vb, qc, kc, vc, gl, cr,
                   hs_ref, hp_ref, tb_ref):
    tm = x_ref.shape[0]
    n_slab = D_MODEL // LANES
    h = _rms(x_ref[...], gm_ref[...])
    h_bf = h.astype(BF16)
    cos0 = cos_ref[...]
    sin0 = sin_ref[...]
    ones = ones_ref[...]
    gains = gains_ref[...]

    def norm_rope(y, gain, cos, sin, scale=None):
        ss = jnp.dot((y * y).astype(BF16), ones, preferred_element_type=F32)
        yn = y * lax.rsqrt(ss * (1.0 / HEAD_DIM) + NORM_EPS)
        tiles = []
        for t in range(2):
            z = yn[:, t * LANES:(t + 1) * LANES] * gain
            z = z * cos + pltpu.roll(z, LANES // 2, 1) * sin
            tiles.append((z if scale is None else z * scale).astype(BF16))
        return tiles

    for s in range(n_slab):
        hs_ref[s] = h[:, s * LANES:(s + 1) * LANES]
    tb_ref[0] = cos0
    tb_ref[1] = sin0

    def class_major(d):
        if d == 1:
            return h_bf, cos0, sin0
        rows = tm // d
        for c in range(d):
            for s in range(n_slab):
                hp_ref[c * rows:(c + 1) * rows, s * LANES:(s + 1) * LANES] = (
                    hs_ref[s, pl.ds(c, rows, stride=d), :].astype(BF16))
        cos = jnp.concatenate([tb_ref[0, pl.ds(c, rows, stride=d), :] for c in range(d)], axis=0)
        sin = jnp.concatenate([tb_ref[1, pl.ds(c, rows, stride=d), :] for c in range(d)], axis=0)
        return hp_ref[...], cos, sin

    def store_classes(ref, col, val, d):
        rows = tm // d
        for c in range(d):
            ref[c, :, col:col + LANES] = val[c * rows:(c + 1) * rows, :]

    for gi, (qo, ko, vo) in enumerate(((qa1, ka1, va1), (qa2, ka2, va2), (qa3, ka3, va3))):
        d = A_CONFIGS[gi][1]
        hp, cos, sin = class_major(d)
        yq, yk, yv = [jnp.dot(hp, w_ref[:, pl.ds(OFF_A + part * A_SEG + gi * A_GROUP_WIDTH, A_GROUP_WIDTH)],
                              preferred_element_type=F32) for part in range(3)]
        for j in range(A_GROUP_WIDTH // (2 * LANES)):
            c0 = j * 2 * LANES
            q_tiles = norm_rope(yq[:, c0:c0 + 2 * LANES], gains[0:1], cos, sin, Q_SCALE)
            k_tiles = norm_rope(yk[:, c0:c0 + 2 * LANES], gains[1:2], cos, sin)
            for t in range(2):
                ct = c0 + t * LANES
                store_classes(qo, ct, q_tiles[t], d)
                store_classes(ko, ct, k_tiles[t], d)
                store_classes(vo, ct, yv[:, ct:ct + LANES].astype(BF16), d)

    y = jnp.dot(h_bf, w_ref[:, OFF_B:OFF_B + B_SEG], preferred_element_type=F32)
    for j in range(2):
        c0 = j * 2 * LANES
        q_tiles = norm_rope(y[:, c0:c0 + 2 * LANES], gains[2:3], cos0, sin0, Q_SCALE)
        for t in range(2):
            qb[0, :, c0 + t * LANES:c0 + (t + 1) * LANES] = q_tiles[t]
    k_tiles = norm_rope(y[:, 512:768], gains[3:4], cos0, sin0)
    for t in range(2):
        kb[0, :, t * LANES:(t + 1) * LANES] = k_tiles[t]
    vb[0, :, :] = y[:, 768:1024].astype(BF16)

    y = jnp.dot(h_bf, w_ref[:, OFF_C:OFF_C + C_SEG], preferred_element_type=F32)
    qc[...] = y[:, 0:256].astype(BF16)
    kc[...] = y[:, 256:512].astype(BF16)
    vc[...] = y[:, 512:1024].astype(BF16)
    gl[...] = y[:, 1024:1152].astype(BF16)
    cr[...] = y[:, 1152:1664].astype(BF16)


def _inproj(x, gain, cos, sin, w_packed, head_gains, ones_bd, layer, batch, seq):
    t = x.shape[0]
    tm = TOKEN_TILE
    tiles_per_seq = seq // tm

    def cm_spec(d, width):
        return pl.BlockSpec((None, d, tm // d, width),
                            lambda i: (i // tiles_per_seq, 0, i % tiles_per_seq, 0))

    def cm_shape(d, width):
        return jax.ShapeDtypeStruct((batch, d, seq // d, width), BF16)

    def tok_spec(width):
        return pl.BlockSpec((tm, width), lambda i: (i, 0))

    out_specs, out_shape = [], []
    for _, d in A_CONFIGS:
        for _ in range(3):
            out_specs.append(cm_spec(d, A_GROUP_WIDTH))
            out_shape.append(cm_shape(d, A_GROUP_WIDTH))
    for width in (512, 256, 256):
        out_specs.append(cm_spec(1, width))
        out_shape.append(cm_shape(1, width))
    for width in (256, 256, 512, 128, 512):
        out_specs.append(tok_spec(width))
        out_shape.append(jax.ShapeDtypeStruct((t, width), BF16))

    return pl.pallas_call(
        _inproj_kernel,
        grid=(t // tm,),
        in_specs=[tok_spec(D_MODEL),
                  pl.BlockSpec((None, 1, D_MODEL), lambda i: (layer, 0, 0)),
                  tok_spec(LANES), tok_spec(LANES),
                  _resident((None, D_MODEL, W_PACKED), lambda i: (layer, 0, 0)),
                  pl.BlockSpec((None, 4, LANES), lambda i: (layer, 0, 0)),
                  pl.BlockSpec((2 * LANES, 2 * LANES), lambda i: (0, 0))],
        out_specs=out_specs,
        out_shape=out_shape,
        scratch_shapes=[pltpu.VMEM((D_MODEL // LANES, tm, LANES), F32),
                        pltpu.VMEM((tm, D_MODEL), BF16),
                        pltpu.VMEM((2, tm, LANES), F32)],
        name="inproj",
        compiler_params=_params(1),
    )(x, gain, cos, sin, w_packed, head_gains, ones_bd)


def _attn_kernel(d, nb, window, kv_div, use_sink, *refs):
    if use_sink:
        sink_ref, q_ref, kc_ref, kp_ref, vc_ref, vp_ref, o_ref, lse_ref, osc, lsc, s_scr, p_scr, st_scr = refs
    else:
        q_ref, kc_ref, kp_ref, vc_ref, vp_ref, o_ref, lse_ref, osc, lsc, s_scr, p_scr, st_scr = refs
    chunk = pl.program_id(1)
    lane = lax.broadcasted_iota(jnp.int32, (1, LANES), 1)
    head_of_lane = (lane // (HEAD_DIM // 2)) % 2
    key = lax.broadcasted_iota(jnp.int32, (2 * BLOCK, 2 * BLOCK), 0)
    col = lax.broadcasted_iota(jnp.int32, (2 * BLOCK, 2 * BLOCK), 1)
    dist = (col % BLOCK) + BLOCK - key
    band = (dist >= 0) & (dist <= window)
    col_row = lax.broadcasted_iota(jnp.int32, (1, 2 * BLOCK), 1)
    sub8 = lax.broadcasted_iota(jnp.int32, (8, LANES), 0)
    nt = (((1,), (1,)), ((), ()))
    tn = (((0,), (0,)), ((), ()))

    n_units = d * nb

    def locate(u):
        c = u // nb if d > 1 else 0
        j = u % nb if nb > 1 else 0
        if isinstance(j, int):
            return c, j, j * BLOCK, max(j - 1, 0) * BLOCK
        return c, j, pl.multiple_of(j * BLOCK, BLOCK), pl.multiple_of(jnp.maximum(j - 1, 0) * BLOCK, BLOCK)

    def window_rows(cur_ref, prev_ref, u, p):
        c, j, row0, prow = locate(u)
        kcol = (p // kv_div) * LANES
        ksl = slice(kcol, kcol + LANES)
        cur = cur_ref[c, pl.ds(row0, BLOCK), ksl]
        prev = prev_ref[c, :, ksl]
        if nb > 1:
            prev = jnp.where(j == 0, prev, cur_ref[c, pl.ds(prow, BLOCK), ksl])
        return jnp.concatenate([prev, cur], axis=0)

    def scores(u):
        c, j, row0, _ = locate(u)
        seq_start = jnp.logical_and(chunk == 0, j == 0)
        valid = band & jnp.logical_or(key >= BLOCK, jnp.logical_not(seq_start))
        for p in range(4):
            qp = q_ref[c, pl.ds(row0, BLOCK), p * LANES:(p + 1) * LANES]
            q2 = jnp.concatenate([jnp.where(head_of_lane == e, qp, jnp.zeros_like(qp)) for e in range(2)], axis=0)
            s = lax.dot_general(window_rows(kc_ref, kp_ref, u, p), q2, nt,
                                preferred_element_type=F32)
            s_scr[p] = jnp.where(valid, s, -jnp.inf)

    def softmax(u):
        for p in range(4):
            s = s_scr[p]
            m = jnp.max(s, axis=0, keepdims=True)
            if use_sink:
                sk = jnp.where(col_row < BLOCK, sink_ref[2 * p], sink_ref[2 * p + 1]) * LOG2E
                m = jnp.maximum(m, sk)
            pr = jnp.exp2(s - m)
            den = jnp.sum(pr, axis=0, keepdims=True)
            if use_sink:
                den = den + jnp.exp2(sk - m)
            p_scr[p] = pr.astype(BF16)
            st_scr[p, 0:1, :] = m
            st_scr[p, 1:2, :] = den

    def outputs(u):
        c, _, row0, _ = locate(u)
        start = row0 * d + c
        rows = pl.ds(start, BLOCK, stride=d) if d > 1 else pl.ds(start, BLOCK)
        lse8 = jnp.zeros((8, LANES), F32)
        for p in range(4):
            ob = lax.dot_general(window_rows(vc_ref, vp_ref, u, p), p_scr[p], tn,
                                 preferred_element_type=F32)
            m = st_scr[p, 0:1, :]
            den = st_scr[p, 1:2, :]
            inv = 1.0 / den
            o_t = jnp.concatenate([ob[0:HEAD_DIM, 0:BLOCK] * inv[:, 0:BLOCK],
                                   ob[HEAD_DIM:LANES, BLOCK:2 * BLOCK] * inv[:, BLOCK:2 * BLOCK]], axis=0)
            if d > 1:
                osc[p, rows, :] = o_t.T
            else:
                o_ref[rows, p * LANES:(p + 1) * LANES] = o_t.T.astype(BF16)
            l2 = m + jnp.log2(den)
            lse8 = jnp.where(sub8 == 2 * p, l2[:, 0:BLOCK], lse8)
            lse8 = jnp.where(sub8 == 2 * p + 1, l2[:, BLOCK:2 * BLOCK], lse8)
        lse_t = jnp.concatenate([lse8, jnp.zeros((LANES - 8, LANES), F32)], axis=0)
        if d > 1:
            lsc[rows, :] = lse_t.T
        else:
            lse_ref[rows, :] = lse_t.T

    scores(0)
    softmax(0)
    scores(1)

    def body(u, carry):
        outputs(u - 2)
        softmax(u - 1)
        scores(u)
        return carry

    lax.fori_loop(2, n_units, body, 0)
    outputs(n_units - 2)
    softmax(n_units - 1)
    outputs(n_units - 1)
    if d > 1:
        for p in range(4):
            o_ref[:, p * LANES:(p + 1) * LANES] = osc[p].astype(BF16)
        lse_ref[...] = lsc[...]


def _attention(q, k, v, sinks, *, window, kv_div, batch, seq):
    d = q.shape[1]
    nb = ATTN_CHUNK // (d * BLOCK)
    kw = k.shape[-1]
    rows = nb * BLOCK
    cur = lambda b, i: (b, 0, i, 0)
    prev = lambda b, i: (b, 0, jnp.maximum(i * nb - 1, 0), 0)
    in_specs = [pl.BlockSpec((None, d, rows, A_GROUP_WIDTH), cur),
                pl.BlockSpec((None, d, rows, kw), cur),
                pl.BlockSpec((None, d, BLOCK, kw), prev),
                pl.BlockSpec((None, d, rows, kw), cur),
                pl.BlockSpec((None, d, BLOCK, kw), prev)]
    args = [q, k, k, v, v]
    use_sink = sinks is not None
    if use_sink:
        in_specs = [pl.BlockSpec(memory_space=pltpu.SMEM)] + in_specs
        args = [sinks] + args
    return pl.pallas_call(
        functools.partial(_attn_kernel, d, nb, window, kv_div, use_sink),
        grid=(batch, seq // ATTN_CHUNK),
        in_specs=in_specs,
        out_specs=[pl.BlockSpec((None, ATTN_CHUNK, A_GROUP_WIDTH), lambda b, i: (b, i, 0)),
                   pl.BlockSpec((None, ATTN_CHUNK, LANES), lambda b, i: (b, i, 0))],
        out_shape=[jax.ShapeDtypeStruct((batch, seq, A_GROUP_WIDTH), BF16),
                   jax.ShapeDtypeStruct((batch, seq, LANES), F32)],
        scratch_shapes=[pltpu.VMEM((4, ATTN_CHUNK if d > 1 else 8, LANES), F32),
                        pltpu.VMEM((ATTN_CHUNK if d > 1 else 8, LANES), F32),
                        pltpu.VMEM((4, 2 * BLOCK, 2 * BLOCK), F32),
                        pltpu.VMEM((4, 2 * BLOCK, 2 * BLOCK), BF16),
                        pltpu.VMEM((4, 8, 2 * BLOCK), F32)],
        name=f"attn_d{d}_w{window}",
        compiler_params=_params(2),
    )(*args)


def _gla_kernel(q_ref, k_ref, v_ref, gl_ref, cr_ref, gu_ref, gb_ref, on_ref, tri_ref, y_ref, st_ref):
    tm = q_ref.shape[0]
    width = C_HEADS * C_DK
    n_chunk = GLA_SUB // C_CHUNK

    @pl.when(pl.program_id(1) == 0)
    def _():
        st_ref[...] = jnp.zeros_like(st_ref)

    tri = tri_ref[...]
    keep = tri > 0
    on_gain = on_ref[...]
    head_of_lane = lax.broadcasted_iota(jnp.int32, (1, LANES), 1) // C_DK
    nt = (((1,), (1,)), ((), ()))
    tn = (((0,), (0,)), ((), ()))

    def chunked(a):
        return a.reshape(n_chunk, C_CHUNK, width)

    for sub in range(tm // GLA_SUB):
        rows = slice(sub * GLA_SUB, (sub + 1) * GLA_SUB)
        x = jnp.dot(gl_ref[rows, :], gu_ref[...], preferred_element_type=F32) + gb_ref[...]
        lg = (jnp.minimum(x, 0.0) - jnp.log(1.0 + jnp.exp(-jnp.abs(x)))) * (1.0 / C_TAU)
        g1 = lg.astype(BF16)
        g2 = (lg - g1.astype(F32)).astype(BF16)
        bs = jnp.dot(tri, jnp.concatenate([g1, g2], axis=1), preferred_element_type=F32)
        b = chunked(bs[:, 0:width] + bs[:, width:2 * width])
        b_mid = b[:, C_CHUNK // 2 - 1:C_CHUNK // 2, :]
        b_last = b[:, C_CHUNK - 1:C_CHUNK, :]
        q = chunked(q_ref[rows, :].astype(F32) * (C_DK ** -0.5))
        k = chunked(k_ref[rows, :].astype(F32))
        v = v_ref[rows, :]
        qd = (q * jnp.exp(b)).astype(BF16)
        qs = (q * jnp.exp(b - b_mid)).astype(BF16)
        ks = (k * jnp.exp(b_mid - b)).astype(BF16)
        kd = (k * jnp.exp(b_last - b)).astype(BF16)
        decay = jnp.exp(b_last)
        for hh in range(C_HEADS):
            cols = slice(hh * LANES, (hh + 1) * LANES)
            kcols = slice((hh // 2) * LANES, (hh // 2 + 1) * LANES)
            mine = head_of_lane == (hh % 2)
            qs_h = jnp.where(mine, qs[:, :, kcols], jnp.zeros((), BF16))
            qd_h = jnp.where(mine, qd[:, :, kcols], jnp.zeros((), BF16))
            att = lax.dot_general(qs_h.reshape(GLA_SUB, LANES), ks[:, :, kcols].reshape(GLA_SUB, LANES),
                                  nt, preferred_element_type=F32)
            att = jnp.where(keep, att, 0.0).astype(BF16)
            o = jnp.dot(att, v[:, cols], preferred_element_type=F32)
            st = st_ref[hh]
            inter = []
            for c in range(n_chunk):
                inter.append(lax.dot_general(qd_h[c], st.astype(BF16), nt, preferred_element_type=F32))
                inc = lax.dot_general(v[c * C_CHUNK:(c + 1) * C_CHUNK, cols], kd[c, :, kcols], tn,
                                      preferred_element_type=F32)
                st = st * decay[c, :, kcols] + inc
            st_ref[hh] = st
            o = o + jnp.concatenate(inter, axis=0)
            gate = cr_ref[rows, cols].astype(F32)
            y_ref[rows, cols] = (_rms(o, on_gain) * (gate * jax.nn.sigmoid(gate))).astype(BF16)


def _gla(q, k, v, gl, cr, gate_up, gate_bias, out_norm, tri, layer, batch, seq):
    t = q.shape[0]
    tm = TOKEN_TILE
    per_seq = seq // tm
    width = C_HEADS * C_DV
    kwidth = C_HEADS * C_DK
    tok = lambda w: pl.BlockSpec((tm, w), lambda b, i: (b * per_seq + i, 0))
    return pl.pallas_call(
        _gla_kernel,
        grid=(batch, per_seq),
        in_specs=[tok(kwidth), tok(kwidth), tok(width), tok(LANES), tok(width),
                  pl.BlockSpec((None, LANES, kwidth), lambda b, i: (layer, 0, 0)),
                  pl.BlockSpec((None, 1, kwidth), lambda b, i: (layer, 0, 0)),
                  pl.BlockSpec((None, 1, C_DV), lambda b, i: (layer, 0, 0)),
                  pl.BlockSpec((GLA_SUB, GLA_SUB), lambda b, i: (0, 0))],
        out_specs=tok(width),
        out_shape=jax.ShapeDtypeStruct((t, width), BF16),
        scratch_shapes=[pltpu.VMEM((C_HEADS, C_DV, LANES), F32)],
        name="gla",
        compiler_params=_params(2),
    )(q, k, v, gl, cr, gate_up, gate_bias, out_norm, tri)


def _merge_kernel(x_ref, gm_ref, o1, o2, o3, l1, l2, l3, yb_ref, yc_ref, wg_ref, wb_ref, wo_ref, ex_ref, out_ref):
    x = x_ref[...]
    h = _rms(x, gm_ref[...]).astype(BF16)
    la, lb, lc = l1[...], l2[...], l3[...]
    top = jnp.maximum(jnp.maximum(la, lb), lc)
    ea, eb, ec = jnp.exp2(la - top), jnp.exp2(lb - top), jnp.exp2(lc - top)
    inv = 1.0 / (ea + eb + ec)
    expand = ex_ref[...]
    low_lanes = lax.broadcasted_iota(jnp.int32, (1, LANES), 1) < LANES // 2
    ya = None
    for e, o in ((ea, o1), (eb, o2), (ec, o3)):
        w = e * inv
        w_hi = w.astype(BF16)
        w_lo = pltpu.roll(w - w_hi.astype(F32), LANES // 2, 1).astype(BF16)
        wx = jnp.dot(jnp.where(low_lanes, w_hi, w_lo), expand, preferred_element_type=F32)
        term = wx * o[...].astype(F32)
        ya = term if ya is None else ya + term
    merged = None
    for i, y in enumerate((ya.astype(BF16), yb_ref[...], yc_ref[...])):
        gate = jax.nn.sigmoid(jnp.dot(h, wg_ref[:, i * D_MODEL:(i + 1) * D_MODEL], preferred_element_type=F32))
        term = gate * jnp.dot(y, wb_ref[i], preferred_element_type=F32)
        merged = term if merged is None else merged + term
    out_ref[...] = x + jnp.dot(merged.astype(BF16), wo_ref[...], preferred_element_type=F32)


def _merge(x, gain, oa, la, yb, yc, w_gate, w_branch, w_out, expand, layer):
    t = x.shape[0]
    tm = TOKEN_TILE
    tok = lambda w: pl.BlockSpec((tm, w), lambda i: (i, 0))
    bw = A_GROUP_WIDTH
    return pl.pallas_call(
        _merge_kernel,
        grid=(t // tm,),
        in_specs=[tok(D_MODEL),
                  pl.BlockSpec((None, 1, D_MODEL), lambda i: (layer, 0, 0)),
                  tok(bw), tok(bw), tok(bw), tok(LANES), tok(LANES), tok(LANES), tok(bw), tok(bw),
                  _resident((None, D_MODEL, N_BRANCH * D_MODEL), lambda i: (layer, 0, 0)),
                  _resident((None, N_BRANCH, bw, D_MODEL), lambda i: (layer, 0, 0, 0)),
                  _resident((None, D_MODEL, D_MODEL), lambda i: (layer, 0, 0)),
                  pl.BlockSpec((LANES, bw), lambda i: (0, 0))],
        out_specs=tok(D_MODEL),
        out_shape=jax.ShapeDtypeStruct((t, D_MODEL), F32),
        name="merge",
        compiler_params=_params(1),
    )(x, gain, *oa, *la, yb, yc, w_gate, w_branch, w_out, expand)


def _split_half_layout(w):
    lead, cols = w.shape[:-1], w.shape[-1]
    half = HEAD_DIM // 2
    return w.reshape(lead + (cols // LANES, 2, 2, half)).swapaxes(-3, -2).reshape(lead + (cols,))


def _pack_w_in(w_in):
    n_layer = w_in.shape[0]
    lead = w_in.shape[:2]
    edges = np.cumsum([0, 4608, 512, 256, 256, 256, 512, C_GATE_RANK, 512, N_BRANCH * D_MODEL])
    seg = [w_in[:, :, edges[i]:edges[i + 1]] for i in range(9)]
    a, b_q, b_kv, c_q, c_k, c_v, c_glow, c_r, gates = seg
    b_kv = b_kv.reshape(lead + (2, 2, 1, HEAD_DIM))
    b_kv = jnp.broadcast_to(b_kv, lead + (2, 2, 2, HEAD_DIM)).reshape(lead + (512,))
    c_glow = jnp.pad(c_glow, ((0, 0), (0, 0), (0, LANES - C_GATE_RANK)))
    a_qk, a_v = a[:, :, :2 * A_SEG], a[:, :, 2 * A_SEG:]
    b_k, b_v = b_kv[:, :, :256], b_kv[:, :, 256:]
    packed = jnp.concatenate([_split_half_layout(a_qk), a_v, _split_half_layout(b_q), _split_half_layout(b_k), b_v,
                              c_q, c_k, c_v, c_glow, c_r], axis=-1)
    assert packed.shape == (n_layer, D_MODEL, W_PACKED)
    return packed.astype(BF16), gates.astype(BF16)


def kernel(x, positions, norm_ffn1, w_ffn1_in, w_ffn1_out, norm_mix, w_in, a_q_norm, a_k_norm, b_q_norm, b_k_norm,
           b_sinks, c_gate_up, c_gate_bias, c_out_norm, w_branch, w_out, norm_ffn2, w_ffn2_in, w_ffn2_out):
    batch, seq, _ = x.shape
    n_layer = w_in.shape[0]
    assert seq % ATTN_CHUNK == 0 and x.shape[2] == D_MODEL

    w_packed, w_gate = _pack_w_in(w_in)
    w1_in, w1_out = w_ffn1_in.astype(BF16), w_ffn1_out.astype(BF16)
    w2_in, w2_out = w_ffn2_in.astype(BF16), w_ffn2_out.astype(BF16)
    wb, wo = w_branch.astype(BF16), w_out.astype(BF16)
    row = lambda g: g[:, None, :]
    head_gains = jnp.stack([_split_half_layout(jnp.tile(g, (1, LANES // HEAD_DIM)))
                            for g in (a_q_norm, a_k_norm, b_q_norm, b_k_norm)], axis=1)
    gate_up = jnp.pad(c_gate_up, ((0, 0), (0, LANES - C_GATE_RANK), (0, 0))).astype(BF16)
    gate_bias = row(c_gate_bias)

    lane2 = np.arange(2 * LANES)
    lane_head = 2 * (lane2 // LANES) + (lane2 // (HEAD_DIM // 2)) % 2
    ones_bd = jnp.asarray(lane_head[:, None] == lane_head[None, :], BF16)
    expand = jnp.asarray((np.arange(LANES) % (LANES // 2))[:, None] == (np.arange(A_GROUP_WIDTH) // HEAD_DIM)[None, :],
                         BF16)
    idx = np.arange(GLA_SUB)
    tri = jnp.asarray((idx[:, None] >= idx[None, :]) & (idx[:, None] // C_CHUNK == idx[None, :] // C_CHUNK), BF16)

    cos, sin = _rope_tables(positions)
    xt = x.reshape(batch * seq, D_MODEL)
    for l in range(n_layer):
        xt = _ffn(xt, row(norm_ffn1), w1_in, w1_out, l)
        (qa1, ka1, va1, qa2, ka2, va2, qa3, ka3, va3, qb, kb, vb, qc, kc, vc, gl, cr) = _inproj(
            xt, row(norm_mix), cos, sin, w_packed, head_gains, ones_bd, l, batch, seq)
        oa, la = [], []
        for (window, d), (q, k, v) in zip(A_CONFIGS, ((qa1, ka1, va1), (qa2, ka2, va2), (qa3, ka3, va3))):
            o, lse = _attention(q, k, v, None, window=window // d, kv_div=1, batch=batch, seq=seq)
            oa.append(o.reshape(batch * seq, A_GROUP_WIDTH))
            la.append(lse.reshape(batch * seq, LANES))
        yb, _ = _attention(qb, kb, vb, b_sinks[l], window=B_WINDOW - 1, kv_div=2, batch=batch, seq=seq)
        yb = yb.reshape(batch * seq, A_GROUP_WIDTH)
        yc = _gla(qc, kc, vc, gl, cr, gate_up, gate_bias, row(c_out_norm), tri, l, batch, seq)
        xt = _merge(xt, row(norm_mix), oa, la, yb, yc, w_gate, wb, wo, expand, l)
        xt = _ffn(xt, row(norm_ffn2), w2_in, w2_out, l)
    return xt.reshape(batch, seq, D_MODEL)
```

```python
import functools

import numpy as np
import jax
import jax.numpy as jnp
from jax import lax
from jax.experimental import pallas as pl
from jax.experimental.pallas import tpu as pltpu

F32 = jnp.float32
BF16 = jnp.bfloat16

D_MODEL = 1024
D_FF = 2816
HEAD_DIM = 64
NORM_EPS = 1e-6
ROPE_THETA = 10000.0
LOG2E = float(np.log2(np.e))
Q_SCALE = HEAD_DIM ** -0.5 * LOG2E
LANES = 128
BLOCK = 128
A_CONFIGS = ((128, 1), (512, 4), (2048, 16))
A_GROUP_WIDTH = 512
B_WINDOW = 128
C_HEADS = 4
C_DK = 64
C_DV = 128
C_GATE_RANK = 16
C_TAU = 16.0
C_CHUNK = 64
GLA_SUB = 256
N_BRANCH = 3

A_SEG = 3 * A_GROUP_WIDTH
OFF_A = 0
OFF_B = 3 * A_SEG
B_SEG = 1024
OFF_C = OFF_B + B_SEG
C_SEG = 1664
W_PACKED = OFF_C + C_SEG

TOKEN_TILE = 512
ATTN_CHUNK = 2048
MXU_COLS = 256
FFN_CHUNKS = ((0, 1536), (1536, D_FF))
VMEM_LIMIT = 56 * 2**20


def _params(n_axes):
    return pltpu.CompilerParams(dimension_semantics=("arbitrary",) * n_axes, vmem_limit_bytes=VMEM_LIMIT)


def _rms(x, gain):
    return x * lax.rsqrt(jnp.mean(x * x, axis=-1, keepdims=True) + NORM_EPS) * gain


def _resident(block, index_map):
    return pl.BlockSpec(block, index_map, pipeline_mode=pl.Buffered(1))


def _rope_kernel(pos_ref, inv_ref, sgn_ref, cos_ref, sin_ref):
    ang = pos_ref[...].astype(F32) * inv_ref[...]
    cos_ref[...] = jnp.cos(ang)
    sin_ref[...] = jnp.sin(ang) * sgn_ref[...]


def _rope_tables(positions):
    t = positions.size
    inv = ROPE_THETA ** (-jnp.arange(0, HEAD_DIM, 2, dtype=F32) / HEAD_DIM)
    inv = jnp.tile(inv, LANES // (HEAD_DIM // 2))[None, :]
    sgn = jnp.tile(jnp.concatenate([-jnp.ones(HEAD_DIM // 2, F32), jnp.ones(HEAD_DIM // 2, F32)]),
                   LANES // HEAD_DIM)[None, :]
    tm = TOKEN_TILE
    tab = jax.ShapeDtypeStruct((t, LANES), F32)
    return pl.pallas_call(
        _rope_kernel,
        grid=(t // tm,),
        in_specs=[pl.BlockSpec((tm, LANES), lambda i: (i, 0)),
                  pl.BlockSpec((1, LANES), lambda i: (0, 0)),
                  pl.BlockSpec((1, LANES), lambda i: (0, 0))],
        out_specs=[pl.BlockSpec((tm, LANES), lambda i: (i, 0))] * 2,
        out_shape=[tab, tab],
        name="rope",
        compiler_params=_params(1),
    )(jnp.broadcast_to(positions.reshape(t, 1), (t, LANES)), inv, sgn)


def _ffn_kernel(x_ref, g_ref, wg_ref, wu_ref, wo_ref, o_ref):
    x = x_ref[...]
    h = _rms(x, g_ref[...]).astype(BF16)
    acc = None
    for lo, hi in FFN_CHUNKS:
        sl = slice(lo, hi)
        g = jnp.dot(h, wg_ref[:, sl], preferred_element_type=F32)
        u = jnp.dot(h, wu_ref[:, sl], preferred_element_type=F32)
        a = (g * jax.nn.sigmoid(g) * u).astype(BF16)
        part = jnp.dot(a, wo_ref[sl, :], preferred_element_type=F32)
        acc = part if acc is None else acc + part
    o_ref[...] = x + 0.5 * acc


def _ffn(x, gain, w_in, w_out, layer):
    t = x.shape[0]
    tm = TOKEN_TILE
    return pl.pallas_call(
        _ffn_kernel,
        grid=(t // tm,),
        in_specs=[pl.BlockSpec((tm, D_MODEL), lambda i: (i, 0)),
                  pl.BlockSpec((None, 1, D_MODEL), lambda i: (layer, 0, 0)),
                  _resident((None, D_MODEL, D_FF), lambda i: (layer, 0, 0)),
                  _resident((None, D_MODEL, D_FF), lambda i: (layer, 0, 1)),
                  _resident((None, D_FF, D_MODEL), lambda i: (layer, 0, 0))],
        out_specs=pl.BlockSpec((tm, D_MODEL), lambda i: (i, 0)),
        out_shape=jax.ShapeDtypeStruct((t, D_MODEL), F32),
        name="ffn",
        compiler_params=_params(1),
    )(x, gain, w_in, w_in, w_out)


def _inproj_kernel(x_ref, gm_ref, cos_ref, sin_ref, w_ref, gains_ref, ones_ref,
                   qa1, ka1, va1, qa2, ka2, va2, qa3, ka3, va3, qb, kb, vb, qc, kc, vc, gl, cr,
                   hs_ref, hp_ref, tb_ref):
    tm = x_ref.shape[0]
    n_slab = D_MODEL // LANES
    h = _rms(x_ref[...], gm_ref[...])
    h_bf = h.astype(BF16)
    cos0 = cos_ref[...]
    sin0 = sin_ref[...]
    ones = ones_ref[...]
    gains = gains_ref[...]

    first_half = (lax.broadcasted_iota(jnp.int32, (1, LANES), 1) % HEAD_DIM) < (HEAD_DIM // 2)

    def norm_rope(y, gain, cos, sin):
        ss = jnp.dot((y * y).astype(BF16), ones, preferred_element_type=F32)
        yn = y * lax.rsqrt(ss * (1.0 / HEAD_DIM) + NORM_EPS) * gain
        partner = jnp.where(first_half, pltpu.roll(yn, LANES - HEAD_DIM // 2, 1), pltpu.roll(yn, HEAD_DIM // 2, 1))
        return yn * cos + partner * sin

    for s in range(n_slab):
        hs_ref[s] = h[:, s * LANES:(s + 1) * LANES]
    tb_ref[0] = cos0
    tb_ref[1] = sin0

    def class_major(d):
        if d == 1:
            return h_bf, cos0, sin0
        rows = tm // d
        for c in range(d):
            for s in range(n_slab):
                hp_ref[c * rows:(c + 1) * rows, s * LANES:(s + 1) * LANES] = (
                    hs_ref[s, pl.ds(c, rows, stride=d), :].astype(BF16))
        cos = jnp.concatenate([tb_ref[0, pl.ds(c, rows, stride=d), :] for c in range(d)], axis=0)
        sin = jnp.concatenate([tb_ref[1, pl.ds(c, rows, stride=d), :] for c in range(d)], axis=0)
        return hp_ref[...], cos, sin

    def store_classes(ref, col, val, d):
        rows = tm // d
        for c in range(d):
            ref[c, :, col:col + LANES] = val[c * rows:(c + 1) * rows, :]

    for gi, (qo, ko, vo) in enumerate(((qa1, ka1, va1), (qa2, ka2, va2), (qa3, ka3, va3))):
        d = A_CONFIGS[gi][1]
        hp, cos, sin = class_major(d)
        yq, yk, yv = [jnp.dot(hp, w_ref[:, pl.ds(OFF_A + part * A_SEG + gi * A_GROUP_WIDTH, A_GROUP_WIDTH)],
                              preferred_element_type=F32) for part in range(3)]
        for j in range(A_GROUP_WIDTH // LANES):
            c0 = j * LANES
            store_classes(qo, c0, (norm_rope(yq[:, c0:c0 + LANES], gains[0:1], cos, sin) * Q_SCALE).astype(BF16), d)
            store_classes(ko, c0, norm_rope(yk[:, c0:c0 + LANES], gains[1:2], cos, sin).astype(BF16), d)
            store_classes(vo, c0, yv[:, c0:c0 + LANES].astype(BF16), d)

    y = jnp.dot(h_bf, w_ref[:, OFF_B:OFF_B + B_SEG], preferred_element_type=F32)
    for j in range(4):
        c0 = j * LANES
        qb[0, :, c0:c0 + LANES] = (norm_rope(y[:, c0:c0 + LANES], gains[2:3], cos0, sin0) * Q_SCALE).astype(BF16)
    for j in range(2):
        c0 = j * LANES
        kb[0, :, c0:c0 + LANES] = norm_rope(y[:, 512 + c0:512 + c0 + LANES], gains[3:4], cos0, sin0).astype(BF16)
        vb[0, :, c0:c0 + LANES] = y[:, 768 + c0:768 + c0 + LANES].astype(BF16)

    y = jnp.dot(h_bf, w_ref[:, OFF_C:OFF_C + C_SEG], preferred_element_type=F32)
    qc[...] = y[:, 0:256].astype(BF16)
    kc[...] = y[:, 256:512].astype(BF16)
    vc[...] = y[:, 512:1024].astype(BF16)
    gl[...] = y[:, 1024:1152].astype(BF16)
    cr[...] = y[:, 1152:1664].astype(BF16)


def _inproj(x, gain, cos, sin, w_packed, head_gains, ones_bd, layer, batch, seq):
    t = x.shape[0]
    tm = TOKEN_TILE
    tiles_per_seq = seq // tm

    def cm_spec(d, width):
        return pl.BlockSpec((None, d, tm // d, width),
                            lambda i: (i // tiles_per_seq, 0, i % tiles_per_seq, 0))

    def cm_shape(d, width):
        return jax.ShapeDtypeStruct((batch, d, seq // d, width), BF16)

    def tok_spec(width):
        return pl.BlockSpec((tm, width), lambda i: (i, 0))

    out_specs, out_shape = [], []
    for _, d in A_CONFIGS:
        for _ in range(3):
            out_specs.append(cm_spec(d, A_GROUP_WIDTH))
            out_shape.append(cm_shape(d, A_GROUP_WIDTH))
    for width in (512, 256, 256):
        out_specs.append(cm_spec(1, width))
        out_shape.append(cm_shape(1, width))
    for width in (256, 256, 512, 128, 512):
        out_specs.append(tok_spec(width))
        out_shape.append(jax.ShapeDtypeStruct((t, width), BF16))

    return pl.pallas_call(
        _inproj_kernel,
        grid=(t // tm,),
        in_specs=[tok_spec(D_MODEL),
                  pl.BlockSpec((None, 1, D_MODEL), lambda i: (layer, 0, 0)),
                  tok_spec(LANES), tok_spec(LANES),
                  _resident((None, D_MODEL, W_PACKED), lambda i: (layer, 0, 0)),
                  pl.BlockSpec((None, 4, LANES), lambda i: (layer, 0, 0)),
                  pl.BlockSpec((LANES, LANES), lambda i: (0, 0))],
        out_specs=out_specs,
        out_shape=out_shape,
        scratch_shapes=[pltpu.VMEM((D_MODEL // LANES, tm, LANES), F32),
                        pltpu.VMEM((tm, D_MODEL), BF16),
                        pltpu.VMEM((2, tm, LANES), F32)],
        name="inproj",
        compiler_params=_params(1),
    )(x, gain, cos, sin, w_packed, head_gains, ones_bd)


def _attn_kernel(d, nb, window, kv_div, use_sink, *refs):
    if use_sink:
        sink_ref, q_ref, kc_ref, kp_ref, vc_ref, vp_ref, o_ref, lse_ref, osc, lsc, s_scr, p_scr, st_scr = refs
    else:
        q_ref, kc_ref, kp_ref, vc_ref, vp_ref, o_ref, lse_ref, osc, lsc, s_scr, p_scr, st_scr = refs
    chunk = pl.program_id(1)
    lane = lax.broadcasted_iota(jnp.int32, (1, LANES), 1)
    head_of_lane = lane // HEAD_DIM
    key = lax.broadcasted_iota(jnp.int32, (2 * BLOCK, 2 * BLOCK), 0)
    col = lax.broadcasted_iota(jnp.int32, (2 * BLOCK, 2 * BLOCK), 1)
    dist = (col % BLOCK) + BLOCK - key
    band = (dist >= 0) & (dist <= window)
    col_row = lax.broadcasted_iota(jnp.int32, (1, 2 * BLOCK), 1)
    sub8 = lax.broadcasted_iota(jnp.int32, (8, LANES), 0)
    nt = (((1,), (1,)), ((), ()))
    tn = (((0,), (0,)), ((), ()))

    n_units = d * nb

    def locate(u):
        c = u // nb if d > 1 else 0
        j = u % nb if nb > 1 else 0
        if isinstance(j, int):
            return c, j, j * BLOCK, max(j - 1, 0) * BLOCK
        return c, j, pl.multiple_of(j * BLOCK, BLOCK), pl.multiple_of(jnp.maximum(j - 1, 0) * BLOCK, BLOCK)

    def window_rows(cur_ref, prev_ref, u, p):
        c, j, row0, prow = locate(u)
        kcol = (p // kv_div) * LANES
        ksl = slice(kcol, kcol + LANES)
        cur = cur_ref[c, pl.ds(row0, BLOCK), ksl]
        prev = prev_ref[c, :, ksl]
        if nb > 1:
            prev = jnp.where(j == 0, prev, cur_ref[c, pl.ds(prow, BLOCK), ksl])
        return jnp.concatenate([prev, cur], axis=0)

    def scores(u):
        c, j, row0, _ = locate(u)
        seq_start = jnp.logical_and(chunk == 0, j == 0)
        valid = band & jnp.logical_or(key >= BLOCK, jnp.logical_not(seq_start))
        for p in range(4):
            qp = q_ref[c, pl.ds(row0, BLOCK), p * LANES:(p + 1) * LANES]
            q2 = jnp.concatenate([jnp.where(head_of_lane == e, qp, jnp.zeros_like(qp)) for e in range(2)], axis=0)
            s = lax.dot_general(window_rows(kc_ref, kp_ref, u, p), q2, nt,
                                preferred_element_type=F32)
            s = jnp.where(valid, s, -jnp.inf)
            s_scr[p] = s
            m = jnp.max(s, axis=0, keepdims=True)
            if use_sink:
                m = jnp.maximum(m, sink_row(p))
            st_scr[p, 0:1, :] = m

    def sink_row(p):
        return jnp.where(col_row < BLOCK, sink_ref[2 * p], sink_ref[2 * p + 1]) * LOG2E

    def softmax(u):
        for p in range(4):
            m = st_scr[p, 0:1, :]
            pr = jnp.exp2(s_scr[p] - m)
            den = jnp.sum(pr, axis=0, keepdims=True)
            if use_sink:
                den = den + jnp.exp2(sink_row(p) - m)
            p_scr[p] = pr.astype(BF16)
            st_scr[p, 1:2, :] = den
            st_scr[p, 2:3, :] = m

    def outputs(u):
        c, _, row0, _ = locate(u)
        start = row0 * d + c
        rows = pl.ds(start, BLOCK, stride=d) if d > 1 else pl.ds(start, BLOCK)
        lse8 = jnp.zeros((8, LANES), F32)
        for p in range(4):
            ob = lax.dot_general(window_rows(vc_ref, vp_ref, u, p), p_scr[p], tn,
                                 preferred_element_type=F32)
            m = st_scr[p, 2:3, :]
            den = st_scr[p, 1:2, :]
            inv = 1.0 / den
            o_t = jnp.concatenate([ob[0:HEAD_DIM, 0:BLOCK] * inv[:, 0:BLOCK],
                                   ob[HEAD_DIM:LANES, BLOCK:2 * BLOCK] * inv[:, BLOCK:2 * BLOCK]], axis=0)
            if d > 1:
                osc[p, rows, :] = o_t.T
            else:
                o_ref[rows, p * LANES:(p + 1) * LANES] = o_t.T.astype(BF16)
            l2 = m + jnp.log2(den)
            lse8 = jnp.where(sub8 == 2 * p, l2[:, 0:BLOCK], lse8)
            lse8 = jnp.where(sub8 == 2 * p + 1, l2[:, BLOCK:2 * BLOCK], lse8)
        lse_t = jnp.concatenate([lse8, jnp.zeros((LANES - 8, LANES), F32)], axis=0)
        if d > 1:
            lsc[rows, :] = lse_t.T
        else:
            lse_ref[rows, :] = lse_t.T

    scores(0)
    softmax(0)
    scores(1)

    def body(u, carry):
        outputs(u - 2)
        softmax(u - 1)
        scores(u)
        return carry

    lax.fori_loop(2, n_units, body, 0)
    outputs(n_units - 2)
    softmax(n_units - 1)
    outputs(n_units - 1)
    if d > 1:
        for p in range(4):
            o_ref[:, p * LANES:(p + 1) * LANES] = osc[p].astype(BF16)
        lse_ref[...] = lsc[...]


def _attention(q, k, v, sinks, *, window, kv_div, batch, seq):
    d = q.shape[1]
    nb = ATTN_CHUNK // (d * BLOCK)
    kw = k.shape[-1]
    rows = nb * BLOCK
    cur = lambda b, i: (b, 0, i, 0)
    prev = lambda b, i: (b, 0, jnp.maximum(i * nb - 1, 0), 0)
    in_specs = [pl.BlockSpec((None, d, rows, A_GROUP_WIDTH), cur),
                pl.BlockSpec((None, d, rows, kw), cur),
                pl.BlockSpec((None, d, BLOCK, kw), prev),
                pl.BlockSpec((None, d, rows, kw), cur),
                pl.BlockSpec((None, d, BLOCK, kw), prev)]
    args = [q, k, k, v, v]
    use_sink = sinks is not None
    if use_sink:
        in_specs = [pl.BlockSpec(memory_space=pltpu.SMEM)] + in_specs
        args = [sinks] + args
    return pl.pallas_call(
        functools.partial(_attn_kernel, d, nb, window, kv_div, use_sink),
        grid=(batch, seq // ATTN_CHUNK),
        in_specs=in_specs,
        out_specs=[pl.BlockSpec((None, ATTN_CHUNK, A_GROUP_WIDTH), lambda b, i: (b, i, 0)),
                   pl.BlockSpec((None, ATTN_CHUNK, LANES), lambda b, i: (b, i, 0))],
        out_shape=[jax.ShapeDtypeStruct((batch, seq, A_GROUP_WIDTH), BF16),
                   jax.ShapeDtypeStruct((batch, seq, LANES), F32)],
        scratch_shapes=[pltpu.VMEM((4, ATTN_CHUNK if d > 1 else 8, LANES), F32),
                        pltpu.VMEM((ATTN_CHUNK if d > 1 else 8, LANES), F32),
                        pltpu.VMEM((4, 2 * BLOCK, 2 * BLOCK), F32),
                        pltpu.VMEM((4, 2 * BLOCK, 2 * BLOCK), BF16),
                        pltpu.VMEM((4, 8, 2 * BLOCK), F32)],
        name=f"attn_d{d}_w{window}",
        compiler_params=_params(2),
    )(*args)


def _gla_kernel(q_ref, k_ref, v_ref, gl_ref, cr_ref, gu_ref, gb_ref, on_ref, tri_ref, y_ref, st_ref):
    tm = q_ref.shape[0]
    width = C_HEADS * C_DK
    n_chunk = GLA_SUB // C_CHUNK

    @pl.when(pl.program_id(1) == 0)
    def _():
        st_ref[...] = jnp.zeros_like(st_ref)

    tri = tri_ref[...]
    keep = tri > 0
    on_gain = on_ref[...]
    head_of_lane = lax.broadcasted_iota(jnp.int32, (1, LANES), 1) // C_DK
    nt = (((1,), (1,)), ((), ()))
    tn = (((0,), (0,)), ((), ()))

    def chunked(a):
        return a.reshape(n_chunk, C_CHUNK, width)

    for sub in range(tm // GLA_SUB):
        rows = slice(sub * GLA_SUB, (sub + 1) * GLA_SUB)
        x = jnp.dot(gl_ref[rows, :], gu_ref[...], preferred_element_type=F32) + gb_ref[...]
        lg = (jnp.minimum(x, 0.0) - jnp.log(1.0 + jnp.exp(-jnp.abs(x)))) * (1.0 / C_TAU)
        g1 = lg.astype(BF16)
        g2 = (lg - g1.astype(F32)).astype(BF16)
        bs = jnp.dot(tri, jnp.concatenate([g1, g2], axis=1), preferred_element_type=F32)
        b = chunked(bs[:, 0:width] + bs[:, width:2 * width])
        b_mid = b[:, C_CHUNK // 2 - 1:C_CHUNK // 2, :]
        b_last = b[:, C_CHUNK - 1:C_CHUNK, :]
        q = chunked(q_ref[rows, :].astype(F32) * (C_DK ** -0.5))
        k = chunked(k_ref[rows, :].astype(F32))
        v = v_ref[rows, :]
        qd = (q * jnp.exp(b)).astype(BF16)
        qs = (q * jnp.exp(b - b_mid)).astype(BF16)
        ks = (k * jnp.exp(b_mid - b)).astype(BF16)
        kd = (k * jnp.exp(b_last - b)).astype(BF16)
        decay = jnp.exp(b_last)
        for hh in range(C_HEADS):
            cols = slice(hh * LANES, (hh + 1) * LANES)
            kcols = slice((hh // 2) * LANES, (hh // 2 + 1) * LANES)
            mine = head_of_lane == (hh % 2)
            qs_h = jnp.where(mine, qs[:, :, kcols], jnp.zeros((), BF16))
            qd_h = jnp.where(mine, qd[:, :, kcols], jnp.zeros((), BF16))
            att = lax.dot_general(qs_h.reshape(GLA_SUB, LANES), ks[:, :, kcols].reshape(GLA_SUB, LANES),
                                  nt, preferred_element_type=F32)
            att = jnp.where(keep, att, 0.0).astype(BF16)
            o = jnp.dot(att, v[:, cols], preferred_element_type=F32)
            st = st_ref[hh]
            inter = []
            for c in range(n_chunk):
                inter.append(lax.dot_general(qd_h[c], st.astype(BF16), nt, preferred_element_type=F32))
                inc = lax.dot_general(v[c * C_CHUNK:(c + 1) * C_CHUNK, cols], kd[c, :, kcols], tn,
                                      preferred_element_type=F32)
                st = st * decay[c, :, kcols] + inc
            st_ref[hh] = st
            o = o + jnp.concatenate(inter, axis=0)
            gate = cr_ref[rows, cols].astype(F32)
            y_ref[rows, cols] = (_rms(o, on_gain) * (gate * jax.nn.sigmoid(gate))).astype(BF16)


def _gla(q, k, v, gl, cr, gate_up, gate_bias, out_norm, tri, layer, batch, seq):
    t = q.shape[0]
    tm = TOKEN_TILE
    per_seq = seq // tm
    width = C_HEADS * C_DV
    kwidth = C_HEADS * C_DK
    tok = lambda w: pl.BlockSpec((tm, w), lambda b, i: (b * per_seq + i, 0))
    return pl.pallas_call(
        _gla_kernel,
        grid=(batch, per_seq),
        in_specs=[tok(kwidth), tok(kwidth), tok(width), tok(LANES), tok(width),
                  pl.BlockSpec((None, LANES, kwidth), lambda b, i: (layer, 0, 0)),
                  pl.BlockSpec((None, 1, kwidth), lambda b, i: (layer, 0, 0)),
                  pl.BlockSpec((None, 1, C_DV), lambda b, i: (layer, 0, 0)),
                  pl.BlockSpec((GLA_SUB, GLA_SUB), lambda b, i: (0, 0))],
        out_specs=tok(width),
        out_shape=jax.ShapeDtypeStruct((t, width), BF16),
        scratch_shapes=[pltpu.VMEM((C_HEADS, C_DV, LANES), F32)],
        name="gla",
        compiler_params=_params(2),
    )(q, k, v, gl, cr, gate_up, gate_bias, out_norm, tri)


def _merge_kernel(x_ref, gm_ref, o1, o2, o3, l1, l2, l3, yb_ref, yc_ref, wg_ref, wb_ref, wo_ref, ex_ref, out_ref):
    x = x_ref[...]
    h = _rms(x, gm_ref[...]).astype(BF16)
    la, lb, lc = l1[...], l2[...], l3[...]
    top = jnp.maximum(jnp.maximum(la, lb), lc)
    ea, eb, ec = jnp.exp2(la - top), jnp.exp2(lb - top), jnp.exp2(lc - top)
    inv = 1.0 / (ea + eb + ec)
    expand = ex_ref[...]
    low_lanes = lax.broadcasted_iota(jnp.int32, (1, LANES), 1) < LANES // 2
    ya = None
    for e, o in ((ea, o1), (eb, o2), (ec, o3)):
        w = e * inv
        w_hi = w.astype(BF16)
        w_lo = pltpu.roll(w - w_hi.astype(F32), LANES // 2, 1).astype(BF16)
        wx = jnp.dot(jnp.where(low_lanes, w_hi, w_lo), expand, preferred_element_type=F32)
        term = wx * o[...].astype(F32)
        ya = term if ya is None else ya + term
    merged = None
    for i, y in enumerate((ya.astype(BF16), yb_ref[...], yc_ref[...])):
        gate = jax.nn.sigmoid(jnp.dot(h, wg_ref[:, i * D_MODEL:(i + 1) * D_MODEL], preferred_element_type=F32))
        term = gate * jnp.dot(y, wb_ref[i], preferred_element_type=F32)
        merged = term if merged is None else merged + term
    out_ref[...] = x + jnp.dot(merged.astype(BF16), wo_ref[...], preferred_element_type=F32)


def _merge(x, gain, oa, la, yb, yc, w_gate, w_branch, w_out, expand, layer):
    t = x.shape[0]
    tm = TOKEN_TILE
    tok = lambda w: pl.BlockSpec((tm, w), lambda i: (i, 0))
    bw = A_GROUP_WIDTH
    return pl.pallas_call(
        _merge_kernel,
        grid=(t // tm,),
        in_specs=[tok(D_MODEL),
                  pl.BlockSpec((None, 1, D_MODEL), lambda i: (layer, 0, 0)),
                  tok(bw), tok(bw), tok(bw), tok(LANES), tok(LANES), tok(LANES), tok(bw), tok(bw),
                  _resident((None, D_MODEL, N_BRANCH * D_MODEL), lambda i: (layer, 0, 0)),
                  _resident((None, N_BRANCH, bw, D_MODEL), lambda i: (layer, 0, 0, 0)),
                  _resident((None, D_MODEL, D_MODEL), lambda i: (layer, 0, 0)),
                  pl.BlockSpec((LANES, bw), lambda i: (0, 0))],
        out_specs=tok(D_MODEL),
        out_shape=jax.ShapeDtypeStruct((t, D_MODEL), F32),
        name="merge",
        compiler_params=_params(1),
    )(x, gain, *oa, *la, yb, yc, w_gate, w_branch, w_out, expand)


def _pack_w_in(w_in):
    n_layer = w_in.shape[0]
    lead = w_in.shape[:2]
    edges = np.cumsum([0, 4608, 512, 256, 256, 256, 512, C_GATE_RANK, 512, N_BRANCH * D_MODEL])
    seg = [w_in[:, :, edges[i]:edges[i + 1]] for i in range(9)]
    a, b_q, b_kv, c_q, c_k, c_v, c_glow, c_r, gates = seg
    b_kv = b_kv.reshape(lead + (2, 2, 1, HEAD_DIM))
    b_kv = jnp.broadcast_to(b_kv, lead + (2, 2, 2, HEAD_DIM)).reshape(lead + (512,))
    c_glow = jnp.pad(c_glow, ((0, 0), (0, 0), (0, LANES - C_GATE_RANK)))
    packed = jnp.concatenate([a, b_q, b_kv, c_q, c_k, c_v, c_glow, c_r], axis=-1)
    assert packed.shape == (n_layer, D_MODEL, W_PACKED)
    return packed.astype(BF16), gates.astype(BF16)


def kernel(x, positions, norm_ffn1, w_ffn1_in, w_ffn1_out, norm_mix, w_in, a_q_norm, a_k_norm, b_q_norm, b_k_norm,
           b_sinks, c_gate_up, c_gate_bias, c_out_norm, w_branch, w_out, norm_ffn2, w_ffn2_in, w_ffn2_out):
    batch, seq, _ = x.shape
    n_layer = w_in.shape[0]
    assert seq % ATTN_CHUNK == 0 and x.shape[2] == D_MODEL

    w_packed, w_gate = _pack_w_in(w_in)
    w1_in, w1_out = w_ffn1_in.astype(BF16), w_ffn1_out.astype(BF16)
    w2_in, w2_out = w_ffn2_in.astype(BF16), w_ffn2_out.astype(BF16)
    wb, wo = w_branch.astype(BF16), w_out.astype(BF16)
    row = lambda g: g[:, None, :]
    head_gains = jnp.stack([jnp.tile(g, (1, LANES // HEAD_DIM)) for g in (a_q_norm, a_k_norm, b_q_norm, b_k_norm)],
                           axis=1)
    gate_up = jnp.pad(c_gate_up, ((0, 0), (0, LANES - C_GATE_RANK), (0, 0))).astype(BF16)
    gate_bias = row(c_gate_bias)

    lane_head = np.arange(LANES) // HEAD_DIM
    ones_bd = jnp.asarray(lane_head[:, None] == lane_head[None, :], BF16)
    expand = jnp.asarray((np.arange(LANES) % (LANES // 2))[:, None] == (np.arange(A_GROUP_WIDTH) // HEAD_DIM)[None, :],
                         BF16)
    idx = np.arange(GLA_SUB)
    tri = jnp.asarray((idx[:, None] >= idx[None, :]) & (idx[:, None] // C_CHUNK == idx[None, :] // C_CHUNK), BF16)

    cos, sin = _rope_tables(positions)
    xt = x.reshape(batch * seq, D_MODEL)
    for l in range(n_layer):
        xt = _ffn(xt, row(norm_ffn1), w1_in, w1_out, l)
        (qa1, ka1, va1, qa2, ka2, va2, qa3, ka3, va3, qb, kb, vb, qc, kc, vc, gl, cr) = _inproj(
            xt, row(norm_mix), cos, sin, w_packed, head_gains, ones_bd, l, batch, seq)
        oa, la = [], []
        for (window, d), (q, k, v) in zip(A_CONFIGS, ((qa1, ka1, va1), (qa2, ka2, va2), (qa3, ka3, va3))):
            o, lse = _attention(q, k, v, None, window=window // d, kv_div=1, batch=batch, seq=seq)
            oa.append(o.reshape(batch * seq, A_GROUP_WIDTH))
            la.append(lse.reshape(batch * seq, LANES))
        yb, _ = _attention(qb, kb, vb, b_sinks[l], window=B_WINDOW - 1, kv_div=2, batch=batch, seq=seq)
        yb = yb.reshape(batch * seq, A_GROUP_WIDTH)
        yc = _gla(qc, kc, vc, gl, cr, gate_up, gate_bias, row(c_out_norm), tri, l, batch, seq)
        xt = _merge(xt, row(norm_mix), oa, la, yb, yc, w_gate, wb, wo, expand, l)
        xt = _ffn(xt, row(norm_ffn2), w2_in, w2_out, l)
    return xt.reshape(batch, seq, D_MODEL)
```

```python
import functools

import numpy as np
import jax
import jax.numpy as jnp
from jax import lax
from jax.experimental import pallas as pl
from jax.experimental.pallas import tpu as pltpu

F32 = jnp.float32
BF16 = jnp.bfloat16

D_MODEL = 1024
D_FF = 2816
HEAD_DIM = 64
NORM_EPS = 1e-6
ROPE_THETA = 10000.0
LOG2E = float(np.log2(np.e))
Q_SCALE = HEAD_DIM ** -0.5 * LOG2E
LANES = 128
BLOCK = 128
A_CONFIGS = ((128, 1), (512, 4), (2048, 16))
A_GROUP_WIDTH = 512
B_WINDOW = 128
C_HEADS = 4
C_DK = 64
C_DV = 128
C_GATE_RANK = 16
C_TAU = 16.0
C_CHUNK = 64
GLA_SUB = 256
N_BRANCH = 3

A_SEG = 3 * A_GROUP_WIDTH
OFF_A = 0
OFF_B = 3 * A_SEG
B_SEG = 1024
OFF_C = OFF_B + B_SEG
C_SEG = 1664
W_PACKED = OFF_C + C_SEG

TOKEN_TILE = 512
ATTN_CHUNK = 2048
MXU_COLS = 256
FFN_CHUNKS = ((0, 1536), (1536, D_FF))
VMEM_LIMIT = 56 * 2**20


def _params(n_axes):
    return pltpu.CompilerParams(dimension_semantics=("arbitrary",) * n_axes, vmem_limit_bytes=VMEM_LIMIT)


def _rms(x, gain):
    return x * lax.rsqrt(jnp.mean(x * x, axis=-1, keepdims=True) + NORM_EPS) * gain


def _resident(block, index_map):
    return pl.BlockSpec(block, index_map, pipeline_mode=pl.Buffered(1))


def _rope_kernel(pos_ref, inv_ref, sgn_ref, cos_ref, sin_ref):
    ang = pos_ref[...].astype(F32) * inv_ref[...]
    cos_ref[...] = jnp.cos(ang)
    sin_ref[...] = jnp.sin(ang) * sgn_ref[...]


def _rope_tables(positions):
    t = positions.size
    inv = ROPE_THETA ** (-jnp.arange(0, HEAD_DIM, 2, dtype=F32) / HEAD_DIM)
    inv = jnp.tile(inv, LANES // (HEAD_DIM // 2))[None, :]
    sgn = jnp.tile(jnp.concatenate([-jnp.ones(HEAD_DIM // 2, F32), jnp.ones(HEAD_DIM // 2, F32)]),
                   LANES // HEAD_DIM)[None, :]
    tm = TOKEN_TILE
    tab = jax.ShapeDtypeStruct((t, LANES), F32)
    return pl.pallas_call(
        _rope_kernel,
        grid=(t // tm,),
        in_specs=[pl.BlockSpec((tm, LANES), lambda i: (i, 0)),
                  pl.BlockSpec((1, LANES), lambda i: (0, 0)),
                  pl.BlockSpec((1, LANES), lambda i: (0, 0))],
        out_specs=[pl.BlockSpec((tm, LANES), lambda i: (i, 0))] * 2,
        out_shape=[tab, tab],
        name="rope",
        compiler_params=_params(1),
    )(jnp.broadcast_to(positions.reshape(t, 1), (t, LANES)), inv, sgn)


def _ffn_kernel(x_ref, g_ref, wg_ref, wu_ref, wo_ref, o_ref):
    x = x_ref[...]
    h = _rms(x, g_ref[...]).astype(BF16)
    acc = None
    for lo, hi in FFN_CHUNKS:
        sl = slice(lo, hi)
        g = jnp.dot(h, wg_ref[:, sl], preferred_element_type=F32)
        u = jnp.dot(h, wu_ref[:, sl], preferred_element_type=F32)
        a = (g * jax.nn.sigmoid(g) * u).astype(BF16)
        part = jnp.dot(a, wo_ref[sl, :], preferred_element_type=F32)
        acc = part if acc is None else acc + part
    o_ref[...] = x + 0.5 * acc


def _ffn(x, gain, w_in, w_out, layer):
    t = x.shape[0]
    tm = TOKEN_TILE
    return pl.pallas_call(
        _ffn_kernel,
        grid=(t // tm,),
        in_specs=[pl.BlockSpec((tm, D_MODEL), lambda i: (i, 0)),
                  pl.BlockSpec((None, 1, D_MODEL), lambda i: (layer, 0, 0)),
                  _resident((None, D_MODEL, D_FF), lambda i: (layer, 0, 0)),
                  _resident((None, D_MODEL, D_FF), lambda i: (layer, 0, 1)),
                  _resident((None, D_FF, D_MODEL), lambda i: (layer, 0, 0))],
        out_specs=pl.BlockSpec((tm, D_MODEL), lambda i: (i, 0)),
        out_shape=jax.ShapeDtypeStruct((t, D_MODEL), F32),
        name="ffn",
        compiler_params=_params(1),
    )(x, gain, w_in, w_in, w_out)


def _inproj_kernel(x_ref, gm_ref, cos_ref, sin_ref, w_ref, gains_ref, ones_ref,
                   qa1, ka1, va1, qa2, ka2, va2, qa3, ka3, va3, qb, kb, vb, qc, kc, vc, gl, cr,
                   hs_ref, hp_ref, tb_ref):
    tm = x_ref.shape[0]
    n_slab = D_MODEL // LANES
    h = _rms(x_ref[...], gm_ref[...])
    h_bf = h.astype(BF16)
    cos0 = cos_ref[...]
    sin0 = sin_ref[...]
    ones = ones_ref[...]
    gains = gains_ref[...]

    first_half = (lax.broadcasted_iota(jnp.int32, (1, LANES), 1) % HEAD_DIM) < (HEAD_DIM // 2)

    def norm_rope(y, gain, cos, sin):
        ss = jnp.dot((y * y).astype(BF16), ones, preferred_element_type=F32)
        yn = y * lax.rsqrt(ss * (1.0 / HEAD_DIM) + NORM_EPS) * gain
        partner = jnp.where(first_half, pltpu.roll(yn, LANES - HEAD_DIM // 2, 1), pltpu.roll(yn, HEAD_DIM // 2, 1))
        return yn * cos + partner * sin

    for s in range(n_slab):
        hs_ref[s] = h[:, s * LANES:(s + 1) * LANES]
    tb_ref[0] = cos0
    tb_ref[1] = sin0

    def class_major(d):
        if d == 1:
            return h_bf, cos0, sin0
        rows = tm // d
        for c in range(d):
            for s in range(n_slab):
                hp_ref[c * rows:(c + 1) * rows, s * LANES:(s + 1) * LANES] = (
                    hs_ref[s, pl.ds(c, rows, stride=d), :].astype(BF16))
        cos = jnp.concatenate([tb_ref[0, pl.ds(c, rows, stride=d), :] for c in range(d)], axis=0)
        sin = jnp.concatenate([tb_ref[1, pl.ds(c, rows, stride=d), :] for c in range(d)], axis=0)
        return hp_ref[...], cos, sin

    def store_classes(ref, col, val, d):
        rows = tm // d
        for c in range(d):
            ref[c, :, col:col + LANES] = val[c * rows:(c + 1) * rows, :]

    for gi, (qo, ko, vo) in enumerate(((qa1, ka1, va1), (qa2, ka2, va2), (qa3, ka3, va3))):
        d = A_CONFIGS[gi][1]
        hp, cos, sin = class_major(d)
        yq, yk, yv = [jnp.dot(hp, w_ref[:, pl.ds(OFF_A + part * A_SEG + gi * A_GROUP_WIDTH, A_GROUP_WIDTH)],
                              preferred_element_type=F32) for part in range(3)]
        for j in range(A_GROUP_WIDTH // LANES):
            c0 = j * LANES
            store_classes(qo, c0, (norm_rope(yq[:, c0:c0 + LANES], gains[0:1], cos, sin) * Q_SCALE).astype(BF16), d)
            store_classes(ko, c0, norm_rope(yk[:, c0:c0 + LANES], gains[1:2], cos, sin).astype(BF16), d)
            store_classes(vo, c0, yv[:, c0:c0 + LANES].astype(BF16), d)

    y = jnp.dot(h_bf, w_ref[:, OFF_B:OFF_B + B_SEG], preferred_element_type=F32)
    for j in range(4):
        c0 = j * LANES
        qb[0, :, c0:c0 + LANES] = (norm_rope(y[:, c0:c0 + LANES], gains[2:3], cos0, sin0) * Q_SCALE).astype(BF16)
    for j in range(2):
        c0 = j * LANES
        kb[0, :, c0:c0 + LANES] = norm_rope(y[:, 512 + c0:512 + c0 + LANES], gains[3:4], cos0, sin0).astype(BF16)
        vb[0, :, c0:c0 + LANES] = y[:, 768 + c0:768 + c0 + LANES].astype(BF16)

    y = jnp.dot(h_bf, w_ref[:, OFF_C:OFF_C + C_SEG], preferred_element_type=F32)
    qc[...] = y[:, 0:256].astype(BF16)
    kc[...] = y[:, 256:512].astype(BF16)
    vc[...] = y[:, 512:1024].astype(BF16)
    gl[...] = y[:, 1024:1152].astype(BF16)
    cr[...] = y[:, 1152:1664].astype(BF16)


def _inproj(x, gain, cos, sin, w_packed, head_gains, ones_bd, layer, batch, seq):
    t = x.shape[0]
    tm = TOKEN_TILE
    tiles_per_seq = seq // tm

    def cm_spec(d, width):
        return pl.BlockSpec((None, d, tm // d, width),
                            lambda i: (i // tiles_per_seq, 0, i % tiles_per_seq, 0))

    def cm_shape(d, width):
        return jax.ShapeDtypeStruct((batch, d, seq // d, width), BF16)

    def tok_spec(width):
        return pl.BlockSpec((tm, width), lambda i: (i, 0))

    out_specs, out_shape = [], []
    for _, d in A_CONFIGS:
        for _ in range(3):
            out_specs.append(cm_spec(d, A_GROUP_WIDTH))
            out_shape.append(cm_shape(d, A_GROUP_WIDTH))
    for width in (512, 256, 256):
        out_specs.append(cm_spec(1, width))
        out_shape.append(cm_shape(1, width))
    for width in (256, 256, 512, 128, 512):
        out_specs.append(tok_spec(width))
        out_shape.append(jax.ShapeDtypeStruct((t, width), BF16))

    return pl.pallas_call(
        _inproj_kernel,
        grid=(t // tm,),
        in_specs=[tok_spec(D_MODEL),
                  pl.BlockSpec((None, 1, D_MODEL), lambda i: (layer, 0, 0)),
                  tok_spec(LANES), tok_spec(LANES),
                  _resident((None, D_MODEL, W_PACKED), lambda i: (layer, 0, 0)),
                  pl.BlockSpec((None, 4, LANES), lambda i: (layer, 0, 0)),
                  pl.BlockSpec((LANES, LANES), lambda i: (0, 0))],
        out_specs=out_specs,
        out_shape=out_shape,
        scratch_shapes=[pltpu.VMEM((D_MODEL // LANES, tm, LANES), F32),
                        pltpu.VMEM((tm, D_MODEL), BF16),
                        pltpu.VMEM((2, tm, LANES), F32)],
        name="inproj",
        compiler_params=_params(1),
    )(x, gain, cos, sin, w_packed, head_gains, ones_bd)


def _attn_kernel(d, nb, window, kv_div, use_sink, merged, *refs):
    if use_sink:
        sink_ref, q_ref, kc_ref, kp_ref, vc_ref, vp_ref, o_ref, lse_ref, osc, lsc, s_scr, p_scr, st_scr = refs
    else:
        q_ref, kc_ref, kp_ref, vc_ref, vp_ref, o_ref, lse_ref, osc, lsc, s_scr, p_scr, st_scr = refs
    chunk = pl.program_id(1)
    lane = lax.broadcasted_iota(jnp.int32, (1, LANES), 1)
    head_of_lane = lane // HEAD_DIM
    key = lax.broadcasted_iota(jnp.int32, (2 * BLOCK, 2 * BLOCK), 0)
    col = lax.broadcasted_iota(jnp.int32, (2 * BLOCK, 2 * BLOCK), 1)
    dist = (col % BLOCK) + BLOCK - key
    band = (dist >= 0) & (dist <= window)
    col_row = lax.broadcasted_iota(jnp.int32, (1, 2 * BLOCK), 1)
    sub8 = lax.broadcasted_iota(jnp.int32, (8, LANES), 0)
    nt = (((1,), (1,)), ((), ()))
    tn = (((0,), (0,)), ((), ()))

    n_units = d * nb

    def locate(u):
        c = u // nb if d > 1 else 0
        j = u % nb if nb > 1 else 0
        if isinstance(j, int):
            return c, j, j * BLOCK, max(j - 1, 0) * BLOCK
        return c, j, pl.multiple_of(j * BLOCK, BLOCK), pl.multiple_of(jnp.maximum(j - 1, 0) * BLOCK, BLOCK)

    def window_rows(cur_ref, prev_ref, u, p, in_loop=False):
        c, j, row0, prow = locate(u)
        kcol = (p // kv_div) * LANES
        ksl = slice(kcol, kcol + LANES)
        cur = cur_ref[c, pl.ds(row0, BLOCK), ksl]
        if isinstance(j, int):
            first = j == 0
        elif d == 1 and in_loop:
            first = False
        else:
            first = None
        if first is None:
            prev = jnp.where(j == 0, prev_ref[c, :, ksl], cur_ref[c, pl.ds(prow, BLOCK), ksl])
        else:
            prev = prev_ref[c, :, ksl] if first else cur_ref[c, pl.ds(prow, BLOCK), ksl]
        return jnp.concatenate([prev, cur], axis=0)

    def scores(u, in_loop=False):
        c, j, row0, _ = locate(u)
        seq_start = jnp.logical_and(chunk == 0, j == 0)
        valid = band & jnp.logical_or(key >= BLOCK, jnp.logical_not(seq_start))
        for p in range(4):
            qp = q_ref[c, pl.ds(row0, BLOCK), p * LANES:(p + 1) * LANES]
            q2 = jnp.concatenate([jnp.where(head_of_lane == e, qp, jnp.zeros_like(qp)) for e in range(2)], axis=0)
            s = lax.dot_general(window_rows(kc_ref, kp_ref, u, p, in_loop), q2, nt,
                                preferred_element_type=F32)
            s = jnp.where(valid, s, -jnp.inf)
            s_scr[p] = s
            m = jnp.max(s, axis=0, keepdims=True)
            if use_sink:
                m = jnp.maximum(m, sink_row(p))
            st_scr[p, 0:1, :] = m

    def sink_row(p):
        return jnp.where(col_row < BLOCK, sink_ref[2 * p], sink_ref[2 * p + 1]) * LOG2E

    def probabilities(p):
        m = st_scr[p, 0:1, :]
        pr = jnp.exp2(s_scr[p] - m)
        den = jnp.sum(pr, axis=0, keepdims=True)
        if use_sink:
            den = den + jnp.exp2(sink_row(p) - m)
        return pr.astype(BF16), m, den

    def softmax(u):
        for p in range(4):
            pr, m, den = probabilities(p)
            p_scr[p] = pr
            st_scr[p, 1:2, :] = den
            st_scr[p, 2:3, :] = m

    def outputs(u, in_loop=False):
        c, _, row0, _ = locate(u)
        start = row0 * d + c
        rows = pl.ds(start, BLOCK, stride=d) if d > 1 else pl.ds(start, BLOCK)
        lse8 = jnp.zeros((8, LANES), F32)
        for p in range(4):
            if merged:
                pr, m, den = probabilities(p)
            else:
                pr, m, den = p_scr[p], st_scr[p, 2:3, :], st_scr[p, 1:2, :]
            ob = lax.dot_general(window_rows(vc_ref, vp_ref, u, p, in_loop), pr, tn,
                                 preferred_element_type=F32)
            inv = 1.0 / den
            o_t = jnp.concatenate([ob[0:HEAD_DIM, 0:BLOCK] * inv[:, 0:BLOCK],
                                   ob[HEAD_DIM:LANES, BLOCK:2 * BLOCK] * inv[:, BLOCK:2 * BLOCK]], axis=0)
            if d > 1:
                osc[p, rows, :] = o_t.T
            else:
                o_ref[rows, p * LANES:(p + 1) * LANES] = o_t.T.astype(BF16)
            l2 = m + jnp.log2(den)
            lse8 = jnp.where(sub8 == 2 * p, l2[:, 0:BLOCK], lse8)
            lse8 = jnp.where(sub8 == 2 * p + 1, l2[:, BLOCK:2 * BLOCK], lse8)
        lse_t = jnp.concatenate([lse8, jnp.zeros((LANES - 8, LANES), F32)], axis=0)
        if d > 1:
            lsc[rows, :] = lse_t.T
        else:
            lse_ref[rows, :] = lse_t.T

    if merged:
        scores(0)
        outputs(0)
        scores(1)

        def body(u, carry):
            outputs(u - 1, True)
            scores(u, True)
            return carry

        lax.fori_loop(2, n_units, body, 0)
        outputs(n_units - 1)
    else:
        scores(0)
        softmax(0)
        scores(1)
        outputs(0)
        softmax(1)
        scores(2)

        def body(u, carry):
            outputs(u - 2, True)
            softmax(u - 1)
            scores(u, True)
            return carry

        lax.fori_loop(3, n_units, body, 0)
        outputs(n_units - 2)
        softmax(n_units - 1)
        outputs(n_units - 1)
    if d > 1:
        for p in range(4):
            o_ref[:, p * LANES:(p + 1) * LANES] = osc[p].astype(BF16)
        lse_ref[...] = lsc[...]


def _attention(q, k, v, sinks, *, window, kv_div, batch, seq, merged):
    d = q.shape[1]
    nb = ATTN_CHUNK // (d * BLOCK)
    kw = k.shape[-1]
    rows = nb * BLOCK
    cur = lambda b, i: (b, 0, i, 0)
    prev = lambda b, i: (b, 0, jnp.maximum(i * nb - 1, 0), 0)
    in_specs = [pl.BlockSpec((None, d, rows, A_GROUP_WIDTH), cur),
                pl.BlockSpec((None, d, rows, kw), cur),
                pl.BlockSpec((None, d, BLOCK, kw), prev),
                pl.BlockSpec((None, d, rows, kw), cur),
                pl.BlockSpec((None, d, BLOCK, kw), prev)]
    args = [q, k, k, v, v]
    use_sink = sinks is not None
    if use_sink:
        in_specs = [pl.BlockSpec(memory_space=pltpu.SMEM)] + in_specs
        args = [sinks] + args
    return pl.pallas_call(
        functools.partial(_attn_kernel, d, nb, window, kv_div, use_sink, merged),
        grid=(batch, seq // ATTN_CHUNK),
        in_specs=in_specs,
        out_specs=[pl.BlockSpec((None, ATTN_CHUNK, A_GROUP_WIDTH), lambda b, i: (b, i, 0)),
                   pl.BlockSpec((None, ATTN_CHUNK, LANES), lambda b, i: (b, i, 0))],
        out_shape=[jax.ShapeDtypeStruct((batch, seq, A_GROUP_WIDTH), BF16),
                   jax.ShapeDtypeStruct((batch, seq, LANES), F32)],
        scratch_shapes=[pltpu.VMEM((4, ATTN_CHUNK if d > 1 else 8, LANES), F32),
                        pltpu.VMEM((ATTN_CHUNK if d > 1 else 8, LANES), F32),
                        pltpu.VMEM((4, 2 * BLOCK, 2 * BLOCK), F32),
                        pltpu.VMEM((4, 2 * BLOCK, 2 * BLOCK), BF16),
                        pltpu.VMEM((4, 8, 2 * BLOCK), F32)],
        name=f"attn_d{d}_w{window}",
        compiler_params=_params(2),
    )(*args)


def _gla_kernel(q_ref, k_ref, v_ref, gl_ref, cr_ref, gu_ref, gb_ref, on_ref, tri_ref, y_ref, st_ref):
    tm = q_ref.shape[0]
    width = C_HEADS * C_DK
    n_chunk = GLA_SUB // C_CHUNK

    @pl.when(pl.program_id(1) == 0)
    def _():
        st_ref[...] = jnp.zeros_like(st_ref)

    tri = tri_ref[...]
    keep = tri > 0
    on_gain = on_ref[...]
    head_of_lane = lax.broadcasted_iota(jnp.int32, (1, LANES), 1) // C_DK
    nt = (((1,), (1,)), ((), ()))
    tn = (((0,), (0,)), ((), ()))

    def chunked(a):
        return a.reshape(n_chunk, C_CHUNK, width)

    for sub in range(tm // GLA_SUB):
        rows = slice(sub * GLA_SUB, (sub + 1) * GLA_SUB)
        x = jnp.dot(gl_ref[rows, :], gu_ref[...], preferred_element_type=F32) + gb_ref[...]
        lg = (jnp.minimum(x, 0.0) - jnp.log(1.0 + jnp.exp(-jnp.abs(x)))) * (1.0 / C_TAU)
        g1 = lg.astype(BF16)
        g2 = (lg - g1.astype(F32)).astype(BF16)
        bs = jnp.dot(tri, jnp.concatenate([g1, g2], axis=1), preferred_element_type=F32)
        b = chunked(bs[:, 0:width] + bs[:, width:2 * width])
        b_mid = b[:, C_CHUNK // 2 - 1:C_CHUNK // 2, :]
        b_last = b[:, C_CHUNK - 1:C_CHUNK, :]
        q = chunked(q_ref[rows, :].astype(F32) * (C_DK ** -0.5))
        k = chunked(k_ref[rows, :].astype(F32))
        v = v_ref[rows, :]
        qd = (q * jnp.exp(b)).astype(BF16)
        qs = (q * jnp.exp(b - b_mid)).astype(BF16)
        ks = (k * jnp.exp(b_mid - b)).astype(BF16)
        kd = (k * jnp.exp(b_last - b)).astype(BF16)
        decay = jnp.exp(b_last)
        for hh in range(C_HEADS):
            cols = slice(hh * LANES, (hh + 1) * LANES)
            kcols = slice((hh // 2) * LANES, (hh // 2 + 1) * LANES)
            mine = head_of_lane == (hh % 2)
            qs_h = jnp.where(mine, qs[:, :, kcols], jnp.zeros((), BF16))
            qd_h = jnp.where(mine, qd[:, :, kcols], jnp.zeros((), BF16))
            att = lax.dot_general(qs_h.reshape(GLA_SUB, LANES), ks[:, :, kcols].reshape(GLA_SUB, LANES),
                                  nt, preferred_element_type=F32)
            att = jnp.where(keep, att, 0.0).astype(BF16)
            o = jnp.dot(att, v[:, cols], preferred_element_type=F32)
            st = st_ref[hh]
            inter = []
            for c in range(n_chunk):
                inter.append(lax.dot_general(qd_h[c], st.astype(BF16), nt, preferred_element_type=F32))
                inc = lax.dot_general(v[c * C_CHUNK:(c + 1) * C_CHUNK, cols], kd[c, :, kcols], tn,
                                      preferred_element_type=F32)
                st = st * decay[c, :, kcols] + inc
            st_ref[hh] = st
            o = o + jnp.concatenate(inter, axis=0)
            gate = cr_ref[rows, cols].astype(F32)
            y_ref[rows, cols] = (_rms(o, on_gain) * (gate * jax.nn.sigmoid(gate))).astype(BF16)


def _gla(q, k, v, gl, cr, gate_up, gate_bias, out_norm, tri, layer, batch, seq):
    t = q.shape[0]
    tm = TOKEN_TILE
    per_seq = seq // tm
    width = C_HEADS * C_DV
    kwidth = C_HEADS * C_DK
    tok = lambda w: pl.BlockSpec((tm, w), lambda b, i: (b * per_seq + i, 0))
    return pl.pallas_call(
        _gla_kernel,
        grid=(batch, per_seq),
        in_specs=[tok(kwidth), tok(kwidth), tok(width), tok(LANES), tok(width),
                  pl.BlockSpec((None, LANES, kwidth), lambda b, i: (layer, 0, 0)),
                  pl.BlockSpec((None, 1, kwidth), lambda b, i: (layer, 0, 0)),
                  pl.BlockSpec((None, 1, C_DV), lambda b, i: (layer, 0, 0)),
                  pl.BlockSpec((GLA_SUB, GLA_SUB), lambda b, i: (0, 0))],
        out_specs=tok(width),
        out_shape=jax.ShapeDtypeStruct((t, width), BF16),
        scratch_shapes=[pltpu.VMEM((C_HEADS, C_DV, LANES), F32)],
        name="gla",
        compiler_params=_params(2),
    )(q, k, v, gl, cr, gate_up, gate_bias, out_norm, tri)


def _merge_kernel(x_ref, gm_ref, o1, o2, o3, l1, l2, l3, yb_ref, yc_ref, wg_ref, wb_ref, wo_ref, ex_ref, out_ref):
    x = x_ref[...]
    h = _rms(x, gm_ref[...]).astype(BF16)
    la, lb, lc = l1[...], l2[...], l3[...]
    top = jnp.maximum(jnp.maximum(la, lb), lc)
    ea, eb, ec = jnp.exp2(la - top), jnp.exp2(lb - top), jnp.exp2(lc - top)
    inv = 1.0 / (ea + eb + ec)
    expand = ex_ref[...]
    low_lanes = lax.broadcasted_iota(jnp.int32, (1, LANES), 1) < LANES // 2
    ya = None
    for e, o in ((ea, o1), (eb, o2), (ec, o3)):
        w = e * inv
        w_hi = w.astype(BF16)
        w_lo = pltpu.roll(w - w_hi.astype(F32), LANES // 2, 1).astype(BF16)
        wx = jnp.dot(jnp.where(low_lanes, w_hi, w_lo), expand, preferred_element_type=F32)
        term = wx * o[...].astype(F32)
        ya = term if ya is None else ya + term
    merged = None
    for i, y in enumerate((ya.astype(BF16), yb_ref[...], yc_ref[...])):
        gate = jax.nn.sigmoid(jnp.dot(h, wg_ref[:, i * D_MODEL:(i + 1) * D_MODEL], preferred_element_type=F32))
        term = gate * jnp.dot(y, wb_ref[i], preferred_element_type=F32)
        merged = term if merged is None else merged + term
    out_ref[...] = x + jnp.dot(merged.astype(BF16), wo_ref[...], preferred_element_type=F32)


def _merge(x, gain, oa, la, yb, yc, w_gate, w_branch, w_out, expand, layer):
    t = x.shape[0]
    tm = TOKEN_TILE
    tok = lambda w: pl.BlockSpec((tm, w), lambda i: (i, 0))
    bw = A_GROUP_WIDTH
    return pl.pallas_call(
        _merge_kernel,
        grid=(t // tm,),
        in_specs=[tok(D_MODEL),
                  pl.BlockSpec((None, 1, D_MODEL), lambda i: (layer, 0, 0)),
                  tok(bw), tok(bw), tok(bw), tok(LANES), tok(LANES), tok(LANES), tok(bw), tok(bw),
                  _resident((None, D_MODEL, N_BRANCH * D_MODEL), lambda i: (layer, 0, 0)),
                  _resident((None, N_BRANCH, bw, D_MODEL), lambda i: (layer, 0, 0, 0)),
                  _resident((None, D_MODEL, D_MODEL), lambda i: (layer, 0, 0)),
                  pl.BlockSpec((LANES, bw), lambda i: (0, 0))],
        out_specs=tok(D_MODEL),
        out_shape=jax.ShapeDtypeStruct((t, D_MODEL), F32),
        name="merge",
        compiler_params=_params(1),
    )(x, gain, *oa, *la, yb, yc, w_gate, w_branch, w_out, expand)


def _pack_w_in(w_in):
    n_layer = w_in.shape[0]
    lead = w_in.shape[:2]
    edges = np.cumsum([0, 4608, 512, 256, 256, 256, 512, C_GATE_RANK, 512, N_BRANCH * D_MODEL])
    seg = [w_in[:, :, edges[i]:edges[i + 1]] for i in range(9)]
    a, b_q, b_kv, c_q, c_k, c_v, c_glow, c_r, gates = seg
    b_kv = b_kv.reshape(lead + (2, 2, 1, HEAD_DIM))
    b_kv = jnp.broadcast_to(b_kv, lead + (2, 2, 2, HEAD_DIM)).reshape(lead + (512,))
    c_glow = jnp.pad(c_glow, ((0, 0), (0, 0), (0, LANES - C_GATE_RANK)))
    packed = jnp.concatenate([a, b_q, b_kv, c_q, c_k, c_v, c_glow, c_r], axis=-1)
    assert packed.shape == (n_layer, D_MODEL, W_PACKED)
    return packed.astype(BF16), gates.astype(BF16)


def kernel(x, positions, norm_ffn1, w_ffn1_in, w_ffn1_out, norm_mix, w_in, a_q_norm, a_k_norm, b_q_norm, b_k_norm,
           b_sinks, c_gate_up, c_gate_bias, c_out_norm, w_branch, w_out, norm_ffn2, w_ffn2_in, w_ffn2_out):
    batch, seq, _ = x.shape
    n_layer = w_in.shape[0]
    assert seq % ATTN_CHUNK == 0 and x.shape[2] == D_MODEL

    w_packed, w_gate = _pack_w_in(w_in)
    w1_in, w1_out = w_ffn1_in.astype(BF16), w_ffn1_out.astype(BF16)
    w2_in, w2_out = w_ffn2_in.astype(BF16), w_ffn2_out.astype(BF16)
    wb, wo = w_branch.astype(BF16), w_out.astype(BF16)
    row = lambda g: g[:, None, :]
    head_gains = jnp.stack([jnp.tile(g, (1, LANES // HEAD_DIM)) for g in (a_q_norm, a_k_norm, b_q_norm, b_k_norm)],
                           axis=1)
    gate_up = jnp.pad(c_gate_up, ((0, 0), (0, LANES - C_GATE_RANK), (0, 0))).astype(BF16)
    gate_bias = row(c_gate_bias)

    lane_head = np.arange(LANES) // HEAD_DIM
    ones_bd = jnp.asarray(lane_head[:, None] == lane_head[None, :], BF16)
    expand = jnp.asarray((np.arange(LANES) % (LANES // 2))[:, None] == (np.arange(A_GROUP_WIDTH) // HEAD_DIM)[None, :],
                         BF16)
    idx = np.arange(GLA_SUB)
    tri = jnp.asarray((idx[:, None] >= idx[None, :]) & (idx[:, None] // C_CHUNK == idx[None, :] // C_CHUNK), BF16)

    cos, sin = _rope_tables(positions)
    xt = x.reshape(batch * seq, D_MODEL)
    for l in range(n_layer):
        xt = _ffn(xt, row(norm_ffn1), w1_in, w1_out, l)
        (qa1, ka1, va1, qa2, ka2, va2, qa3, ka3, va3, qb, kb, vb, qc, kc, vc, gl, cr) = _inproj(
            xt, row(norm_mix), cos, sin, w_packed, head_gains, ones_bd, l, batch, seq)
        oa, la = [], []
        for (window, d), (q, k, v) in zip(A_CONFIGS, ((qa1, ka1, va1), (qa2, ka2, va2), (qa3, ka3, va3))):
            o, lse = _attention(q, k, v, None, window=window // d, kv_div=1, batch=batch, seq=seq, merged=d != 4)
            oa.append(o.reshape(batch * seq, A_GROUP_WIDTH))
            la.append(lse.reshape(batch * seq, LANES))
        yb, _ = _attention(qb, kb, vb, b_sinks[l], window=B_WINDOW - 1, kv_div=2, batch=batch, seq=seq, merged=False)
        yb = yb.reshape(batch * seq, A_GROUP_WIDTH)
        yc = _gla(qc, kc, vc, gl, cr, gate_up, gate_bias, row(c_out_norm), tri, l, batch, seq)
        xt = _merge(xt, row(norm_mix), oa, la, yb, yc, w_gate, wb, wo, expand, l)
        xt = _ffn(xt, row(norm_ffn2), w2_in, w2_out, l)
    return xt.reshape(batch, seq, D_MODEL)
```

```python
import functools

import numpy as np
import jax
import jax.numpy as jnp
from jax import lax
from jax.experimental import pallas as pl
from jax.experimental.pallas import tpu as pltpu

F32 = jnp.float32
BF16 = jnp.bfloat16

D_MODEL = 1024
D_FF = 2816
HEAD_DIM = 64
NORM_EPS = 1e-6
ROPE_THETA = 10000.0
LOG2E = float(np.log2(np.e))
Q_SCALE = HEAD_DIM ** -0.5 * LOG2E
LANES = 128
BLOCK = 128
A_CONFIGS = ((128, 1), (512, 4), (2048, 16))
A_GROUP_WIDTH = 512
B_WINDOW = 128
C_HEADS = 4
C_DK = 64
C_DV = 128
C_GATE_RANK = 16
C_TAU = 16.0
C_CHUNK = 64
GLA_SUB = 256
N_BRANCH = 3

A_SEG = 3 * A_GROUP_WIDTH
OFF_A = 0
OFF_B = 3 * A_SEG
B_SEG = 1024
OFF_C = OFF_B + B_SEG
C_SEG = 1664
W_PACKED = OFF_C + C_SEG

TOKEN_TILE = 512
ATTN_CHUNK = 2048
MXU_COLS = 256
FFN_CHUNKS = ((0, 1536), (1536, D_FF))
VMEM_LIMIT = 56 * 2**20


def _params(n_axes):
    return pltpu.CompilerParams(dimension_semantics=("arbitrary",) * n_axes, vmem_limit_bytes=VMEM_LIMIT)


def _rms(x, gain):
    return x * lax.rsqrt(jnp.mean(x * x, axis=-1, keepdims=True) + NORM_EPS) * gain


def _sigmoid(x):
    return 0.5 * jnp.tanh(0.5 * x) + 0.5


def _resident(block, index_map):
    return pl.BlockSpec(block, index_map, pipeline_mode=pl.Buffered(1))


def _rope_kernel(pos_ref, inv_ref, sgn_ref, cos_ref, sin_ref):
    ang = pos_ref[...].astype(F32) * inv_ref[...]
    cos_ref[...] = jnp.cos(ang)
    sin_ref[...] = jnp.sin(ang) * sgn_ref[...]


def _rope_tables(positions):
    t = positions.size
    inv = ROPE_THETA ** (-jnp.arange(0, HEAD_DIM, 2, dtype=F32) / HEAD_DIM)
    inv = jnp.tile(inv, LANES // (HEAD_DIM // 2))[None, :]
    sgn = jnp.tile(jnp.concatenate([-jnp.ones(HEAD_DIM // 2, F32), jnp.ones(HEAD_DIM // 2, F32)]),
                   LANES // HEAD_DIM)[None, :]
    tm = TOKEN_TILE
    tab = jax.ShapeDtypeStruct((t, LANES), F32)
    return pl.pallas_call(
        _rope_kernel,
        grid=(t // tm,),
        in_specs=[pl.BlockSpec((tm, LANES), lambda i: (i, 0)),
                  pl.BlockSpec((1, LANES), lambda i: (0, 0)),
                  pl.BlockSpec((1, LANES), lambda i: (0, 0))],
        out_specs=[pl.BlockSpec((tm, LANES), lambda i: (i, 0))] * 2,
        out_shape=[tab, tab],
        name="rope",
        compiler_params=_params(1),
    )(jnp.broadcast_to(positions.reshape(t, 1), (t, LANES)), inv, sgn)


def _ffn_kernel(x_ref, g_ref, wg_ref, wu_ref, wo_ref, o_ref):
    x = x_ref[...]
    h = _rms(x, g_ref[...]).astype(BF16)
    acc = None
    for lo, hi in FFN_CHUNKS:
        sl = slice(lo, hi)
        g = jnp.dot(h, wg_ref[:, sl], preferred_element_type=F32)
        u = jnp.dot(h, wu_ref[:, sl], preferred_element_type=F32)
        a = (g * _sigmoid(g) * u).astype(BF16)
        part = jnp.dot(a, wo_ref[sl, :], preferred_element_type=F32)
        acc = part if acc is None else acc + part
    o_ref[...] = x + 0.5 * acc


def _ffn(x, gain, w_in, w_out, layer):
    t = x.shape[0]
    tm = TOKEN_TILE
    return pl.pallas_call(
        _ffn_kernel,
        grid=(t // tm,),
        in_specs=[pl.BlockSpec((tm, D_MODEL), lambda i: (i, 0)),
                  pl.BlockSpec((None, 1, D_MODEL), lambda i: (layer, 0, 0)),
                  _resident((None, D_MODEL, D_FF), lambda i: (layer, 0, 0)),
                  _resident((None, D_MODEL, D_FF), lambda i: (layer, 0, 1)),
                  _resident((None, D_FF, D_MODEL), lambda i: (layer, 0, 0))],
        out_specs=pl.BlockSpec((tm, D_MODEL), lambda i: (i, 0)),
        out_shape=jax.ShapeDtypeStruct((t, D_MODEL), F32),
        name="ffn",
        compiler_params=_params(1),
    )(x, gain, w_in, w_in, w_out)


def _inproj_kernel(x_ref, gm_ref, cos_ref, sin_ref, w_ref, gains_ref, ones_ref,
                   qa1, ka1, va1, qa2, ka2, va2, qa3, ka3, va3, qb, kb, vb, qc, kc, vc, gl, cr,
                   hs_ref, hp_ref, tb_ref):
    tm = x_ref.shape[0]
    n_slab = D_MODEL // LANES
    h = _rms(x_ref[...], gm_ref[...])
    h_bf = h.astype(BF16)
    cos0 = cos_ref[...]
    sin0 = sin_ref[...]
    ones = ones_ref[...]
    gains = gains_ref[...]

    first_half = (lax.broadcasted_iota(jnp.int32, (1, LANES), 1) % HEAD_DIM) < (HEAD_DIM // 2)

    def norm_rope(y, gain, cos, sin):
        ss = jnp.dot((y * y).astype(BF16), ones, preferred_element_type=F32)
        yn = y * lax.rsqrt(ss * (1.0 / HEAD_DIM) + NORM_EPS) * gain
        partner = jnp.where(first_half, pltpu.roll(yn, LANES - HEAD_DIM // 2, 1), pltpu.roll(yn, HEAD_DIM // 2, 1))
        return yn * cos + partner * sin

    for s in range(n_slab):
        hs_ref[s] = h[:, s * LANES:(s + 1) * LANES]
    tb_ref[0] = cos0
    tb_ref[1] = sin0

    def class_major(d):
        if d == 1:
            return h_bf, cos0, sin0
        rows = tm // d
        for c in range(d):
            for s in range(n_slab):
                hp_ref[c * rows:(c + 1) * rows, s * LANES:(s + 1) * LANES] = (
                    hs_ref[s, pl.ds(c, rows, stride=d), :].astype(BF16))
        cos = jnp.concatenate([tb_ref[0, pl.ds(c, rows, stride=d), :] for c in range(d)], axis=0)
        sin = jnp.concatenate([tb_ref[1, pl.ds(c, rows, stride=d), :] for c in range(d)], axis=0)
        return hp_ref[...], cos, sin

    def store_classes(ref, col, val, d):
        rows = tm // d
        for c in range(d):
            ref[c, :, col:col + LANES] = val[c * rows:(c + 1) * rows, :]

    for gi, (qo, ko, vo) in enumerate(((qa1, ka1, va1), (qa2, ka2, va2), (qa3, ka3, va3))):
        d = A_CONFIGS[gi][1]
        hp, cos, sin = class_major(d)
        yq, yk, yv = [jnp.dot(hp, w_ref[:, pl.ds(OFF_A + part * A_SEG + gi * A_GROUP_WIDTH, A_GROUP_WIDTH)],
                              preferred_element_type=F32) for part in range(3)]
        for j in range(A_GROUP_WIDTH // LANES):
            c0 = j * LANES
            store_classes(qo, c0, (norm_rope(yq[:, c0:c0 + LANES], gains[0:1], cos, sin) * Q_SCALE).astype(BF16), d)
            store_classes(ko, c0, norm_rope(yk[:, c0:c0 + LANES], gains[1:2], cos, sin).astype(BF16), d)
            store_classes(vo, c0, yv[:, c0:c0 + LANES].astype(BF16), d)

    y = jnp.dot(h_bf, w_ref[:, OFF_B:OFF_B + B_SEG], preferred_element_type=F32)
    for j in range(4):
        c0 = j * LANES
        qb[0, :, c0:c0 + LANES] = (norm_rope(y[:, c0:c0 + LANES], gains[2:3], cos0, sin0) * Q_SCALE).astype(BF16)
    for j in range(2):
        c0 = j * LANES
        kb[0, :, c0:c0 + LANES] = norm_rope(y[:, 512 + c0:512 + c0 + LANES], gains[3:4], cos0, sin0).astype(BF16)
        vb[0, :, c0:c0 + LANES] = y[:, 768 + c0:768 + c0 + LANES].astype(BF16)

    y = jnp.dot(h_bf, w_ref[:, OFF_C:OFF_C + C_SEG], preferred_element_type=F32)
    qc[...] = y[:, 0:256].astype(BF16)
    kc[...] = y[:, 256:512].astype(BF16)
    vc[...] = y[:, 512:1024].astype(BF16)
    gl[...] = y[:, 1024:1152].astype(BF16)
    cr[...] = y[:, 1152:1664].astype(BF16)


def _inproj(x, gain, cos, sin, w_packed, head_gains, ones_bd, layer, batch, seq):
    t = x.shape[0]
    tm = TOKEN_TILE
    tiles_per_seq = seq // tm

    def cm_spec(d, width):
        return pl.BlockSpec((None, d, tm // d, width),
                            lambda i: (i // tiles_per_seq, 0, i % tiles_per_seq, 0))

    def cm_shape(d, width):
        return jax.ShapeDtypeStruct((batch, d, seq // d, width), BF16)

    def tok_spec(width):
        return pl.BlockSpec((tm, width), lambda i: (i, 0))

    out_specs, out_shape = [], []
    for _, d in A_CONFIGS:
        for _ in range(3):
            out_specs.append(cm_spec(d, A_GROUP_WIDTH))
            out_shape.append(cm_shape(d, A_GROUP_WIDTH))
    for width in (512, 256, 256):
        out_specs.append(cm_spec(1, width))
        out_shape.append(cm_shape(1, width))
    for width in (256, 256, 512, 128, 512):
        out_specs.append(tok_spec(width))
        out_shape.append(jax.ShapeDtypeStruct((t, width), BF16))

    return pl.pallas_call(
        _inproj_kernel,
        grid=(t // tm,),
        in_specs=[tok_spec(D_MODEL),
                  pl.BlockSpec((None, 1, D_MODEL), lambda i: (layer, 0, 0)),
                  tok_spec(LANES), tok_spec(LANES),
                  _resident((None, D_MODEL, W_PACKED), lambda i: (layer, 0, 0)),
                  pl.BlockSpec((None, 4, LANES), lambda i: (layer, 0, 0)),
                  pl.BlockSpec((LANES, LANES), lambda i: (0, 0))],
        out_specs=out_specs,
        out_shape=out_shape,
        scratch_shapes=[pltpu.VMEM((D_MODEL // LANES, tm, LANES), F32),
                        pltpu.VMEM((tm, D_MODEL), BF16),
                        pltpu.VMEM((2, tm, LANES), F32)],
        name="inproj",
        compiler_params=_params(1),
    )(x, gain, cos, sin, w_packed, head_gains, ones_bd)


def _attn_kernel(d, nb, window, kv_div, use_sink, *refs):
    if use_sink:
        sink_ref, q_ref, kc_ref, kp_ref, vc_ref, vp_ref, o_ref, lse_ref, osc, lsc, s_scr, p_scr, st_scr = refs
    else:
        q_ref, kc_ref, kp_ref, vc_ref, vp_ref, o_ref, lse_ref, osc, lsc, s_scr, p_scr, st_scr = refs
    chunk = pl.program_id(1)
    lane = lax.broadcasted_iota(jnp.int32, (1, LANES), 1)
    head_of_lane = lane // HEAD_DIM
    key = lax.broadcasted_iota(jnp.int32, (2 * BLOCK, 2 * BLOCK), 0)
    col = lax.broadcasted_iota(jnp.int32, (2 * BLOCK, 2 * BLOCK), 1)
    dist = (col % BLOCK) + BLOCK - key
    band = (dist >= 0) & (dist <= window)
    col_row = lax.broadcasted_iota(jnp.int32, (1, 2 * BLOCK), 1)
    sub8 = lax.broadcasted_iota(jnp.int32, (8, LANES), 0)
    nt = (((1,), (1,)), ((), ()))
    tn = (((0,), (0,)), ((), ()))

    n_units = d * nb

    def locate(u):
        c = u // nb if d > 1 else 0
        j = u % nb if nb > 1 else 0
        if isinstance(j, int):
            return c, j, j * BLOCK, max(j - 1, 0) * BLOCK
        return c, j, pl.multiple_of(j * BLOCK, BLOCK), pl.multiple_of(jnp.maximum(j - 1, 0) * BLOCK, BLOCK)

    def window_rows(cur_ref, prev_ref, u, p, in_loop=False):
        c, j, row0, prow = locate(u)
        kcol = (p // kv_div) * LANES
        ksl = slice(kcol, kcol + LANES)
        cur = cur_ref[c, pl.ds(row0, BLOCK), ksl]
        if isinstance(j, int):
            first = j == 0
        elif d == 1 and in_loop:
            first = False
        else:
            first = None
        if first is None:
            prev = jnp.where(j == 0, prev_ref[c, :, ksl], cur_ref[c, pl.ds(prow, BLOCK), ksl])
        else:
            prev = prev_ref[c, :, ksl] if first else cur_ref[c, pl.ds(prow, BLOCK), ksl]
        return jnp.concatenate([prev, cur], axis=0)

    def scores(u, in_loop=False):
        c, j, row0, _ = locate(u)
        seq_start = jnp.logical_and(chunk == 0, j == 0)
        valid = band & jnp.logical_or(key >= BLOCK, jnp.logical_not(seq_start))
        for p in range(4):
            qp = q_ref[c, pl.ds(row0, BLOCK), p * LANES:(p + 1) * LANES]
            q2 = jnp.concatenate([jnp.where(head_of_lane == e, qp, jnp.zeros_like(qp)) for e in range(2)], axis=0)
            s = lax.dot_general(window_rows(kc_ref, kp_ref, u, p, in_loop), q2, nt,
                                preferred_element_type=F32)
            s = jnp.where(valid, s, -jnp.inf)
            s_scr[p] = s
            m = jnp.max(s, axis=0, keepdims=True)
            if use_sink:
                m = jnp.maximum(m, sink_row(p))
            st_scr[p, 0:1, :] = m

    def sink_row(p):
        return jnp.where(col_row < BLOCK, sink_ref[2 * p], sink_ref[2 * p + 1]) * LOG2E

    ones8 = jnp.ones((8, 2 * BLOCK), BF16)

    def softmax(u):
        for p in range(4):
            m = st_scr[p, 0:1, :]
            p_scr[p] = jnp.exp2((s_scr[p] - m).astype(BF16))
            st_scr[p, 2:3, :] = m

    def outputs(u, in_loop=False):
        c, _, row0, _ = locate(u)
        start = row0 * d + c
        rows = pl.ds(start, BLOCK, stride=d) if d > 1 else pl.ds(start, BLOCK)
        lse8 = jnp.zeros((8, LANES), F32)
        for p in range(4):
            pr, m = p_scr[p], st_scr[p, 2:3, :]
            ob = lax.dot_general(window_rows(vc_ref, vp_ref, u, p, in_loop), pr, tn,
                                 preferred_element_type=F32)
            den = jnp.dot(ones8, pr, preferred_element_type=F32)[0:1, :]
            if use_sink:
                den = den + jnp.exp2(sink_row(p) - m)
            inv = 1.0 / den
            o_t = jnp.concatenate([ob[0:HEAD_DIM, 0:BLOCK] * inv[:, 0:BLOCK],
                                   ob[HEAD_DIM:LANES, BLOCK:2 * BLOCK] * inv[:, BLOCK:2 * BLOCK]], axis=0)
            if d > 1:
                osc[p, rows, :] = o_t.T
            else:
                o_ref[rows, p * LANES:(p + 1) * LANES] = o_t.T.astype(BF16)
            l2 = m + jnp.log2(den)
            lse8 = jnp.where(sub8 == 2 * p, l2[:, 0:BLOCK], lse8)
            lse8 = jnp.where(sub8 == 2 * p + 1, l2[:, BLOCK:2 * BLOCK], lse8)
        lse_t = jnp.concatenate([lse8, jnp.zeros((LANES - 8, LANES), F32)], axis=0)
        if d > 1:
            lsc[rows, :] = lse_t.T
        else:
            lse_ref[rows, :] = lse_t.T

    scores(0)
    softmax(0)
    scores(1)
    outputs(0)
    softmax(1)
    scores(2)

    def body(u, carry):
        outputs(u - 2, True)
        softmax(u - 1)
        scores(u, True)
        return carry

    lax.fori_loop(3, n_units, body, 0)
    outputs(n_units - 2)
    softmax(n_units - 1)
    outputs(n_units - 1)
    if d > 1:
        for p in range(4):
            o_ref[:, p * LANES:(p + 1) * LANES] = osc[p].astype(BF16)
        lse_ref[...] = lsc[...]


def _attention(q, k, v, sinks, *, window, kv_div, batch, seq):
    d = q.shape[1]
    nb = ATTN_CHUNK // (d * BLOCK)
    kw = k.shape[-1]
    rows = nb * BLOCK
    cur = lambda b, i: (b, 0, i, 0)
    prev = lambda b, i: (b, 0, jnp.maximum(i * nb - 1, 0), 0)
    in_specs = [pl.BlockSpec((None, d, rows, A_GROUP_WIDTH), cur),
                pl.BlockSpec((None, d, rows, kw), cur),
                pl.BlockSpec((None, d, BLOCK, kw), prev),
                pl.BlockSpec((None, d, rows, kw), cur),
                pl.BlockSpec((None, d, BLOCK, kw), prev)]
    args = [q, k, k, v, v]
    use_sink = sinks is not None
    if use_sink:
        in_specs = [pl.BlockSpec(memory_space=pltpu.SMEM)] + in_specs
        args = [sinks] + args
    return pl.pallas_call(
        functools.partial(_attn_kernel, d, nb, window, kv_div, use_sink),
        grid=(batch, seq // ATTN_CHUNK),
        in_specs=in_specs,
        out_specs=[pl.BlockSpec((None, ATTN_CHUNK, A_GROUP_WIDTH), lambda b, i: (b, i, 0)),
                   pl.BlockSpec((None, ATTN_CHUNK, LANES), lambda b, i: (b, i, 0))],
        out_shape=[jax.ShapeDtypeStruct((batch, seq, A_GROUP_WIDTH), BF16),
                   jax.ShapeDtypeStruct((batch, seq, LANES), F32)],
        scratch_shapes=[pltpu.VMEM((4, ATTN_CHUNK if d > 1 else 8, LANES), F32),
                        pltpu.VMEM((ATTN_CHUNK if d > 1 else 8, LANES), F32),
                        pltpu.VMEM((4, 2 * BLOCK, 2 * BLOCK), F32),
                        pltpu.VMEM((4, 2 * BLOCK, 2 * BLOCK), BF16),
                        pltpu.VMEM((4, 8, 2 * BLOCK), F32)],
        name=f"attn_d{d}_w{window}",
        compiler_params=_params(2),
    )(*args)


def _gla_kernel(q_ref, k_ref, v_ref, gl_ref, cr_ref, gu_ref, gb_ref, on_ref, tri_ref, y_ref, st_ref):
    tm = q_ref.shape[0]
    width = C_HEADS * C_DK
    n_chunk = GLA_SUB // C_CHUNK

    @pl.when(pl.program_id(1) == 0)
    def _():
        st_ref[...] = jnp.zeros_like(st_ref)

    tri = tri_ref[...]
    keep = tri > 0
    on_gain = on_ref[...]
    head_of_lane = lax.broadcasted_iota(jnp.int32, (1, LANES), 1) // C_DK
    nt = (((1,), (1,)), ((), ()))
    tn = (((0,), (0,)), ((), ()))

    def chunked(a):
        return a.reshape(n_chunk, C_CHUNK, width)

    for sub in range(tm // GLA_SUB):
        rows = slice(sub * GLA_SUB, (sub + 1) * GLA_SUB)
        x = jnp.dot(gl_ref[rows, :], gu_ref[...], preferred_element_type=F32) + gb_ref[...]
        lg = (jnp.minimum(x, 0.0) - jnp.log(1.0 + jnp.exp(-jnp.abs(x)))) * (1.0 / C_TAU)
        g1 = lg.astype(BF16)
        g2 = (lg - g1.astype(F32)).astype(BF16)
        bs = jnp.dot(tri, jnp.concatenate([g1, g2], axis=1), preferred_element_type=F32)
        b = chunked(bs[:, 0:width] + bs[:, width:2 * width])
        b_mid = b[:, C_CHUNK // 2 - 1:C_CHUNK // 2, :]
        b_last = b[:, C_CHUNK - 1:C_CHUNK, :]
        q = chunked(q_ref[rows, :].astype(F32) * (C_DK ** -0.5))
        k = chunked(k_ref[rows, :].astype(F32))
        v = v_ref[rows, :]
        qd = (q * jnp.exp(b)).astype(BF16)
        qs = (q * jnp.exp(b - b_mid)).astype(BF16)
        ks = (k * jnp.exp(b_mid - b)).astype(BF16)
        kd = (k * jnp.exp(b_last - b)).astype(BF16)
        decay = jnp.exp(b_last)
        for hh in range(C_HEADS):
            cols = slice(hh * LANES, (hh + 1) * LANES)
            kcols = slice((hh // 2) * LANES, (hh // 2 + 1) * LANES)
            mine = head_of_lane == (hh % 2)
            qs_h = jnp.where(mine, qs[:, :, kcols], jnp.zeros((), BF16))
            qd_h = jnp.where(mine, qd[:, :, kcols], jnp.zeros((), BF16))
            att = lax.dot_general(qs_h.reshape(GLA_SUB, LANES), ks[:, :, kcols].reshape(GLA_SUB, LANES),
                                  nt, preferred_element_type=F32)
            att = jnp.where(keep, att, 0.0).astype(BF16)
            o = jnp.dot(att, v[:, cols], preferred_element_type=F32)
            st = st_ref[hh]
            inter = []
            for c in range(n_chunk):
                inter.append(lax.dot_general(qd_h[c], st.astype(BF16), nt, preferred_element_type=F32))
                inc = lax.dot_general(v[c * C_CHUNK:(c + 1) * C_CHUNK, cols], kd[c, :, kcols], tn,
                                      preferred_element_type=F32)
                st = st * decay[c, :, kcols] + inc
            st_ref[hh] = st
            o = o + jnp.concatenate(inter, axis=0)
            gate = cr_ref[rows, cols].astype(F32)
            y_ref[rows, cols] = (_rms(o, on_gain) * (gate * _sigmoid(gate))).astype(BF16)


def _gla(q, k, v, gl, cr, gate_up, gate_bias, out_norm, tri, layer, batch, seq):
    t = q.shape[0]
    tm = TOKEN_TILE
    per_seq = seq // tm
    width = C_HEADS * C_DV
    kwidth = C_HEADS * C_DK
    tok = lambda w: pl.BlockSpec((tm, w), lambda b, i: (b * per_seq + i, 0))
    return pl.pallas_call(
        _gla_kernel,
        grid=(batch, per_seq),
        in_specs=[tok(kwidth), tok(kwidth), tok(width), tok(LANES), tok(width),
                  pl.BlockSpec((None, LANES, kwidth), lambda b, i: (layer, 0, 0)),
                  pl.BlockSpec((None, 1, kwidth), lambda b, i: (layer, 0, 0)),
                  pl.BlockSpec((None, 1, C_DV), lambda b, i: (layer, 0, 0)),
                  pl.BlockSpec((GLA_SUB, GLA_SUB), lambda b, i: (0, 0))],
        out_specs=tok(width),
        out_shape=jax.ShapeDtypeStruct((t, width), BF16),
        scratch_shapes=[pltpu.VMEM((C_HEADS, C_DV, LANES), F32)],
        name="gla",
        compiler_params=_params(2),
    )(q, k, v, gl, cr, gate_up, gate_bias, out_norm, tri)


def _merge_kernel(x_ref, gm_ref, o1, o2, o3, l1, l2, l3, yb_ref, yc_ref, wg_ref, wb_ref, wo_ref, ex_ref, out_ref):
    x = x_ref[...]
    h = _rms(x, gm_ref[...]).astype(BF16)
    la, lb, lc = l1[...], l2[...], l3[...]
    top = jnp.maximum(jnp.maximum(la, lb), lc)
    ea, eb, ec = jnp.exp2(la - top), jnp.exp2(lb - top), jnp.exp2(lc - top)
    inv = 1.0 / (ea + eb + ec)
    expand = ex_ref[...]
    low_lanes = lax.broadcasted_iota(jnp.int32, (1, LANES), 1) < LANES // 2
    ya = None
    for e, o in ((ea, o1), (eb, o2), (ec, o3)):
        w = e * inv
        w_hi = w.astype(BF16)
        w_lo = pltpu.roll(w - w_hi.astype(F32), LANES // 2, 1).astype(BF16)
        wx = jnp.dot(jnp.where(low_lanes, w_hi, w_lo), expand, preferred_element_type=F32)
        term = wx * o[...].astype(F32)
        ya = term if ya is None else ya + term
    merged = None
    for i, y in enumerate((ya.astype(BF16), yb_ref[...], yc_ref[...])):
        gate = _sigmoid(jnp.dot(h, wg_ref[:, i * D_MODEL:(i + 1) * D_MODEL], preferred_element_type=F32))
        term = gate * jnp.dot(y, wb_ref[i], preferred_element_type=F32)
        merged = term if merged is None else merged + term
    out_ref[...] = x + jnp.dot(merged.astype(BF16), wo_ref[...], preferred_element_type=F32)


def _merge(x, gain, oa, la, yb, yc, w_gate, w_branch, w_out, expand, layer):
    t = x.shape[0]
    tm = TOKEN_TILE
    tok = lambda w: pl.BlockSpec((tm, w), lambda i: (i, 0))
    bw = A_GROUP_WIDTH
    return pl.pallas_call(
        _merge_kernel,
        grid=(t // tm,),
        in_specs=[tok(D_MODEL),
                  pl.BlockSpec((None, 1, D_MODEL), lambda i: (layer, 0, 0)),
                  tok(bw), tok(bw), tok(bw), tok(LANES), tok(LANES), tok(LANES), tok(bw), tok(bw),
                  _resident((None, D_MODEL, N_BRANCH * D_MODEL), lambda i: (layer, 0, 0)),
                  _resident((None, N_BRANCH, bw, D_MODEL), lambda i: (layer, 0, 0, 0)),
                  _resident((None, D_MODEL, D_MODEL), lambda i: (layer, 0, 0)),
                  pl.BlockSpec((LANES, bw), lambda i: (0, 0))],
        out_specs=tok(D_MODEL),
        out_shape=jax.ShapeDtypeStruct((t, D_MODEL), F32),
        name="merge",
        compiler_params=_params(1),
    )(x, gain, *oa, *la, yb, yc, w_gate, w_branch, w_out, expand)


def _pack_w_in(w_in):
    n_layer = w_in.shape[0]
    lead = w_in.shape[:2]
    edges = np.cumsum([0, 4608, 512, 256, 256, 256, 512, C_GATE_RANK, 512, N_BRANCH * D_MODEL])
    seg = [w_in[:, :, edges[i]:edges[i + 1]] for i in range(9)]
    a, b_q, b_kv, c_q, c_k, c_v, c_glow, c_r, gates = seg
    b_kv = b_kv.reshape(lead + (2, 2, 1, HEAD_DIM))
    b_kv = jnp.broadcast_to(b_kv, lead + (2, 2, 2, HEAD_DIM)).reshape(lead + (512,))
    c_glow = jnp.pad(c_glow, ((0, 0), (0, 0), (0, LANES - C_GATE_RANK)))
    packed = jnp.concatenate([a, b_q, b_kv, c_q, c_k, c_v, c_glow, c_r], axis=-1)
    assert packed.shape == (n_layer, D_MODEL, W_PACKED)
    return packed.astype(BF16), gates.astype(BF16)


def kernel(x, positions, norm_ffn1, w_ffn1_in, w_ffn1_out, norm_mix, w_in, a_q_norm, a_k_norm, b_q_norm, b_k_norm,
           b_sinks, c_gate_up, c_gate_bias, c_out_norm, w_branch, w_out, norm_ffn2, w_ffn2_in, w_ffn2_out):
    batch, seq, _ = x.shape
    n_layer = w_in.shape[0]
    assert seq % ATTN_CHUNK == 0 and x.shape[2] == D_MODEL

    w_packed, w_gate = _pack_w_in(w_in)
    w1_in, w1_out = w_ffn1_in.astype(BF16), w_ffn1_out.astype(BF16)
    w2_in, w2_out = w_ffn2_in.astype(BF16), w_ffn2_out.astype(BF16)
    wb, wo = w_branch.astype(BF16), w_out.astype(BF16)
    row = lambda g: g[:, None, :]
    head_gains = jnp.stack([jnp.tile(g, (1, LANES // HEAD_DIM)) for g in (a_q_norm, a_k_norm, b_q_norm, b_k_norm)],
                           axis=1)
    gate_up = jnp.pad(c_gate_up, ((0, 0), (0, LANES - C_GATE_RANK), (0, 0))).astype(BF16)
    gate_bias = row(c_gate_bias)

    lane_head = np.arange(LANES) // HEAD_DIM
    ones_bd = jnp.asarray(lane_head[:, None] == lane_head[None, :], BF16)
    expand = jnp.asarray((np.arange(LANES) % (LANES // 2))[:, None] == (np.arange(A_GROUP_WIDTH) // HEAD_DIM)[None, :],
                         BF16)
    idx = np.arange(GLA_SUB)
    tri = jnp.asarray((idx[:, None] >= idx[None, :]) & (idx[:, None] // C_CHUNK == idx[None, :] // C_CHUNK), BF16)

    cos, sin = _rope_tables(positions)
    xt = x.reshape(batch * seq, D_MODEL)
    for l in range(n_layer):
        xt = _ffn(xt, row(norm_ffn1), w1_in, w1_out, l)
        (qa1, ka1, va1, qa2, ka2, va2, qa3, ka3, va3, qb, kb, vb, qc, kc, vc, gl, cr) = _inproj(
            xt, row(norm_mix), cos, sin, w_packed, head_gains, ones_bd, l, batch, seq)
        oa, la = [], []
        for (window, d), (q, k, v) in zip(A_CONFIGS, ((qa1, ka1, va1), (qa2, ka2, va2), (qa3, ka3, va3))):
            o, lse = _attention(q, k, v, None, window=window // d, kv_div=1, batch=batch, seq=seq)
            oa.append(o.reshape(batch * seq, A_GROUP_WIDTH))
            la.append(lse.reshape(batch * seq, LANES))
        yb, _ = _attention(qb, kb, vb, b_sinks[l], window=B_WINDOW - 1, kv_div=2, batch=batch, seq=seq)
        yb = yb.reshape(batch * seq, A_GROUP_WIDTH)
        yc = _gla(qc, kc, vc, gl, cr, gate_up, gate_bias, row(c_out_norm), tri, l, batch, seq)
        xt = _merge(xt, row(norm_mix), oa, la, yb, yc, w_gate, wb, wo, expand, l)
        xt = _ffn(xt, row(norm_ffn2), w2_in, w2_out, l)
    return xt.reshape(batch, seq, D_MODEL)
```

```python
import functools

import numpy as np
import jax
import jax.numpy as jnp
from jax import lax
from jax.experimental import pallas as pl
from jax.experimental.pallas import tpu as pltpu

F32 = jnp.float32
BF16 = jnp.bfloat16

D_MODEL = 1024
D_FF = 2816
HEAD_DIM = 64
NORM_EPS = 1e-6
ROPE_THETA = 10000.0
LOG2E = float(np.log2(np.e))
Q_SCALE = HEAD_DIM ** -0.5 * LOG2E
LANES = 128
BLOCK = 128
A_CONFIGS = ((128, 1), (512, 4), (2048, 16))
A_GROUP_WIDTH = 512
B_WINDOW = 128
C_HEADS = 4
C_DK = 64
C_DV = 128
C_GATE_RANK = 16
C_TAU = 16.0
C_CHUNK = 64
GLA_SUB = 256
N_BRANCH = 3

A_SEG = 3 * A_GROUP_WIDTH
OFF_A = 0
OFF_B = 3 * A_SEG
B_SEG = 1024
OFF_C = OFF_B + B_SEG
C_SEG = 1664
W_PACKED = OFF_C + C_SEG

TOKEN_TILE = 512
ATTN_CHUNK = 2048
MXU_COLS = 256
FFN_CHUNKS = ((0, 1536), (1536, D_FF))
VMEM_LIMIT = 56 * 2**20


def _params(n_axes):
    return pltpu.CompilerParams(dimension_semantics=("arbitrary",) * n_axes, vmem_limit_bytes=VMEM_LIMIT)


def _rms(x, gain):
    return x * lax.rsqrt(jnp.mean(x * x, axis=-1, keepdims=True) + NORM_EPS) * gain


def _sigmoid(x):
    return 0.5 * jnp.tanh(0.5 * x) + 0.5


def _resident(block, index_map):
    return pl.BlockSpec(block, index_map, pipeline_mode=pl.Buffered(1))


def _rope_kernel(pos_ref, inv_ref, sgn_ref, cos_ref, sin_ref):
    ang = pos_ref[...].astype(F32) * inv_ref[...]
    cos_ref[...] = jnp.cos(ang)
    sin_ref[...] = jnp.sin(ang) * sgn_ref[...]


def _rope_tables(positions):
    t = positions.size
    inv = ROPE_THETA ** (-jnp.arange(0, HEAD_DIM, 2, dtype=F32) / HEAD_DIM)
    inv = jnp.tile(inv, LANES // (HEAD_DIM // 2))[None, :]
    sgn = jnp.tile(jnp.concatenate([-jnp.ones(HEAD_DIM // 2, F32), jnp.ones(HEAD_DIM // 2, F32)]),
                   LANES // HEAD_DIM)[None, :]
    tm = TOKEN_TILE
    tab = jax.ShapeDtypeStruct((t, LANES), F32)
    return pl.pallas_call(
        _rope_kernel,
        grid=(t // tm,),
        in_specs=[pl.BlockSpec((tm, LANES), lambda i: (i, 0)),
                  pl.BlockSpec((1, LANES), lambda i: (0, 0)),
                  pl.BlockSpec((1, LANES), lambda i: (0, 0))],
        out_specs=[pl.BlockSpec((tm, LANES), lambda i: (i, 0))] * 2,
        out_shape=[tab, tab],
        name="rope",
        compiler_params=_params(1),
    )(jnp.broadcast_to(positions.reshape(t, 1), (t, LANES)), inv, sgn)


def _ffn_kernel(x_ref, g_ref, wg_ref, wu_ref, wo_ref, o_ref):
    x = x_ref[...]
    h = _rms(x, g_ref[...]).astype(BF16)
    acc = None
    for lo, hi in FFN_CHUNKS:
        sl = slice(lo, hi)
        g = jnp.dot(h, wg_ref[:, sl], preferred_element_type=F32)
        u = jnp.dot(h, wu_ref[:, sl], preferred_element_type=F32)
        a = (g * _sigmoid(g) * u).astype(BF16)
        part = jnp.dot(a, wo_ref[sl, :], preferred_element_type=F32)
        acc = part if acc is None else acc + part
    o_ref[...] = x + 0.5 * acc


def _ffn(x, gain, w_in, w_out, layer):
    t = x.shape[0]
    tm = TOKEN_TILE
    return pl.pallas_call(
        _ffn_kernel,
        grid=(t // tm,),
        in_specs=[pl.BlockSpec((tm, D_MODEL), lambda i: (i, 0)),
                  pl.BlockSpec((None, 1, D_MODEL), lambda i: (layer, 0, 0)),
                  _resident((None, D_MODEL, D_FF), lambda i: (layer, 0, 0)),
                  _resident((None, D_MODEL, D_FF), lambda i: (layer, 0, 1)),
                  _resident((None, D_FF, D_MODEL), lambda i: (layer, 0, 0))],
        out_specs=pl.BlockSpec((tm, D_MODEL), lambda i: (i, 0)),
        out_shape=jax.ShapeDtypeStruct((t, D_MODEL), F32),
        name="ffn",
        compiler_params=_params(1),
    )(x, gain, w_in, w_in, w_out)


def _inproj_kernel(x_ref, gm_ref, cos_ref, sin_ref, w_ref, gains_ref, ones_ref,
                   qa1, ka1, va1, qa2, ka2, va2, qa3, ka3, va3, qb, kb, vb, qc, kc, vc, gl, cr,
                   hs_ref, hp_ref, tb_ref):
    tm = x_ref.shape[0]
    n_slab = D_MODEL // LANES
    h = _rms(x_ref[...], gm_ref[...])
    h_bf = h.astype(BF16)
    cos0 = cos_ref[...]
    sin0 = sin_ref[...]
    ones = ones_ref[...]
    gains = gains_ref[...]

    first_half = (lax.broadcasted_iota(jnp.int32, (1, LANES), 1) % HEAD_DIM) < (HEAD_DIM // 2)

    def norm_rope(y, gain, cos, sin):
        ss = jnp.dot((y * y).astype(BF16), ones, preferred_element_type=F32)
        yn = y * lax.rsqrt(ss * (1.0 / HEAD_DIM) + NORM_EPS) * gain
        partner = jnp.where(first_half, pltpu.roll(yn, LANES - HEAD_DIM // 2, 1), pltpu.roll(yn, HEAD_DIM // 2, 1))
        return yn * cos + partner * sin

    for s in range(n_slab):
        hs_ref[s] = h[:, s * LANES:(s + 1) * LANES]
    tb_ref[0] = cos0
    tb_ref[1] = sin0

    def class_major(d):
        if d == 1:
            return h_bf, cos0, sin0
        rows = tm // d
        for c in range(d):
            for s in range(n_slab):
                hp_ref[c * rows:(c + 1) * rows, s * LANES:(s + 1) * LANES] = (
                    hs_ref[s, pl.ds(c, rows, stride=d), :].astype(BF16))
        cos = jnp.concatenate([tb_ref[0, pl.ds(c, rows, stride=d), :] for c in range(d)], axis=0)
        sin = jnp.concatenate([tb_ref[1, pl.ds(c, rows, stride=d), :] for c in range(d)], axis=0)
        return hp_ref[...], cos, sin

    def store_classes(ref, col, val, d):
        rows = tm // d
        for c in range(d):
            ref[c, :, col:col + LANES] = val[c * rows:(c + 1) * rows, :]

    for gi, (qo, ko, vo) in enumerate(((qa1, ka1, va1), (qa2, ka2, va2), (qa3, ka3, va3))):
        d = A_CONFIGS[gi][1]
        hp, cos, sin = class_major(d)
        yq, yk, yv = [jnp.dot(hp, w_ref[:, pl.ds(OFF_A + part * A_SEG + gi * A_GROUP_WIDTH, A_GROUP_WIDTH)],
                              preferred_element_type=F32) for part in range(3)]
        for j in range(A_GROUP_WIDTH // LANES):
            c0 = j * LANES
            store_classes(qo, c0, (norm_rope(yq[:, c0:c0 + LANES], gains[0:1], cos, sin) * Q_SCALE).astype(BF16), d)
            store_classes(ko, c0, norm_rope(yk[:, c0:c0 + LANES], gains[1:2], cos, sin).astype(BF16), d)
            store_classes(vo, c0, yv[:, c0:c0 + LANES].astype(BF16), d)

    y = jnp.dot(h_bf, w_ref[:, OFF_B:OFF_B + B_SEG], preferred_element_type=F32)
    for j in range(4):
        c0 = j * LANES
        qb[0, :, c0:c0 + LANES] = (norm_rope(y[:, c0:c0 + LANES], gains[2:3], cos0, sin0) * Q_SCALE).astype(BF16)
    for j in range(2):
        c0 = j * LANES
        kb[0, :, c0:c0 + LANES] = norm_rope(y[:, 512 + c0:512 + c0 + LANES], gains[3:4], cos0, sin0).astype(BF16)
        vb[0, :, c0:c0 + LANES] = y[:, 768 + c0:768 + c0 + LANES].astype(BF16)

    y = jnp.dot(h_bf, w_ref[:, OFF_C:OFF_C + C_SEG], preferred_element_type=F32)
    qc[...] = y[:, 0:256].astype(BF16)
    kc[...] = y[:, 256:512].astype(BF16)
    vc[...] = y[:, 512:1024].astype(BF16)
    gl[...] = y[:, 1024:1152].astype(BF16)
    cr[...] = y[:, 1152:1664].astype(BF16)


def _inproj(x, gain, cos, sin, w_packed, head_gains, ones_bd, layer, batch, seq):
    t = x.shape[0]
    tm = TOKEN_TILE
    tiles_per_seq = seq // tm

    def cm_spec(d, width):
        return pl.BlockSpec((None, d, tm // d, width),
                            lambda i: (i // tiles_per_seq, 0, i % tiles_per_seq, 0))

    def cm_shape(d, width):
        return jax.ShapeDtypeStruct((batch, d, seq // d, width), BF16)

    def tok_spec(width):
        return pl.BlockSpec((tm, width), lambda i: (i, 0))

    out_specs, out_shape = [], []
    for _, d in A_CONFIGS:
        for _ in range(3):
            out_specs.append(cm_spec(d, A_GROUP_WIDTH))
            out_shape.append(cm_shape(d, A_GROUP_WIDTH))
    for width in (512, 256, 256):
        out_specs.append(cm_spec(1, width))
        out_shape.append(cm_shape(1, width))
    for width in (256, 256, 512, 128, 512):
        out_specs.append(tok_spec(width))
        out_shape.append(jax.ShapeDtypeStruct((t, width), BF16))

    return pl.pallas_call(
        _inproj_kernel,
        grid=(t // tm,),
        in_specs=[tok_spec(D_MODEL),
                  pl.BlockSpec((None, 1, D_MODEL), lambda i: (layer, 0, 0)),
                  tok_spec(LANES), tok_spec(LANES),
                  _resident((None, D_MODEL, W_PACKED), lambda i: (layer, 0, 0)),
                  pl.BlockSpec((None, 4, LANES), lambda i: (layer, 0, 0)),
                  pl.BlockSpec((LANES, LANES), lambda i: (0, 0))],
        out_specs=out_specs,
        out_shape=out_shape,
        scratch_shapes=[pltpu.VMEM((D_MODEL // LANES, tm, LANES), F32),
                        pltpu.VMEM((tm, D_MODEL), BF16),
                        pltpu.VMEM((2, tm, LANES), F32)],
        name="inproj",
        compiler_params=_params(1),
    )(x, gain, cos, sin, w_packed, head_gains, ones_bd)


def _attn_kernel(d, nb, window, kv_div, use_sink, *refs):
    if use_sink:
        sink_ref, q_ref, kc_ref, kp_ref, vc_ref, vp_ref, o_ref, lse_ref, osc, lsc, s_scr, p_scr, st_scr = refs
    else:
        q_ref, kc_ref, kp_ref, vc_ref, vp_ref, o_ref, lse_ref, osc, lsc, s_scr, p_scr, st_scr = refs
    chunk = pl.program_id(1)
    lane = lax.broadcasted_iota(jnp.int32, (1, LANES), 1)
    head_of_lane = lane // HEAD_DIM
    key = lax.broadcasted_iota(jnp.int32, (2 * BLOCK, 2 * BLOCK), 0)
    col = lax.broadcasted_iota(jnp.int32, (2 * BLOCK, 2 * BLOCK), 1)
    dist = (col % BLOCK) + BLOCK - key
    band = (dist >= 0) & (dist <= window)
    col_row = lax.broadcasted_iota(jnp.int32, (1, 2 * BLOCK), 1)
    sub8 = lax.broadcasted_iota(jnp.int32, (8, LANES), 0)
    nt = (((1,), (1,)), ((), ()))
    tn = (((0,), (0,)), ((), ()))

    n_units = d * nb

    def locate(u):
        c = u // nb if d > 1 else 0
        j = u % nb if nb > 1 else 0
        if isinstance(j, int):
            return c, j, j * BLOCK, max(j - 1, 0) * BLOCK
        return c, j, pl.multiple_of(j * BLOCK, BLOCK), pl.multiple_of(jnp.maximum(j - 1, 0) * BLOCK, BLOCK)

    def window_rows(cur_ref, prev_ref, u, p, in_loop=False):
        c, j, row0, prow = locate(u)
        kcol = (p // kv_div) * LANES
        ksl = slice(kcol, kcol + LANES)
        cur = cur_ref[c, pl.ds(row0, BLOCK), ksl]
        if isinstance(j, int):
            first = j == 0
        elif d == 1 and in_loop:
            first = False
        else:
            first = None
        if first is None:
            prev = jnp.where(j == 0, prev_ref[c, :, ksl], cur_ref[c, pl.ds(prow, BLOCK), ksl])
        else:
            prev = prev_ref[c, :, ksl] if first else cur_ref[c, pl.ds(prow, BLOCK), ksl]
        return jnp.concatenate([prev, cur], axis=0)

    def scores(u, in_loop=False):
        c, j, row0, _ = locate(u)
        seq_start = jnp.logical_and(chunk == 0, j == 0)
        valid = band & jnp.logical_or(key >= BLOCK, jnp.logical_not(seq_start))
        for p in range(4):
            qp = q_ref[c, pl.ds(row0, BLOCK), p * LANES:(p + 1) * LANES]
            q2 = jnp.concatenate([jnp.where(head_of_lane == e, qp, jnp.zeros_like(qp)) for e in range(2)], axis=0)
            s = lax.dot_general(window_rows(kc_ref, kp_ref, u, p, in_loop), q2, nt,
                                preferred_element_type=F32)
            s = jnp.where(valid, s, -jnp.inf)
            s_scr[p] = s
            m = jnp.max(s, axis=0, keepdims=True)
            if use_sink:
                m = jnp.maximum(m, sink_row(p))
            st_scr[p, 0:1, :] = m

    def sink_row(p):
        return jnp.where(col_row < BLOCK, sink_ref[2 * p], sink_ref[2 * p + 1]) * LOG2E

    ones8 = jnp.ones((8, 2 * BLOCK), BF16)

    def softmax(u):
        for p in range(4):
            m = st_scr[p, 0:1, :]
            p_scr[p] = jnp.exp2(s_scr[p] - m).astype(BF16)
            st_scr[p, 2:3, :] = m

    def outputs(u, in_loop=False):
        c, _, row0, _ = locate(u)
        start = row0 * d + c
        rows = pl.ds(start, BLOCK, stride=d) if d > 1 else pl.ds(start, BLOCK)
        lse8 = jnp.zeros((8, LANES), F32)
        for p in range(4):
            pr, m = p_scr[p], st_scr[p, 2:3, :]
            ob = lax.dot_general(window_rows(vc_ref, vp_ref, u, p, in_loop), pr, tn,
                                 preferred_element_type=F32)
            den = jnp.dot(ones8, pr, preferred_element_type=F32)[0:1, :]
            if use_sink:
                den = den + jnp.exp2(sink_row(p) - m)
            inv = 1.0 / den
            o_t = jnp.concatenate([ob[0:HEAD_DIM, 0:BLOCK] * inv[:, 0:BLOCK],
                                   ob[HEAD_DIM:LANES, BLOCK:2 * BLOCK] * inv[:, BLOCK:2 * BLOCK]], axis=0)
            if d > 1:
                osc[p, rows, :] = o_t.T
            else:
                o_ref[rows, p * LANES:(p + 1) * LANES] = o_t.T.astype(BF16)
            l2 = m + jnp.log2(den)
            lse8 = jnp.where(sub8 == 2 * p, l2[:, 0:BLOCK], lse8)
            lse8 = jnp.where(sub8 == 2 * p + 1, l2[:, BLOCK:2 * BLOCK], lse8)
        lse_t = jnp.concatenate([lse8, jnp.zeros((LANES - 8, LANES), F32)], axis=0)
        if d > 1:
            lsc[rows, :] = lse_t.T
        else:
            lse_ref[rows, :] = lse_t.T

    scores(0)
    softmax(0)
    scores(1)
    outputs(0)
    softmax(1)
    scores(2)

    def body(u, carry):
        outputs(u - 2, True)
        softmax(u - 1)
        scores(u, True)
        return carry

    lax.fori_loop(3, n_units, body, 0)
    outputs(n_units - 2)
    softmax(n_units - 1)
    outputs(n_units - 1)
    if d > 1:
        for p in range(4):
            o_ref[:, p * LANES:(p + 1) * LANES] = osc[p].astype(BF16)
        lse_ref[...] = lsc[...]


def _attention(q, k, v, sinks, *, window, kv_div, batch, seq):
    d = q.shape[1]
    nb = ATTN_CHUNK // (d * BLOCK)
    kw = k.shape[-1]
    rows = nb * BLOCK
    cur = lambda b, i: (b, 0, i, 0)
    prev = lambda b, i: (b, 0, jnp.maximum(i * nb - 1, 0), 0)
    in_specs = [pl.BlockSpec((None, d, rows, A_GROUP_WIDTH), cur),
                pl.BlockSpec((None, d, rows, kw), cur),
                pl.BlockSpec((None, d, BLOCK, kw), prev),
                pl.BlockSpec((None, d, rows, kw), cur),
                pl.BlockSpec((None, d, BLOCK, kw), prev)]
    args = [q, k, k, v, v]
    use_sink = sinks is not None
    if use_sink:
        in_specs = [pl.BlockSpec(memory_space=pltpu.SMEM)] + in_specs
        args = [sinks] + args
    return pl.pallas_call(
        functools.partial(_attn_kernel, d, nb, window, kv_div, use_sink),
        grid=(batch, seq // ATTN_CHUNK),
        in_specs=in_specs,
        out_specs=[pl.BlockSpec((None, ATTN_CHUNK, A_GROUP_WIDTH), lambda b, i: (b, i, 0)),
                   pl.BlockSpec((None, ATTN_CHUNK, LANES), lambda b, i: (b, i, 0))],
        out_shape=[jax.ShapeDtypeStruct((batch, seq, A_GROUP_WIDTH), BF16),
                   jax.ShapeDtypeStruct((batch, seq, LANES), F32)],
        scratch_shapes=[pltpu.VMEM((4, ATTN_CHUNK if d > 1 else 8, LANES), F32),
                        pltpu.VMEM((ATTN_CHUNK if d > 1 else 8, LANES), F32),
                        pltpu.VMEM((4, 2 * BLOCK, 2 * BLOCK), F32),
                        pltpu.VMEM((4, 2 * BLOCK, 2 * BLOCK), BF16),
                        pltpu.VMEM((4, 8, 2 * BLOCK), F32)],
        name=f"attn_d{d}_w{window}",
        compiler_params=_params(2),
    )(*args)


def _gla_kernel(q_ref, k_ref, v_ref, gl_ref, cr_ref, gu_ref, gb_ref, on_ref, tri_ref, y_ref, st_ref):
    tm = q_ref.shape[0]
    width = C_HEADS * C_DK
    n_chunk = GLA_SUB // C_CHUNK

    @pl.when(pl.program_id(1) == 0)
    def _():
        st_ref[...] = jnp.zeros_like(st_ref)

    tri = tri_ref[...]
    keep = tri > 0
    on_gain = on_ref[...]
    head_of_lane = lax.broadcasted_iota(jnp.int32, (1, LANES), 1) // C_DK
    nt = (((1,), (1,)), ((), ()))
    tn = (((0,), (0,)), ((), ()))

    def chunked(a):
        return a.reshape(n_chunk, C_CHUNK, width)

    for sub in range(tm // GLA_SUB):
        rows = slice(sub * GLA_SUB, (sub + 1) * GLA_SUB)
        x = jnp.dot(gl_ref[rows, :], gu_ref[...], preferred_element_type=F32) + gb_ref[...]
        lg = (jnp.minimum(x, 0.0) - jnp.log(1.0 + jnp.exp(-jnp.abs(x)))) * (1.0 / C_TAU)
        g1 = lg.astype(BF16)
        g2 = (lg - g1.astype(F32)).astype(BF16)
        bs = jnp.dot(tri, jnp.concatenate([g1, g2], axis=1), preferred_element_type=F32)
        b = chunked(bs[:, 0:width] + bs[:, width:2 * width])
        b_mid = b[:, C_CHUNK // 2 - 1:C_CHUNK // 2, :]
        b_last = b[:, C_CHUNK - 1:C_CHUNK, :]
        q = chunked(q_ref[rows, :].astype(F32) * (C_DK ** -0.5))
        k = chunked(k_ref[rows, :].astype(F32))
        v = v_ref[rows, :]
        qd = (q * jnp.exp(b)).astype(BF16)
        qs = (q * jnp.exp(b - b_mid)).astype(BF16)
        ks = (k * jnp.exp(b_mid - b)).astype(BF16)
        kd = (k * jnp.exp(b_last - b)).astype(BF16)
        decay = jnp.exp(b_last)
        for hh in range(C_HEADS):
            cols = slice(hh * LANES, (hh + 1) * LANES)
            kcols = slice((hh // 2) * LANES, (hh // 2 + 1) * LANES)
            mine = head_of_lane == (hh % 2)
            qs_h = jnp.where(mine, qs[:, :, kcols], jnp.zeros((), BF16))
            qd_h = jnp.where(mine, qd[:, :, kcols], jnp.zeros((), BF16))
            att = lax.dot_general(qs_h.reshape(GLA_SUB, LANES), ks[:, :, kcols].reshape(GLA_SUB, LANES),
                                  nt, preferred_element_type=F32)
            att = jnp.where(keep, att, 0.0).astype(BF16)
            o = jnp.dot(att, v[:, cols], preferred_element_type=F32)
            st = st_ref[hh]
            inter = []
            for c in range(n_chunk):
                inter.append(lax.dot_general(qd_h[c], st.astype(BF16), nt, preferred_element_type=F32))
                inc = lax.dot_general(v[c * C_CHUNK:(c + 1) * C_CHUNK, cols], kd[c, :, kcols], tn,
                                      preferred_element_type=F32)
                st = st * decay[c, :, kcols] + inc
            st_ref[hh] = st
            o = o + jnp.concatenate(inter, axis=0)
            gate = cr_ref[rows, cols].astype(F32)
            y_ref[rows, cols] = (_rms(o, on_gain) * (gate * _sigmoid(gate))).astype(BF16)


def _gla(q, k, v, gl, cr, gate_up, gate_bias, out_norm, tri, layer, batch, seq):
    t = q.shape[0]
    tm = TOKEN_TILE
    per_seq = seq // tm
    width = C_HEADS * C_DV
    kwidth = C_HEADS * C_DK
    tok = lambda w: pl.BlockSpec((tm, w), lambda b, i: (b * per_seq + i, 0))
    return pl.pallas_call(
        _gla_kernel,
        grid=(batch, per_seq),
        in_specs=[tok(kwidth), tok(kwidth), tok(width), tok(LANES), tok(width),
                  pl.BlockSpec((None, LANES, kwidth), lambda b, i: (layer, 0, 0)),
                  pl.BlockSpec((None, 1, kwidth), lambda b, i: (layer, 0, 0)),
                  pl.BlockSpec((None, 1, C_DV), lambda b, i: (layer, 0, 0)),
                  pl.BlockSpec((GLA_SUB, GLA_SUB), lambda b, i: (0, 0))],
        out_specs=tok(width),
        out_shape=jax.ShapeDtypeStruct((t, width), BF16),
        scratch_shapes=[pltpu.VMEM((C_HEADS, C_DV, LANES), F32)],
        name="gla",
        compiler_params=_params(2),
    )(q, k, v, gl, cr, gate_up, gate_bias, out_norm, tri)


def _merge_kernel(x_ref, gm_ref, o1, o2, o3, l1, l2, l3, yb_ref, yc_ref, wg_ref, wb_ref, wo_ref, ex_ref, out_ref):
    x = x_ref[...]
    h = _rms(x, gm_ref[...]).astype(BF16)
    la, lb, lc = l1[...], l2[...], l3[...]
    top = jnp.maximum(jnp.maximum(la, lb), lc)
    ea, eb, ec = jnp.exp2(la - top), jnp.exp2(lb - top), jnp.exp2(lc - top)
    inv = 1.0 / (ea + eb + ec)
    expand = ex_ref[...]
    low_lanes = lax.broadcasted_iota(jnp.int32, (1, LANES), 1) < LANES // 2
    ya = None
    for e, o in ((ea, o1), (eb, o2), (ec, o3)):
        w = e * inv
        w_hi = w.astype(BF16)
        w_lo = pltpu.roll(w - w_hi.astype(F32), LANES // 2, 1).astype(BF16)
        wx = jnp.dot(jnp.where(low_lanes, w_hi, w_lo), expand, preferred_element_type=F32)
        term = wx * o[...].astype(F32)
        ya = term if ya is None else ya + term
    merged = None
    for i, y in enumerate((ya.astype(BF16), yb_ref[...], yc_ref[...])):
        gate = _sigmoid(jnp.dot(h, wg_ref[:, i * D_MODEL:(i + 1) * D_MODEL], preferred_element_type=F32))
        term = gate * jnp.dot(y, wb_ref[i], preferred_element_type=F32)
        merged = term if merged is None else merged + term
    out_ref[...] = x + jnp.dot(merged.astype(BF16), wo_ref[...], preferred_element_type=F32)


def _merge(x, gain, oa, la, yb, yc, w_gate, w_branch, w_out, expand, layer):
    t = x.shape[0]
    tm = TOKEN_TILE
    tok = lambda w: pl.BlockSpec((tm, w), lambda i: (i, 0))
    bw = A_GROUP_WIDTH
    return pl.pallas_call(
        _merge_kernel,
        grid=(t // tm,),
        in_specs=[tok(D_MODEL),
                  pl.BlockSpec((None, 1, D_MODEL), lambda i: (layer, 0, 0)),
                  tok(bw), tok(bw), tok(bw), tok(LANES), tok(LANES), tok(LANES), tok(bw), tok(bw),
                  _resident((None, D_MODEL, N_BRANCH * D_MODEL), lambda i: (layer, 0, 0)),
                  _resident((None, N_BRANCH, bw, D_MODEL), lambda i: (layer, 0, 0, 0)),
                  _resident((None, D_MODEL, D_MODEL), lambda i: (layer, 0, 0)),
                  pl.BlockSpec((LANES, bw), lambda i: (0, 0))],
        out_specs=tok(D_MODEL),
        out_shape=jax.ShapeDtypeStruct((t, D_MODEL), F32),
        name="merge",
        compiler_params=_params(1),
    )(x, gain, *oa, *la, yb, yc, w_gate, w_branch, w_out, expand)


def _pack_w_in(w_in):
    n_layer = w_in.shape[0]
    lead = w_in.shape[:2]
    edges = np.cumsum([0, 4608, 512, 256, 256, 256, 512, C_GATE_RANK, 512, N_BRANCH * D_MODEL])
    seg = [w_in[:, :, edges[i]:edges[i + 1]] for i in range(9)]
    a, b_q, b_kv, c_q, c_k, c_v, c_glow, c_r, gates = seg
    b_kv = b_kv.reshape(lead + (2, 2, 1, HEAD_DIM))
    b_kv = jnp.broadcast_to(b_kv, lead + (2, 2, 2, HEAD_DIM)).reshape(lead + (512,))
    c_glow = jnp.pad(c_glow, ((0, 0), (0, 0), (0, LANES - C_GATE_RANK)))
    packed = jnp.concatenate([a, b_q, b_kv, c_q, c_k, c_v, c_glow, c_r], axis=-1)
    assert packed.shape == (n_layer, D_MODEL, W_PACKED)
    return packed.astype(BF16), gates.astype(BF16)


def kernel(x, positions, norm_ffn1, w_ffn1_in, w_ffn1_out, norm_mix, w_in, a_q_norm, a_k_norm, b_q_norm, b_k_norm,
           b_sinks, c_gate_up, c_gate_bias, c_out_norm, w_branch, w_out, norm_ffn2, w_ffn2_in, w_ffn2_out):
    batch, seq, _ = x.shape
    n_layer = w_in.shape[0]
    assert seq % ATTN_CHUNK == 0 and x.shape[2] == D_MODEL

    w_packed, w_gate = _pack_w_in(w_in)
    w1_in, w1_out = w_ffn1_in.astype(BF16), w_ffn1_out.astype(BF16)
    w2_in, w2_out = w_ffn2_in.astype(BF16), w_ffn2_out.astype(BF16)
    wb, wo = w_branch.astype(BF16), w_out.astype(BF16)
    row = lambda g: g[:, None, :]
    head_gains = jnp.stack([jnp.tile(g, (1, LANES // HEAD_DIM)) for g in (a_q_norm, a_k_norm, b_q_norm, b_k_norm)],
                           axis=1)
    gate_up = jnp.pad(c_gate_up, ((0, 0), (0, LANES - C_GATE_RANK), (0, 0))).astype(BF16)
    gate_bias = row(c_gate_bias)

    lane_head = np.arange(LANES) // HEAD_DIM
    ones_bd = jnp.asarray(lane_head[:, None] == lane_head[None, :], BF16)
    expand = jnp.asarray((np.arange(LANES) % (LANES // 2))[:, None] == (np.arange(A_GROUP_WIDTH) // HEAD_DIM)[None, :],
                         BF16)
    idx = np.arange(GLA_SUB)
    tri = jnp.asarray((idx[:, None] >= idx[None, :]) & (idx[:, None] // C_CHUNK == idx[None, :] // C_CHUNK), BF16)

    cos, sin = _rope_tables(positions)
    xt = x.reshape(batch * seq, D_MODEL)
    for l in range(n_layer):
        xt = _ffn(xt, row(norm_ffn1), w1_in, w1_out, l)
        (qa1, ka1, va1, qa2, ka2, va2, qa3, ka3, va3, qb, kb, vb, qc, kc, vc, gl, cr) = _inproj(
            xt, row(norm_mix), cos, sin, w_packed, head_gains, ones_bd, l, batch, seq)
        oa, la = [], []
        for (window, d), (q, k, v) in zip(A_CONFIGS, ((qa1, ka1, va1), (qa2, ka2, va2), (qa3, ka3, va3))):
            o, lse = _attention(q, k, v, None, window=window // d, kv_div=1, batch=batch, seq=seq)
            oa.append(o.reshape(batch * seq, A_GROUP_WIDTH))
            la.append(lse.reshape(batch * seq, LANES))
        yb, _ = _attention(qb, kb, vb, b_sinks[l], window=B_WINDOW - 1, kv_div=2, batch=batch, seq=seq)
        yb = yb.reshape(batch * seq, A_GROUP_WIDTH)
        yc = _gla(qc, kc, vc, gl, cr, gate_up, gate_bias, row(c_out_norm), tri, l, batch, seq)
        xt = _merge(xt, row(norm_mix), oa, la, yb, yc, w_gate, wb, wo, expand, l)
        xt = _ffn(xt, row(norm_ffn2), w2_in, w2_out, l)
    return xt.reshape(batch, seq, D_MODEL)
```

```python
import functools

import numpy as np
import jax
import jax.numpy as jnp
from jax import lax
from jax.experimental import pallas as pl
from jax.experimental.pallas import tpu as pltpu

F32 = jnp.float32
BF16 = jnp.bfloat16

D_MODEL = 1024
D_FF = 2816
HEAD_DIM = 64
NORM_EPS = 1e-6
ROPE_THETA = 10000.0
LOG2E = float(np.log2(np.e))
Q_SCALE = HEAD_DIM ** -0.5 * LOG2E
LANES = 128
BLOCK = 128
A_CONFIGS = ((128, 1), (512, 4), (2048, 16))
A_GROUP_WIDTH = 512
B_WINDOW = 128
C_HEADS = 4
C_DK = 64
C_DV = 128
C_GATE_RANK = 16
C_TAU = 16.0
C_CHUNK = 128
GLA_SUB = 256
GLA_TILE = 2048
N_BRANCH = 3

A_SEG = 3 * A_GROUP_WIDTH
OFF_A = 0
OFF_B = 3 * A_SEG
B_SEG = 1024
OFF_C = OFF_B + B_SEG
C_SEG = 1664
W_PACKED = OFF_C + C_SEG

TOKEN_TILE = 512
ATTN_CHUNK = 2048
MXU_COLS = 256
FFN_CHUNKS = ((0, 1536), (1536, D_FF))
VMEM_LIMIT = 56 * 2**20


def _params(n_axes):
    return pltpu.CompilerParams(dimension_semantics=("arbitrary",) * n_axes, vmem_limit_bytes=VMEM_LIMIT)


def _rms(x, gain):
    return x * lax.rsqrt(jnp.mean(x * x, axis=-1, keepdims=True) + NORM_EPS) * gain


def _sigmoid(x):
    return 0.5 * jnp.tanh(0.5 * x) + 0.5


def _resident(block, index_map):
    return pl.BlockSpec(block, index_map, pipeline_mode=pl.Buffered(1))


def _rope_kernel(pos_ref, inv_ref, sgn_ref, cos_ref, sin_ref):
    ang = pos_ref[...].astype(F32) * inv_ref[...]
    cos_ref[...] = jnp.cos(ang)
    sin_ref[...] = jnp.sin(ang) * sgn_ref[...]


def _rope_tables(positions):
    t = positions.size
    inv = ROPE_THETA ** (-jnp.arange(0, HEAD_DIM, 2, dtype=F32) / HEAD_DIM)
    inv = jnp.tile(inv, LANES // (HEAD_DIM // 2))[None, :]
    sgn = jnp.tile(jnp.concatenate([-jnp.ones(HEAD_DIM // 2, F32), jnp.ones(HEAD_DIM // 2, F32)]),
                   LANES // HEAD_DIM)[None, :]
    tm = TOKEN_TILE
    tab = jax.ShapeDtypeStruct((t, LANES), F32)
    return pl.pallas_call(
        _rope_kernel,
        grid=(t // tm,),
        in_specs=[pl.BlockSpec((tm, LANES), lambda i: (i, 0)),
                  pl.BlockSpec((1, LANES), lambda i: (0, 0)),
                  pl.BlockSpec((1, LANES), lambda i: (0, 0))],
        out_specs=[pl.BlockSpec((tm, LANES), lambda i: (i, 0))] * 2,
        out_shape=[tab, tab],
        name="rope",
        compiler_params=_params(1),
    )(jnp.broadcast_to(positions.reshape(t, 1), (t, LANES)), inv, sgn)


def _ffn_kernel(x_ref, g_ref, wg_ref, wu_ref, wo_ref, o_ref):
    x = x_ref[...]
    h = _rms(x, g_ref[...]).astype(BF16)
    acc = None
    for lo, hi in FFN_CHUNKS:
        sl = slice(lo, hi)
        g = jnp.dot(h, wg_ref[:, sl], preferred_element_type=F32)
        u = jnp.dot(h, wu_ref[:, sl], preferred_element_type=F32)
        a = (g * _sigmoid(g) * u).astype(BF16)
        part = jnp.dot(a, wo_ref[sl, :], preferred_element_type=F32)
        acc = part if acc is None else acc + part
    o_ref[...] = x + 0.5 * acc


def _ffn(x, gain, w_in, w_out, layer):
    t = x.shape[0]
    tm = TOKEN_TILE
    return pl.pallas_call(
        _ffn_kernel,
        grid=(t // tm,),
        in_specs=[pl.BlockSpec((tm, D_MODEL), lambda i: (i, 0)),
                  pl.BlockSpec((None, 1, D_MODEL), lambda i: (layer, 0, 0)),
                  _resident((None, D_MODEL, D_FF), lambda i: (layer, 0, 0)),
                  _resident((None, D_MODEL, D_FF), lambda i: (layer, 0, 1)),
                  _resident((None, D_FF, D_MODEL), lambda i: (layer, 0, 0))],
        out_specs=pl.BlockSpec((tm, D_MODEL), lambda i: (i, 0)),
        out_shape=jax.ShapeDtypeStruct((t, D_MODEL), F32),
        name="ffn",
        compiler_params=_params(1),
    )(x, gain, w_in, w_in, w_out)


def _inproj_kernel(x_ref, gm_ref, cos_ref, sin_ref, w_ref, gains_ref, ones_ref,
                   qa1, ka1, va1, qa2, ka2, va2, qa3, ka3, va3, qb, kb, vb, qc, kc, vc, gl, cr,
                   hs_ref, hp_ref, tb_ref):
    tm = x_ref.shape[0]
    n_slab = D_MODEL // LANES
    h = _rms(x_ref[...], gm_ref[...])
    h_bf = h.astype(BF16)
    cos0 = cos_ref[...]
    sin0 = sin_ref[...]
    ones = ones_ref[...]
    gains = gains_ref[...]

    first_half = (lax.broadcasted_iota(jnp.int32, (1, LANES), 1) % HEAD_DIM) < (HEAD_DIM // 2)

    def norm_rope(y, gain, cos, sin):
        ss = jnp.dot((y * y).astype(BF16), ones, preferred_element_type=F32)
        yn = y * lax.rsqrt(ss * (1.0 / HEAD_DIM) + NORM_EPS) * gain
        partner = jnp.where(first_half, pltpu.roll(yn, LANES - HEAD_DIM // 2, 1), pltpu.roll(yn, HEAD_DIM // 2, 1))
        return yn * cos + partner * sin

    for s in range(n_slab):
        hs_ref[s] = h[:, s * LANES:(s + 1) * LANES]
    tb_ref[0] = cos0
    tb_ref[1] = sin0

    def class_major(d):
        if d == 1:
            return h_bf, cos0, sin0
        rows = tm // d
        for c in range(d):
            for s in range(n_slab):
                hp_ref[c * rows:(c + 1) * rows, s * LANES:(s + 1) * LANES] = (
                    hs_ref[s, pl.ds(c, rows, stride=d), :].astype(BF16))
        cos = jnp.concatenate([tb_ref[0, pl.ds(c, rows, stride=d), :] for c in range(d)], axis=0)
        sin = jnp.concatenate([tb_ref[1, pl.ds(c, rows, stride=d), :] for c in range(d)], axis=0)
        return hp_ref[...], cos, sin

    def store_classes(ref, col, val, d):
        rows = tm // d
        for c in range(d):
            ref[c, :, col:col + LANES] = val[c * rows:(c + 1) * rows, :]

    for gi, (qo, ko, vo) in enumerate(((qa1, ka1, va1), (qa2, ka2, va2), (qa3, ka3, va3))):
        d = A_CONFIGS[gi][1]
        hp, cos, sin = class_major(d)
        yq, yk, yv = [jnp.dot(hp, w_ref[:, pl.ds(OFF_A + part * A_SEG + gi * A_GROUP_WIDTH, A_GROUP_WIDTH)],
                              preferred_element_type=F32) for part in range(3)]
        for j in range(A_GROUP_WIDTH // LANES):
            c0 = j * LANES
            store_classes(qo, c0, (norm_rope(yq[:, c0:c0 + LANES], gains[0:1], cos, sin) * Q_SCALE).astype(BF16), d)
            store_classes(ko, c0, norm_rope(yk[:, c0:c0 + LANES], gains[1:2], cos, sin).astype(BF16), d)
            store_classes(vo, c0, yv[:, c0:c0 + LANES].astype(BF16), d)

    y = jnp.dot(h_bf, w_ref[:, OFF_B:OFF_B + B_SEG], preferred_element_type=F32)
    for j in range(4):
        c0 = j * LANES
        qb[0, :, c0:c0 + LANES] = (norm_rope(y[:, c0:c0 + LANES], gains[2:3], cos0, sin0) * Q_SCALE).astype(BF16)
    for j in range(2):
        c0 = j * LANES
        kb[0, :, c0:c0 + LANES] = norm_rope(y[:, 512 + c0:512 + c0 + LANES], gains[3:4], cos0, sin0).astype(BF16)
        vb[0, :, c0:c0 + LANES] = y[:, 768 + c0:768 + c0 + LANES].astype(BF16)

    y = jnp.dot(h_bf, w_ref[:, OFF_C:OFF_C + C_SEG], preferred_element_type=F32)
    qc[...] = y[:, 0:256].astype(BF16)
    kc[...] = y[:, 256:512].astype(BF16)
    vc[...] = y[:, 512:1024].astype(BF16)
    gl[...] = y[:, 1024:1152].astype(BF16)
    cr[...] = y[:, 1152:1664].astype(BF16)


def _inproj(x, gain, cos, sin, w_packed, head_gains, ones_bd, layer, batch, seq):
    t = x.shape[0]
    tm = TOKEN_TILE
    tiles_per_seq = seq // tm

    def cm_spec(d, width):
        return pl.BlockSpec((None, d, tm // d, width),
                            lambda i: (i // tiles_per_seq, 0, i % tiles_per_seq, 0))

    def cm_shape(d, width):
        return jax.ShapeDtypeStruct((batch, d, seq // d, width), BF16)

    def tok_spec(width):
        return pl.BlockSpec((tm, width), lambda i: (i, 0))

    out_specs, out_shape = [], []
    for _, d in A_CONFIGS:
        for _ in range(3):
            out_specs.append(cm_spec(d, A_GROUP_WIDTH))
            out_shape.append(cm_shape(d, A_GROUP_WIDTH))
    for width in (512, 256, 256):
        out_specs.append(cm_spec(1, width))
        out_shape.append(cm_shape(1, width))
    for width in (256, 256, 512, 128, 512):
        out_specs.append(tok_spec(width))
        out_shape.append(jax.ShapeDtypeStruct((t, width), BF16))

    return pl.pallas_call(
        _inproj_kernel,
        grid=(t // tm,),
        in_specs=[tok_spec(D_MODEL),
                  pl.BlockSpec((None, 1, D_MODEL), lambda i: (layer, 0, 0)),
                  tok_spec(LANES), tok_spec(LANES),
                  _resident((None, D_MODEL, W_PACKED), lambda i: (layer, 0, 0)),
                  pl.BlockSpec((None, 4, LANES), lambda i: (layer, 0, 0)),
                  pl.BlockSpec((LANES, LANES), lambda i: (0, 0))],
        out_specs=out_specs,
        out_shape=out_shape,
        scratch_shapes=[pltpu.VMEM((D_MODEL // LANES, tm, LANES), F32),
                        pltpu.VMEM((tm, D_MODEL), BF16),
                        pltpu.VMEM((2, tm, LANES), F32)],
        name="inproj",
        compiler_params=_params(1),
    )(x, gain, cos, sin, w_packed, head_gains, ones_bd)


def _attn_kernel(d, nb, window, kv_div, use_sink, *refs):
    if use_sink:
        sink_ref, q_ref, kc_ref, kp_ref, vc_ref, vp_ref, o_ref, lse_ref, osc, lsc, s_scr, p_scr, st_scr = refs
    else:
        q_ref, kc_ref, kp_ref, vc_ref, vp_ref, o_ref, lse_ref, osc, lsc, s_scr, p_scr, st_scr = refs
    chunk = pl.program_id(1)
    lane = lax.broadcasted_iota(jnp.int32, (1, LANES), 1)
    head_of_lane = lane // HEAD_DIM
    key = lax.broadcasted_iota(jnp.int32, (2 * BLOCK, 2 * BLOCK), 0)
    col = lax.broadcasted_iota(jnp.int32, (2 * BLOCK, 2 * BLOCK), 1)
    dist = (col % BLOCK) + BLOCK - key
    band = (dist >= 0) & (dist <= window)
    col_row = lax.broadcasted_iota(jnp.int32, (1, 2 * BLOCK), 1)
    sub8 = lax.broadcasted_iota(jnp.int32, (8, LANES), 0)
    nt = (((1,), (1,)), ((), ()))
    tn = (((0,), (0,)), ((), ()))

    n_units = d * nb

    def locate(u):
        c = u // nb if d > 1 else 0
        j = u % nb if nb > 1 else 0
        if isinstance(j, int):
            return c, j, j * BLOCK, max(j - 1, 0) * BLOCK
        return c, j, pl.multiple_of(j * BLOCK, BLOCK), pl.multiple_of(jnp.maximum(j - 1, 0) * BLOCK, BLOCK)

    def window_rows(cur_ref, prev_ref, u, p, in_loop=False):
        c, j, row0, prow = locate(u)
        kcol = (p // kv_div) * LANES
        ksl = slice(kcol, kcol + LANES)
        cur = cur_ref[c, pl.ds(row0, BLOCK), ksl]
        if isinstance(j, int):
            first = j == 0
        elif d == 1 and in_loop:
            first = False
        else:
            first = None
        if first is None:
            prev = jnp.where(j == 0, prev_ref[c, :, ksl], cur_ref[c, pl.ds(prow, BLOCK), ksl])
        else:
            prev = prev_ref[c, :, ksl] if first else cur_ref[c, pl.ds(prow, BLOCK), ksl]
        return jnp.concatenate([prev, cur], axis=0)

    def scores(u, in_loop=False):
        c, j, row0, _ = locate(u)
        seq_start = jnp.logical_and(chunk == 0, j == 0)
        valid = band & jnp.logical_or(key >= BLOCK, jnp.logical_not(seq_start))
        for p in range(4):
            qp = q_ref[c, pl.ds(row0, BLOCK), p * LANES:(p + 1) * LANES]
            q2 = jnp.concatenate([jnp.where(head_of_lane == e, qp, jnp.zeros_like(qp)) for e in range(2)], axis=0)
            s = lax.dot_general(window_rows(kc_ref, kp_ref, u, p, in_loop), q2, nt,
                                preferred_element_type=F32)
            s = jnp.where(valid, s, -jnp.inf)
            s_scr[p] = s
            m = jnp.max(s, axis=0, keepdims=True)
            if use_sink:
                m = jnp.maximum(m, sink_row(p))
            st_scr[p, 0:1, :] = m

    def sink_row(p):
        return jnp.where(col_row < BLOCK, sink_ref[2 * p], sink_ref[2 * p + 1]) * LOG2E

    ones8 = jnp.ones((8, 2 * BLOCK), BF16)

    def softmax(u):
        for p in range(4):
            m = st_scr[p, 0:1, :]
            p_scr[p] = jnp.exp2(s_scr[p] - m).astype(BF16)
            st_scr[p, 2:3, :] = m

    def outputs(u, in_loop=False):
        c, _, row0, _ = locate(u)
        start = row0 * d + c
        rows = pl.ds(start, BLOCK, stride=d) if d > 1 else pl.ds(start, BLOCK)
        lse8 = jnp.zeros((8, LANES), F32)
        for p in range(4):
            pr, m = p_scr[p], st_scr[p, 2:3, :]
            ob = lax.dot_general(window_rows(vc_ref, vp_ref, u, p, in_loop), pr, tn,
                                 preferred_element_type=F32)
            den = jnp.dot(ones8, pr, preferred_element_type=F32)[0:1, :]
            if use_sink:
                den = den + jnp.exp2(sink_row(p) - m)
            inv = 1.0 / den
            o_t = jnp.concatenate([ob[0:HEAD_DIM, 0:BLOCK] * inv[:, 0:BLOCK],
                                   ob[HEAD_DIM:LANES, BLOCK:2 * BLOCK] * inv[:, BLOCK:2 * BLOCK]], axis=0)
            if d > 1:
                osc[p, rows, :] = o_t.T
            else:
                o_ref[rows, p * LANES:(p + 1) * LANES] = o_t.T.astype(BF16)
            l2 = m + jnp.log2(den)
            lse8 = jnp.where(sub8 == 2 * p, l2[:, 0:BLOCK], lse8)
            lse8 = jnp.where(sub8 == 2 * p + 1, l2[:, BLOCK:2 * BLOCK], lse8)
        lse_t = jnp.concatenate([lse8, jnp.zeros((LANES - 8, LANES), F32)], axis=0)
        if d > 1:
            lsc[rows, :] = lse_t.T
        else:
            lse_ref[rows, :] = lse_t.T

    scores(0)
    softmax(0)
    scores(1)
    outputs(0)
    softmax(1)
    scores(2)

    def body(u, carry):
        outputs(u - 2, True)
        softmax(u - 1)
        scores(u, True)
        return carry

    lax.fori_loop(3, n_units, body, 0)
    outputs(n_units - 2)
    softmax(n_units - 1)
    outputs(n_units - 1)
    if d > 1:
        for p in range(4):
            o_ref[:, p * LANES:(p + 1) * LANES] = osc[p].astype(BF16)
        lse_ref[...] = lsc[...]


def _attention(q, k, v, sinks, *, window, kv_div, batch, seq):
    d = q.shape[1]
    chunk = ATTN_CHUNK if d * BLOCK == ATTN_CHUNK else 2 * ATTN_CHUNK
    nb = chunk // (d * BLOCK)
    kw = k.shape[-1]
    rows = nb * BLOCK
    cur = lambda b, i: (b, 0, i, 0)
    prev = lambda b, i: (b, 0, jnp.maximum(i * nb - 1, 0), 0)
    in_specs = [pl.BlockSpec((None, d, rows, A_GROUP_WIDTH), cur),
                pl.BlockSpec((None, d, rows, kw), cur),
                pl.BlockSpec((None, d, BLOCK, kw), prev),
                pl.BlockSpec((None, d, rows, kw), cur),
                pl.BlockSpec((None, d, BLOCK, kw), prev)]
    args = [q, k, k, v, v]
    use_sink = sinks is not None
    if use_sink:
        in_specs = [pl.BlockSpec(memory_space=pltpu.SMEM)] + in_specs
        args = [sinks] + args
    return pl.pallas_call(
        functools.partial(_attn_kernel, d, nb, window, kv_div, use_sink),
        grid=(batch, seq // chunk),
        in_specs=in_specs,
        out_specs=[pl.BlockSpec((None, chunk, A_GROUP_WIDTH), lambda b, i: (b, i, 0)),
                   pl.BlockSpec((None, chunk, LANES), lambda b, i: (b, i, 0))],
        out_shape=[jax.ShapeDtypeStruct((batch, seq, A_GROUP_WIDTH), BF16),
                   jax.ShapeDtypeStruct((batch, seq, LANES), F32)],
        scratch_shapes=[pltpu.VMEM((4, chunk if d > 1 else 8, LANES), F32),
                        pltpu.VMEM((chunk if d > 1 else 8, LANES), F32),
                        pltpu.VMEM((4, 2 * BLOCK, 2 * BLOCK), F32),
                        pltpu.VMEM((4, 2 * BLOCK, 2 * BLOCK), BF16),
                        pltpu.VMEM((4, 8, 2 * BLOCK), F32)],
        name=f"attn_d{d}_w{window}",
        compiler_params=_params(2),
    )(*args)


def _gla_kernel(q_ref, k_ref, v_ref, gl_ref, cr_ref, gu_ref, gb_ref, on_ref, tri_ref, y_ref,
                st_ref, qd_s, qs_s, ks_s, kd_s, dec_s):
    tm = q_ref.shape[0]
    width = C_HEADS * C_DK
    n_chunk = GLA_SUB // C_CHUNK

    @pl.when(pl.program_id(1) == 0)
    def _():
        st_ref[...] = jnp.zeros_like(st_ref)

    tri = tri_ref[...]
    keep = tri > 0
    on_gain = on_ref[...]
    head_of_lane = lax.broadcasted_iota(jnp.int32, (1, LANES), 1) // C_DK
    nt = (((1,), (1,)), ((), ()))
    tn = (((0,), (0,)), ((), ()))

    def chunked(a):
        return a.reshape(n_chunk, C_CHUNK, width)

    def sub_rows(sub):
        start = sub * GLA_SUB
        return pl.ds(start if isinstance(sub, int) else pl.multiple_of(start, GLA_SUB), GLA_SUB)

    def prepare(sub):
        rows = sub_rows(sub)
        x = jnp.dot(gl_ref[rows, :], gu_ref[...], preferred_element_type=F32) + gb_ref[...]
        lg = (jnp.minimum(x, 0.0) - jnp.log(1.0 + jnp.exp(-jnp.abs(x)))) * (1.0 / C_TAU)
        g1 = lg.astype(BF16)
        g2 = (lg - g1.astype(F32)).astype(BF16)
        bs = jnp.dot(tri, jnp.concatenate([g1, g2], axis=1), preferred_element_type=F32)
        b = chunked(bs[:, 0:width] + bs[:, width:2 * width])
        b_mid = b[:, C_CHUNK // 2 - 1:C_CHUNK // 2, :]
        b_last = b[:, C_CHUNK - 1:C_CHUNK, :]
        q = chunked(q_ref[rows, :].astype(F32) * (C_DK ** -0.5))
        k = chunked(k_ref[rows, :].astype(F32))
        qd_s[...] = (q * jnp.exp(b)).astype(BF16)
        qs_s[...] = (q * jnp.exp(b - b_mid)).astype(BF16)
        ks_s[...] = (k * jnp.exp(b_mid - b)).astype(BF16)
        kd_s[...] = (k * jnp.exp(b_last - b)).astype(BF16)
        dec_s[...] = jnp.broadcast_to(jnp.exp(b_last), dec_s.shape)

    def finish(sub):
        rows = sub_rows(sub)
        v = v_ref[rows, :]
        qd, qs, ks, kd = qd_s[...], qs_s[...], ks_s[...], kd_s[...]
        decay = dec_s[:, 0:1, :]
        for hh in range(C_HEADS):
            cols = slice(hh * LANES, (hh + 1) * LANES)
            kcols = slice((hh // 2) * LANES, (hh // 2 + 1) * LANES)
            mine = head_of_lane == (hh % 2)
            qs_h = jnp.where(mine, qs[:, :, kcols], jnp.zeros((), BF16))
            qd_h = jnp.where(mine, qd[:, :, kcols], jnp.zeros((), BF16))
            att = lax.dot_general(qs_h.reshape(GLA_SUB, LANES), ks[:, :, kcols].reshape(GLA_SUB, LANES),
                                  nt, preferred_element_type=F32)
            att = jnp.where(keep, att, 0.0).astype(BF16)
            o = jnp.dot(att, v[:, cols], preferred_element_type=F32)
            st = st_ref[hh]
            inter = []
            for c in range(n_chunk):
                inter.append(lax.dot_general(qd_h[c], st.astype(BF16), nt, preferred_element_type=F32))
                inc = lax.dot_general(v[c * C_CHUNK:(c + 1) * C_CHUNK, cols], kd[c, :, kcols], tn,
                                      preferred_element_type=F32)
                st = st * decay[c, :, kcols] + inc
            st_ref[hh] = st
            o = o + jnp.concatenate(inter, axis=0)
            gate = cr_ref[rows, cols].astype(F32)
            y_ref[rows, cols] = (_rms(o, on_gain) * (gate * _sigmoid(gate))).astype(BF16)

    n_sub = tm // GLA_SUB
    prepare(0)

    def body(sub, carry):
        finish(sub - 1)
        prepare(sub)
        return carry

    lax.fori_loop(1, n_sub, body, 0)
    finish(n_sub - 1)


def _gla(q, k, v, gl, cr, gate_up, gate_bias, out_norm, tri, layer, batch, seq):
    t = q.shape[0]
    tm = GLA_TILE
    per_seq = seq // tm
    width = C_HEADS * C_DV
    kwidth = C_HEADS * C_DK
    tok = lambda w: pl.BlockSpec((tm, w), lambda b, i: (b * per_seq + i, 0))
    return pl.pallas_call(
        _gla_kernel,
        grid=(batch, per_seq),
        in_specs=[tok(kwidth), tok(kwidth), tok(width), tok(LANES), tok(width),
                  pl.BlockSpec((None, LANES, kwidth), lambda b, i: (layer, 0, 0)),
                  pl.BlockSpec((None, 1, kwidth), lambda b, i: (layer, 0, 0)),
                  pl.BlockSpec((None, 1, C_DV), lambda b, i: (layer, 0, 0)),
                  pl.BlockSpec((GLA_SUB, GLA_SUB), lambda b, i: (0, 0))],
        out_specs=tok(width),
        out_shape=jax.ShapeDtypeStruct((t, width), BF16),
        scratch_shapes=[pltpu.VMEM((C_HEADS, C_DV, LANES), F32)]
        + [pltpu.VMEM((GLA_SUB // C_CHUNK, C_CHUNK, kwidth), BF16)] * 4
        + [pltpu.VMEM((GLA_SUB // C_CHUNK, 8, kwidth), F32)],
        name="gla",
        compiler_params=_params(2),
    )(q, k, v, gl, cr, gate_up, gate_bias, out_norm, tri)


def _merge_kernel(x_ref, gm_ref, o1, o2, o3, l1, l2, l3, yb_ref, yc_ref, wg_ref, wb_ref, wo_ref, ex_ref, out_ref):
    x = x_ref[...]
    h = _rms(x, gm_ref[...]).astype(BF16)
    la, lb, lc = l1[...], l2[...], l3[...]
    top = jnp.maximum(jnp.maximum(la, lb), lc)
    ea, eb, ec = jnp.exp2(la - top), jnp.exp2(lb - top), jnp.exp2(lc - top)
    inv = 1.0 / (ea + eb + ec)
    expand = ex_ref[...]
    low_lanes = lax.broadcasted_iota(jnp.int32, (1, LANES), 1) < LANES // 2
    ya = None
    for e, o in ((ea, o1), (eb, o2), (ec, o3)):
        w = e * inv
        w_hi = w.astype(BF16)
        w_lo = pltpu.roll(w - w_hi.astype(F32), LANES // 2, 1).astype(BF16)
        wx = jnp.dot(jnp.where(low_lanes, w_hi, w_lo), expand, preferred_element_type=F32)
        term = wx * o[...].astype(F32)
        ya = term if ya is None else ya + term
    merged = None
    for i, y in enumerate((ya.astype(BF16), yb_ref[...], yc_ref[...])):
        gate = _sigmoid(jnp.dot(h, wg_ref[:, i * D_MODEL:(i + 1) * D_MODEL], preferred_element_type=F32))
        term = gate * jnp.dot(y, wb_ref[i], preferred_element_type=F32)
        merged = term if merged is None else merged + term
    out_ref[...] = x + jnp.dot(merged.astype(BF16), wo_ref[...], preferred_element_type=F32)


def _merge(x, gain, oa, la, yb, yc, w_gate, w_branch, w_out, expand, layer):
    t = x.shape[0]
    tm = TOKEN_TILE
    tok = lambda w: pl.BlockSpec((tm, w), lambda i: (i, 0))
    bw = A_GROUP_WIDTH
    return pl.pallas_call(
        _merge_kernel,
        grid=(t // tm,),
        in_specs=[tok(D_MODEL),
                  pl.BlockSpec((None, 1, D_MODEL), lambda i: (layer, 0, 0)),
                  tok(bw), tok(bw), tok(bw), tok(LANES), tok(LANES), tok(LANES), tok(bw), tok(bw),
                  _resident((None, D_MODEL, N_BRANCH * D_MODEL), lambda i: (layer, 0, 0)),
                  _resident((None, N_BRANCH, bw, D_MODEL), lambda i: (layer, 0, 0, 0)),
                  _resident((None, D_MODEL, D_MODEL), lambda i: (layer, 0, 0)),
                  pl.BlockSpec((LANES, bw), lambda i: (0, 0))],
        out_specs=tok(D_MODEL),
        out_shape=jax.ShapeDtypeStruct((t, D_MODEL), F32),
        name="merge",
        compiler_params=_params(1),
    )(x, gain, *oa, *la, yb, yc, w_gate, w_branch, w_out, expand)


def _pack_w_in(w_in):
    n_layer = w_in.shape[0]
    lead = w_in.shape[:2]
    edges = np.cumsum([0, 4608, 512, 256, 256, 256, 512, C_GATE_RANK, 512, N_BRANCH * D_MODEL])
    seg = [w_in[:, :, edges[i]:edges[i + 1]] for i in range(9)]
    a, b_q, b_kv, c_q, c_k, c_v, c_glow, c_r, gates = seg
    b_kv = b_kv.reshape(lead + (2, 2, 1, HEAD_DIM))
    b_kv = jnp.broadcast_to(b_kv, lead + (2, 2, 2, HEAD_DIM)).reshape(lead + (512,))
    c_glow = jnp.pad(c_glow, ((0, 0), (0, 0), (0, LANES - C_GATE_RANK)))
    packed = jnp.concatenate([a, b_q, b_kv, c_q, c_k, c_v, c_glow, c_r], axis=-1)
    assert packed.shape == (n_layer, D_MODEL, W_PACKED)
    return packed.astype(BF16), gates.astype(BF16)


def kernel(x, positions, norm_ffn1, w_ffn1_in, w_ffn1_out, norm_mix, w_in, a_q_norm, a_k_norm, b_q_norm, b_k_norm,
           b_sinks, c_gate_up, c_gate_bias, c_out_norm, w_branch, w_out, norm_ffn2, w_ffn2_in, w_ffn2_out):
    batch, seq, _ = x.shape
    n_layer = w_in.shape[0]
    assert seq % (2 * ATTN_CHUNK) == 0 and seq % GLA_TILE == 0 and x.shape[2] == D_MODEL

    w_packed, w_gate = _pack_w_in(w_in)
    w1_in, w1_out = w_ffn1_in.astype(BF16), w_ffn1_out.astype(BF16)
    w2_in, w2_out = w_ffn2_in.astype(BF16), w_ffn2_out.astype(BF16)
    wb, wo = w_branch.astype(BF16), w_out.astype(BF16)
    row = lambda g: g[:, None, :]
    head_gains = jnp.stack([jnp.tile(g, (1, LANES // HEAD_DIM)) for g in (a_q_norm, a_k_norm, b_q_norm, b_k_norm)],
                           axis=1)
    gate_up = jnp.pad(c_gate_up, ((0, 0), (0, LANES - C_GATE_RANK), (0, 0))).astype(BF16)
    gate_bias = row(c_gate_bias)

    lane_head = np.arange(LANES) // HEAD_DIM
    ones_bd = jnp.asarray(lane_head[:, None] == lane_head[None, :], BF16)
    expand = jnp.asarray((np.arange(LANES) % (LANES // 2))[:, None] == (np.arange(A_GROUP_WIDTH) // HEAD_DIM)[None, :],
                         BF16)
    idx = np.arange(GLA_SUB)
    tri = jnp.asarray((idx[:, None] >= idx[None, :]) & (idx[:, None] // C_CHUNK == idx[None, :] // C_CHUNK), BF16)

    cos, sin = _rope_tables(positions)
    xt = x.reshape(batch * seq, D_MODEL)
    for l in range(n_layer):
        xt = _ffn(xt, row(norm_ffn1), w1_in, w1_out, l)
        (qa1, ka1, va1, qa2, ka2, va2, qa3, ka3, va3, qb, kb, vb, qc, kc, vc, gl, cr) = _inproj(
            xt, row(norm_mix), cos, sin, w_packed, head_gains, ones_bd, l, batch, seq)
        oa, la = [], []
        for (window, d), (q, k, v) in zip(A_CONFIGS, ((qa1, ka1, va1), (qa2, ka2, va2), (qa3, ka3, va3))):
            o, lse = _attention(q, k, v, None, window=window // d, kv_div=1, batch=batch, seq=seq)
            oa.append(o.reshape(batch * seq, A_GROUP_WIDTH))
            la.append(lse.reshape(batch * seq, LANES))
        yb, _ = _attention(qb, kb, vb, b_sinks[l], window=B_WINDOW - 1, kv_div=2, batch=batch, seq=seq)
        yb = yb.reshape(batch * seq, A_GROUP_WIDTH)
        yc = _gla(qc, kc, vc, gl, cr, gate_up, gate_bias, row(c_out_norm), tri, l, batch, seq)
        xt = _merge(xt, row(norm_mix), oa, la, yb, yc, w_gate, wb, wo, expand, l)
        xt = _ffn(xt, row(norm_ffn2), w2_in, w2_out, l)
    return xt.reshape(batch, seq, D_MODEL)
```

```python
import functools

import numpy as np
import jax
import jax.numpy as jnp
from jax import lax
from jax.experimental import pallas as pl
from jax.experimental.pallas import tpu as pltpu

F32 = jnp.float32
BF16 = jnp.bfloat16

D_MODEL = 1024
D_FF = 2816
HEAD_DIM = 64
NORM_EPS = 1e-6
ROPE_THETA = 10000.0
LOG2E = float(np.log2(np.e))
Q_SCALE = HEAD_DIM ** -0.5 * LOG2E
LANES = 128
BLOCK = 128
A_CONFIGS = ((128, 1), (512, 4), (2048, 16))
A_GROUP_WIDTH = 512
B_WINDOW = 128
C_HEADS = 4
C_DK = 64
C_DV = 128
C_GATE_RANK = 16
C_TAU = 16.0
C_CHUNK = 128
GLA_SUB = 256
GLA_TILE = 2048
N_BRANCH = 3

A_SEG = 3 * A_GROUP_WIDTH
OFF_A = 0
OFF_B = 3 * A_SEG
B_SEG = 1024
OFF_C = OFF_B + B_SEG
C_SEG = 1664
W_PACKED = OFF_C + C_SEG

TOKEN_TILE = 512
ATTN_CHUNK = 2048
MXU_COLS = 256
FFN_CHUNKS = ((0, 1536), (1536, D_FF))
VMEM_LIMIT = 56 * 2**20


def _params(n_axes):
    return pltpu.CompilerParams(dimension_semantics=("arbitrary",) * n_axes, vmem_limit_bytes=VMEM_LIMIT)


def _rms(x, gain):
    return x * lax.rsqrt(jnp.mean(x * x, axis=-1, keepdims=True) + NORM_EPS) * gain


def _dot_t(a, w_t):
    return lax.dot_general(a, w_t, (((1,), (1,)), ((), ())), preferred_element_type=F32)


def _sigmoid(x):
    return 0.5 * jnp.tanh(0.5 * x) + 0.5


def _resident(block, index_map):
    return pl.BlockSpec(block, index_map, pipeline_mode=pl.Buffered(1))


def _rope_kernel(pos_ref, inv_ref, sgn_ref, cos_ref, sin_ref):
    ang = pos_ref[...].astype(F32) * inv_ref[...]
    cos_ref[...] = jnp.cos(ang)
    sin_ref[...] = jnp.sin(ang) * sgn_ref[...]


def _rope_tables(positions):
    t = positions.size
    inv = ROPE_THETA ** (-jnp.arange(0, HEAD_DIM, 2, dtype=F32) / HEAD_DIM)
    inv = jnp.tile(inv, LANES // (HEAD_DIM // 2))[None, :]
    sgn = jnp.tile(jnp.concatenate([-jnp.ones(HEAD_DIM // 2, F32), jnp.ones(HEAD_DIM // 2, F32)]),
                   LANES // HEAD_DIM)[None, :]
    tm = TOKEN_TILE
    tab = jax.ShapeDtypeStruct((t, LANES), F32)
    return pl.pallas_call(
        _rope_kernel,
        grid=(t // tm,),
        in_specs=[pl.BlockSpec((tm, LANES), lambda i: (i, 0)),
                  pl.BlockSpec((1, LANES), lambda i: (0, 0)),
                  pl.BlockSpec((1, LANES), lambda i: (0, 0))],
        out_specs=[pl.BlockSpec((tm, LANES), lambda i: (i, 0))] * 2,
        out_shape=[tab, tab],
        name="rope",
        compiler_params=_params(1),
    )(jnp.broadcast_to(positions.reshape(t, 1), (t, LANES)), inv, sgn)


def _ffn_kernel(x_ref, g_ref, wg_ref, wu_ref, wo_ref, o_ref):
    x = x_ref[...]
    h = _rms(x, g_ref[...]).astype(BF16)
    acc = None
    for lo, hi in FFN_CHUNKS:
        sl = slice(lo, hi)
        g = jnp.dot(h, wg_ref[:, sl], preferred_element_type=F32)
        u = jnp.dot(h, wu_ref[:, sl], preferred_element_type=F32)
        a = (g * _sigmoid(g) * u).astype(BF16)
        part = jnp.dot(a, wo_ref[sl, :], preferred_element_type=F32)
        acc = part if acc is None else acc + part
    o_ref[...] = x + 0.5 * acc


def _ffn(x, gain, w_in, w_out, layer):
    t = x.shape[0]
    tm = TOKEN_TILE
    return pl.pallas_call(
        _ffn_kernel,
        grid=(t // tm,),
        in_specs=[pl.BlockSpec((tm, D_MODEL), lambda i: (i, 0)),
                  pl.BlockSpec((None, 1, D_MODEL), lambda i: (layer, 0, 0)),
                  _resident((None, D_MODEL, D_FF), lambda i: (layer, 0, 0)),
                  _resident((None, D_MODEL, D_FF), lambda i: (layer, 0, 1)),
                  _resident((None, D_FF, D_MODEL), lambda i: (layer, 0, 0))],
        out_specs=pl.BlockSpec((tm, D_MODEL), lambda i: (i, 0)),
        out_shape=jax.ShapeDtypeStruct((t, D_MODEL), F32),
        name="ffn",
        compiler_params=_params(1),
    )(x, gain, w_in, w_in, w_out)


def _inproj_kernel(x_ref, gm_ref, cos_ref, sin_ref, w_ref, gains_ref, ones_ref,
                   qa1, ka1, va1, qa2, ka2, va2, qa3, ka3, va3, qb, kb, vb, qc, kc, vc, gl, cr,
                   hs_ref, hp_ref, tb_ref):
    tm = x_ref.shape[0]
    n_slab = D_MODEL // LANES
    h = _rms(x_ref[...], gm_ref[...])
    h_bf = h.astype(BF16)
    cos0 = cos_ref[...]
    sin0 = sin_ref[...]
    ones = ones_ref[...]
    gains = gains_ref[...]

    first_half = (lax.broadcasted_iota(jnp.int32, (1, LANES), 1) % HEAD_DIM) < (HEAD_DIM // 2)

    def norm_rope(y, gain, cos, sin):
        ms = jnp.dot((y * y).astype(BF16), ones, preferred_element_type=F32)
        yn = y * lax.rsqrt(ms + NORM_EPS) * gain
        partner = jnp.where(first_half, pltpu.roll(yn, LANES - HEAD_DIM // 2, 1), pltpu.roll(yn, HEAD_DIM // 2, 1))
        return yn * cos + partner * sin

    for s in range(n_slab):
        hs_ref[s] = h[:, s * LANES:(s + 1) * LANES]
    tb_ref[0] = cos0
    tb_ref[1] = sin0

    def class_major(d):
        if d == 1:
            return h_bf, cos0, sin0
        rows = tm // d
        for c in range(d):
            for s in range(n_slab):
                hp_ref[c * rows:(c + 1) * rows, s * LANES:(s + 1) * LANES] = (
                    hs_ref[s, pl.ds(c, rows, stride=d), :].astype(BF16))
        cos = jnp.concatenate([tb_ref[0, pl.ds(c, rows, stride=d), :] for c in range(d)], axis=0)
        sin = jnp.concatenate([tb_ref[1, pl.ds(c, rows, stride=d), :] for c in range(d)], axis=0)
        return hp_ref[...], cos, sin

    def store_classes(ref, col, val, d):
        rows = tm // d
        for c in range(d):
            ref[c, :, col:col + LANES] = val[c * rows:(c + 1) * rows, :]

    for gi, (qo, ko, vo) in enumerate(((qa1, ka1, va1), (qa2, ka2, va2), (qa3, ka3, va3))):
        d = A_CONFIGS[gi][1]
        hp, cos, sin = class_major(d)
        yq, yk, yv = [_dot_t(hp, w_ref[pl.ds(OFF_A + part * A_SEG + gi * A_GROUP_WIDTH, A_GROUP_WIDTH), :])
                      for part in range(3)]
        for j in range(A_GROUP_WIDTH // LANES):
            c0 = j * LANES
            store_classes(qo, c0, norm_rope(yq[:, c0:c0 + LANES], gains[0:1], cos, sin).astype(BF16), d)
            store_classes(ko, c0, norm_rope(yk[:, c0:c0 + LANES], gains[1:2], cos, sin).astype(BF16), d)
            store_classes(vo, c0, yv[:, c0:c0 + LANES].astype(BF16), d)

    y = _dot_t(h_bf, w_ref[OFF_B:OFF_B + B_SEG, :])
    for j in range(4):
        c0 = j * LANES
        qb[0, :, c0:c0 + LANES] = norm_rope(y[:, c0:c0 + LANES], gains[2:3], cos0, sin0).astype(BF16)
    for j in range(2):
        c0 = j * LANES
        kb[0, :, c0:c0 + LANES] = norm_rope(y[:, 512 + c0:512 + c0 + LANES], gains[3:4], cos0, sin0).astype(BF16)
        vb[0, :, c0:c0 + LANES] = y[:, 768 + c0:768 + c0 + LANES].astype(BF16)

    y = _dot_t(h_bf, w_ref[OFF_C:OFF_C + C_SEG, :])
    qc[...] = y[:, 0:256].astype(BF16)
    kc[...] = y[:, 256:512].astype(BF16)
    vc[...] = y[:, 512:1024].astype(BF16)
    gl[...] = y[:, 1024:1152].astype(BF16)
    cr[...] = y[:, 1152:1664].astype(BF16)


def _inproj(x, gain, cos, sin, w_packed, head_gains, ones_bd, layer, batch, seq):
    t = x.shape[0]
    tm = TOKEN_TILE
    tiles_per_seq = seq // tm

    def cm_spec(d, width):
        return pl.BlockSpec((None, d, tm // d, width),
                            lambda i: (i // tiles_per_seq, 0, i % tiles_per_seq, 0))

    def cm_shape(d, width):
        return jax.ShapeDtypeStruct((batch, d, seq // d, width), BF16)

    def tok_spec(width):
        return pl.BlockSpec((tm, width), lambda i: (i, 0))

    out_specs, out_shape = [], []
    for _, d in A_CONFIGS:
        for _ in range(3):
            out_specs.append(cm_spec(d, A_GROUP_WIDTH))
            out_shape.append(cm_shape(d, A_GROUP_WIDTH))
    for width in (512, 256, 256):
        out_specs.append(cm_spec(1, width))
        out_shape.append(cm_shape(1, width))
    for width in (256, 256, 512, 128, 512):
        out_specs.append(tok_spec(width))
        out_shape.append(jax.ShapeDtypeStruct((t, width), BF16))

    return pl.pallas_call(
        _inproj_kernel,
        grid=(t // tm,),
        in_specs=[tok_spec(D_MODEL),
                  pl.BlockSpec((None, 1, D_MODEL), lambda i: (layer, 0, 0)),
                  tok_spec(LANES), tok_spec(LANES),
                  _resident((None, W_PACKED, D_MODEL), lambda i: (layer, 0, 0)),
                  pl.BlockSpec((None, 4, LANES), lambda i: (layer, 0, 0)),
                  pl.BlockSpec((LANES, LANES), lambda i: (0, 0))],
        out_specs=out_specs,
        out_shape=out_shape,
        scratch_shapes=[pltpu.VMEM((D_MODEL // LANES, tm, LANES), F32),
                        pltpu.VMEM((tm, D_MODEL), BF16),
                        pltpu.VMEM((2, tm, LANES), F32)],
        name="inproj",
        compiler_params=_params(1),
    )(x, gain, cos, sin, w_packed, head_gains, ones_bd)


def _attn_kernel(d, nb, window, kv_div, use_sink, *refs):
    if use_sink:
        sink_ref, q_ref, kc_ref, kp_ref, vc_ref, vp_ref, o_ref, lse_ref, osc, lsc, s_scr, p_scr, st_scr = refs
    else:
        q_ref, kc_ref, kp_ref, vc_ref, vp_ref, o_ref, lse_ref, osc, lsc, s_scr, p_scr, st_scr = refs
    chunk = pl.program_id(1)
    lane = lax.broadcasted_iota(jnp.int32, (1, LANES), 1)
    head_of_lane = lane // HEAD_DIM
    key = lax.broadcasted_iota(jnp.int32, (2 * BLOCK, 2 * BLOCK), 0)
    col = lax.broadcasted_iota(jnp.int32, (2 * BLOCK, 2 * BLOCK), 1)
    dist = (col % BLOCK) + BLOCK - key
    band = (dist >= 0) & (dist <= window)
    col_row = lax.broadcasted_iota(jnp.int32, (1, 2 * BLOCK), 1)
    sub8 = lax.broadcasted_iota(jnp.int32, (8, LANES), 0)
    nt = (((1,), (1,)), ((), ()))
    tn = (((0,), (0,)), ((), ()))

    n_units = d * nb

    def locate(u):
        c = u // nb if d > 1 else 0
        j = u % nb if nb > 1 else 0
        if isinstance(j, int):
            return c, j, j * BLOCK, max(j - 1, 0) * BLOCK
        return c, j, pl.multiple_of(j * BLOCK, BLOCK), pl.multiple_of(jnp.maximum(j - 1, 0) * BLOCK, BLOCK)

    def window_rows(cur_ref, prev_ref, u, p, in_loop=False):
        c, j, row0, prow = locate(u)
        kcol = (p // kv_div) * LANES
        ksl = slice(kcol, kcol + LANES)
        cur = cur_ref[c, pl.ds(row0, BLOCK), ksl]
        if isinstance(j, int):
            first = j == 0
        elif d == 1 and in_loop:
            first = False
        else:
            first = None
        if first is None:
            prev = jnp.where(j == 0, prev_ref[c, :, ksl], cur_ref[c, pl.ds(prow, BLOCK), ksl])
        else:
            prev = prev_ref[c, :, ksl] if first else cur_ref[c, pl.ds(prow, BLOCK), ksl]
        return jnp.concatenate([prev, cur], axis=0)

    def scores(u, in_loop=False):
        c, j, row0, _ = locate(u)
        seq_start = jnp.logical_and(chunk == 0, j == 0)
        valid = band & jnp.logical_or(key >= BLOCK, jnp.logical_not(seq_start))
        for p in range(4):
            qp = q_ref[c, pl.ds(row0, BLOCK), p * LANES:(p + 1) * LANES]
            q2 = jnp.concatenate([jnp.where(head_of_lane == e, qp, jnp.zeros_like(qp)) for e in range(2)], axis=0)
            s = lax.dot_general(window_rows(kc_ref, kp_ref, u, p, in_loop), q2, nt,
                                preferred_element_type=F32)
            s = jnp.where(valid, s, -jnp.inf)
            s_scr[p] = s
            m = jnp.max(s, axis=0, keepdims=True)
            if use_sink:
                m = jnp.maximum(m, sink_row(p))
            st_scr[p, 0:1, :] = m

    def sink_row(p):
        return jnp.where(col_row < BLOCK, sink_ref[2 * p], sink_ref[2 * p + 1]) * LOG2E

    ones8 = jnp.ones((8, 2 * BLOCK), BF16)

    def softmax(u):
        for p in range(4):
            m = st_scr[p, 0:1, :]
            p_scr[p] = jnp.exp2(s_scr[p] - m).astype(BF16)
            st_scr[p, 2:3, :] = m

    def outputs(u, in_loop=False):
        c, _, row0, _ = locate(u)
        start = row0 * d + c
        rows = pl.ds(start, BLOCK, stride=d) if d > 1 else pl.ds(start, BLOCK)
        lse8 = jnp.zeros((8, LANES), F32)
        for p in range(4):
            pr, m = p_scr[p], st_scr[p, 2:3, :]
            ob = lax.dot_general(window_rows(vc_ref, vp_ref, u, p, in_loop), pr, tn,
                                 preferred_element_type=F32)
            den = jnp.dot(ones8, pr, preferred_element_type=F32)[0:1, :]
            if use_sink:
                den = den + jnp.exp2(sink_row(p) - m)
            inv = 1.0 / den
            o_t = jnp.concatenate([ob[0:HEAD_DIM, 0:BLOCK] * inv[:, 0:BLOCK],
                                   ob[HEAD_DIM:LANES, BLOCK:2 * BLOCK] * inv[:, BLOCK:2 * BLOCK]], axis=0)
            if d > 1:
                osc[p, rows, :] = o_t.T
            else:
                o_ref[rows, p * LANES:(p + 1) * LANES] = o_t.T.astype(BF16)
            l2 = m + jnp.log2(den)
            lse8 = jnp.where(sub8 == 2 * p, l2[:, 0:BLOCK], lse8)
            lse8 = jnp.where(sub8 == 2 * p + 1, l2[:, BLOCK:2 * BLOCK], lse8)
        lse_t = jnp.concatenate([lse8, jnp.zeros((LANES - 8, LANES), F32)], axis=0)
        if d > 1:
            lsc[rows, :] = lse_t.T
        else:
            lse_ref[rows, :] = lse_t.T

    scores(0)
    softmax(0)
    scores(1)
    outputs(0)
    softmax(1)
    scores(2)

    def body(u, carry):
        outputs(u - 2, True)
        softmax(u - 1)
        scores(u, True)
        return carry

    lax.fori_loop(3, n_units, body, 0)
    outputs(n_units - 2)
    softmax(n_units - 1)
    outputs(n_units - 1)
    if d > 1:
        for p in range(4):
            o_ref[:, p * LANES:(p + 1) * LANES] = osc[p].astype(BF16)
        lse_ref[...] = lsc[...]


def _attention(q, k, v, sinks, *, window, kv_div, batch, seq):
    d = q.shape[1]
    chunk = ATTN_CHUNK
    nb = chunk // (d * BLOCK)
    kw = k.shape[-1]
    rows = nb * BLOCK
    cur = lambda b, i: (b, 0, i, 0)
    prev = lambda b, i: (b, 0, jnp.maximum(i * nb - 1, 0), 0)
    in_specs = [pl.BlockSpec((None, d, rows, A_GROUP_WIDTH), cur),
                pl.BlockSpec((None, d, rows, kw), cur),
                pl.BlockSpec((None, d, BLOCK, kw), prev),
                pl.BlockSpec((None, d, rows, kw), cur),
                pl.BlockSpec((None, d, BLOCK, kw), prev)]
    args = [q, k, k, v, v]
    use_sink = sinks is not None
    if use_sink:
        in_specs = [pl.BlockSpec(memory_space=pltpu.SMEM)] + in_specs
        args = [sinks] + args
    return pl.pallas_call(
        functools.partial(_attn_kernel, d, nb, window, kv_div, use_sink),
        grid=(batch, seq // chunk),
        in_specs=in_specs,
        out_specs=[pl.BlockSpec((None, chunk, A_GROUP_WIDTH), lambda b, i: (b, i, 0)),
                   pl.BlockSpec((None, chunk, LANES), lambda b, i: (b, i, 0))],
        out_shape=[jax.ShapeDtypeStruct((batch, seq, A_GROUP_WIDTH), BF16),
                   jax.ShapeDtypeStruct((batch, seq, LANES), F32)],
        scratch_shapes=[pltpu.VMEM((4, chunk if d > 1 else 8, LANES), F32),
                        pltpu.VMEM((chunk if d > 1 else 8, LANES), F32),
                        pltpu.VMEM((4, 2 * BLOCK, 2 * BLOCK), F32),
                        pltpu.VMEM((4, 2 * BLOCK, 2 * BLOCK), BF16),
                        pltpu.VMEM((4, 8, 2 * BLOCK), F32)],
        name=f"attn_d{d}_w{window}",
        compiler_params=_params(2),
    )(*args)


def _gla_kernel(q_ref, k_ref, v_ref, gl_ref, cr_ref, gu_ref, gb_ref, on_ref, tri_ref, y_ref,
                st_ref, qd_s, qs_s, ks_s, kd_s, dec_s):
    tm = q_ref.shape[0]
    width = C_HEADS * C_DK
    n_chunk = GLA_SUB // C_CHUNK

    @pl.when(pl.program_id(1) == 0)
    def _():
        st_ref[...] = jnp.zeros_like(st_ref)

    tri = tri_ref[...]
    keep = tri > 0
    on_gain = on_ref[...]
    head_of_lane = lax.broadcasted_iota(jnp.int32, (1, LANES), 1) // C_DK
    nt = (((1,), (1,)), ((), ()))
    tn = (((0,), (0,)), ((), ()))

    def chunked(a):
        return a.reshape(n_chunk, C_CHUNK, width)

    def sub_rows(sub):
        start = sub * GLA_SUB
        return pl.ds(start if isinstance(sub, int) else pl.multiple_of(start, GLA_SUB), GLA_SUB)

    def prepare(sub):
        rows = sub_rows(sub)
        x = jnp.dot(gl_ref[rows, :], gu_ref[...], preferred_element_type=F32) + gb_ref[...]
        lg = (jnp.minimum(x, 0.0) - jnp.log(1.0 + jnp.exp(-jnp.abs(x)))) * (1.0 / C_TAU)
        g1 = lg.astype(BF16)
        g2 = (lg - g1.astype(F32)).astype(BF16)
        bs = jnp.dot(tri, jnp.concatenate([g1, g2], axis=1), preferred_element_type=F32)
        b = chunked(bs[:, 0:width] + bs[:, width:2 * width])
        b_mid = b[:, C_CHUNK // 2 - 1:C_CHUNK // 2, :]
        b_last = b[:, C_CHUNK - 1:C_CHUNK, :]
        q = chunked(q_ref[rows, :].astype(F32) * (C_DK ** -0.5))
        k = chunked(k_ref[rows, :].astype(F32))
        qd_s[...] = (q * jnp.exp(b)).astype(BF16)
        qs_s[...] = (q * jnp.exp(b - b_mid)).astype(BF16)
        ks_s[...] = (k * jnp.exp(b_mid - b)).astype(BF16)
        kd_s[...] = (k * jnp.exp(b_last - b)).astype(BF16)
        dec_s[...] = jnp.broadcast_to(jnp.exp(b_last), dec_s.shape)

    def finish(sub):
        rows = sub_rows(sub)
        v = v_ref[rows, :]
        qd, qs, ks, kd = qd_s[...], qs_s[...], ks_s[...], kd_s[...]
        decay = dec_s[:, 0:1, :]
        for hh in range(C_HEADS):
            cols = slice(hh * LANES, (hh + 1) * LANES)
            kcols = slice((hh // 2) * LANES, (hh // 2 + 1) * LANES)
            mine = head_of_lane == (hh % 2)
            qs_h = jnp.where(mine, qs[:, :, kcols], jnp.zeros((), BF16))
            qd_h = jnp.where(mine, qd[:, :, kcols], jnp.zeros((), BF16))
            att = lax.dot_general(qs_h.reshape(GLA_SUB, LANES), ks[:, :, kcols].reshape(GLA_SUB, LANES),
                                  nt, preferred_element_type=F32)
            att = jnp.where(keep, att, 0.0).astype(BF16)
            o = jnp.dot(att, v[:, cols], preferred_element_type=F32)
            st = st_ref[hh]
            inter = []
            for c in range(n_chunk):
                inter.append(lax.dot_general(qd_h[c], st.astype(BF16), nt, preferred_element_type=F32))
                inc = lax.dot_general(v[c * C_CHUNK:(c + 1) * C_CHUNK, cols], kd[c, :, kcols], tn,
                                      preferred_element_type=F32)
                st = st * decay[c, :, kcols] + inc
            st_ref[hh] = st
            o = o + jnp.concatenate(inter, axis=0)
            gate = cr_ref[rows, cols].astype(F32)
            y_ref[rows, cols] = (_rms(o, on_gain) * (gate * _sigmoid(gate))).astype(BF16)

    n_sub = tm // GLA_SUB
    prepare(0)

    def body(sub, carry):
        finish(sub - 1)
        prepare(sub)
        return carry

    lax.fori_loop(1, n_sub, body, 0)
    finish(n_sub - 1)


def _gla(q, k, v, gl, cr, gate_up, gate_bias, out_norm, tri, layer, batch, seq):
    t = q.shape[0]
    tm = GLA_TILE
    per_seq = seq // tm
    width = C_HEADS * C_DV
    kwidth = C_HEADS * C_DK
    tok = lambda w: pl.BlockSpec((tm, w), lambda b, i: (b * per_seq + i, 0))
    return pl.pallas_call(
        _gla_kernel,
        grid=(batch, per_seq),
        in_specs=[tok(kwidth), tok(kwidth), tok(width), tok(LANES), tok(width),
                  pl.BlockSpec((None, LANES, kwidth), lambda b, i: (layer, 0, 0)),
                  pl.BlockSpec((None, 1, kwidth), lambda b, i: (layer, 0, 0)),
                  pl.BlockSpec((None, 1, C_DV), lambda b, i: (layer, 0, 0)),
                  pl.BlockSpec((GLA_SUB, GLA_SUB), lambda b, i: (0, 0))],
        out_specs=tok(width),
        out_shape=jax.ShapeDtypeStruct((t, width), BF16),
        scratch_shapes=[pltpu.VMEM((C_HEADS, C_DV, LANES), F32)]
        + [pltpu.VMEM((GLA_SUB // C_CHUNK, C_CHUNK, kwidth), BF16)] * 4
        + [pltpu.VMEM((GLA_SUB // C_CHUNK, 8, kwidth), F32)],
        name="gla",
        compiler_params=_params(2),
    )(q, k, v, gl, cr, gate_up, gate_bias, out_norm, tri)


def _merge_kernel(x_ref, gm_ref, o1, o2, o3, l1, l2, l3, yb_ref, yc_ref, wg_ref, wb_ref, wo_ref, ex_ref, out_ref):
    x = x_ref[...]
    h = _rms(x, gm_ref[...]).astype(BF16)
    la, lb, lc = l1[...], l2[...], l3[...]
    top = jnp.maximum(jnp.maximum(la, lb), lc)
    ea, eb, ec = jnp.exp2(la - top), jnp.exp2(lb - top), jnp.exp2(lc - top)
    inv = 1.0 / (ea + eb + ec)
    expand = ex_ref[...]
    low_lanes = lax.broadcasted_iota(jnp.int32, (1, LANES), 1) < LANES // 2
    ya = None
    for e, o in ((ea, o1), (eb, o2), (ec, o3)):
        w = e * inv
        w_hi = w.astype(BF16)
        w_lo = pltpu.roll(w - w_hi.astype(F32), LANES // 2, 1).astype(BF16)
        wx = jnp.dot(jnp.where(low_lanes, w_hi, w_lo), expand, preferred_element_type=F32)
        term = wx * o[...].astype(F32)
        ya = term if ya is None else ya + term
    merged = None
    for i, y in enumerate((ya.astype(BF16), yb_ref[...], yc_ref[...])):
        gate = _sigmoid(_dot_t(h, wg_ref[i * D_MODEL:(i + 1) * D_MODEL, :]))
        term = gate * jnp.dot(y, wb_ref[i], preferred_element_type=F32)
        merged = term if merged is None else merged + term
    out_ref[...] = x + jnp.dot(merged.astype(BF16), wo_ref[...], preferred_element_type=F32)


def _merge(x, gain, oa, la, yb, yc, w_gate, w_branch, w_out, expand, layer):
    t = x.shape[0]
    tm = TOKEN_TILE
    tok = lambda w: pl.BlockSpec((tm, w), lambda i: (i, 0))
    bw = A_GROUP_WIDTH
    return pl.pallas_call(
        _merge_kernel,
        grid=(t // tm,),
        in_specs=[tok(D_MODEL),
                  pl.BlockSpec((None, 1, D_MODEL), lambda i: (layer, 0, 0)),
                  tok(bw), tok(bw), tok(bw), tok(LANES), tok(LANES), tok(LANES), tok(bw), tok(bw),
                  _resident((None, N_BRANCH * D_MODEL, D_MODEL), lambda i: (layer, 0, 0)),
                  _resident((None, N_BRANCH, bw, D_MODEL), lambda i: (layer, 0, 0, 0)),
                  _resident((None, D_MODEL, D_MODEL), lambda i: (layer, 0, 0)),
                  pl.BlockSpec((LANES, bw), lambda i: (0, 0))],
        out_specs=tok(D_MODEL),
        out_shape=jax.ShapeDtypeStruct((t, D_MODEL), F32),
        name="merge",
        compiler_params=_params(1),
    )(x, gain, *oa, *la, yb, yc, w_gate, w_branch, w_out, expand)


def _pack_w_in(w_in):
    w_t = jnp.swapaxes(w_in, 1, 2)
    n_layer = w_t.shape[0]
    edges = np.cumsum([0, 4608, 512, 256, 256, 256, 512, C_GATE_RANK, 512, N_BRANCH * D_MODEL])
    seg = [w_t[:, edges[i]:edges[i + 1], :] for i in range(9)]
    a, b_q, b_kv, c_q, c_k, c_v, c_glow, c_r, gates = seg
    b_kv = b_kv.reshape(n_layer, 2, 2, 1, HEAD_DIM, D_MODEL)
    b_kv = jnp.broadcast_to(b_kv, (n_layer, 2, 2, 2, HEAD_DIM, D_MODEL)).reshape(n_layer, 512, D_MODEL)
    c_glow = jnp.pad(c_glow, ((0, 0), (0, LANES - C_GATE_RANK), (0, 0)))
    packed = jnp.concatenate([a, b_q, b_kv, c_q, c_k, c_v, c_glow, c_r], axis=1)
    assert packed.shape == (n_layer, W_PACKED, D_MODEL)
    return packed.astype(BF16), gates.astype(BF16)


def kernel(x, positions, norm_ffn1, w_ffn1_in, w_ffn1_out, norm_mix, w_in, a_q_norm, a_k_norm, b_q_norm, b_k_norm,
           b_sinks, c_gate_up, c_gate_bias, c_out_norm, w_branch, w_out, norm_ffn2, w_ffn2_in, w_ffn2_out):
    batch, seq, _ = x.shape
    n_layer = w_in.shape[0]
    assert seq % ATTN_CHUNK == 0 and seq % GLA_TILE == 0 and x.shape[2] == D_MODEL

    w_packed, w_gate = _pack_w_in(w_in)
    w1_in, w1_out = w_ffn1_in.astype(BF16), w_ffn1_out.astype(BF16)
    w2_in, w2_out = w_ffn2_in.astype(BF16), w_ffn2_out.astype(BF16)
    wb, wo = w_branch.astype(BF16), w_out.astype(BF16)
    row = lambda g: g[:, None, :]
    head_gains = jnp.stack([jnp.tile(g * s, (1, LANES // HEAD_DIM)) for g, s in
                            ((a_q_norm, Q_SCALE), (a_k_norm, 1.0), (b_q_norm, Q_SCALE), (b_k_norm, 1.0))], axis=1)
    gate_up = jnp.pad(c_gate_up, ((0, 0), (0, LANES - C_GATE_RANK), (0, 0))).astype(BF16)
    gate_bias = row(c_gate_bias)

    lane_head = np.arange(LANES) // HEAD_DIM
    ones_bd = jnp.asarray((lane_head[:, None] == lane_head[None, :]) / HEAD_DIM, BF16)
    expand = jnp.asarray((np.arange(LANES) % (LANES // 2))[:, None] == (np.arange(A_GROUP_WIDTH) // HEAD_DIM)[None, :],
                         BF16)
    idx = np.arange(GLA_SUB)
    tri = jnp.asarray((idx[:, None] >= idx[None, :]) & (idx[:, None] // C_CHUNK == idx[None, :] // C_CHUNK), BF16)

    cos, sin = _rope_tables(positions)
    xt = x.reshape(batch * seq, D_MODEL)
    for l in range(n_layer):
        xt = _ffn(xt, row(norm_ffn1), w1_in, w1_out, l)
        (qa1, ka1, va1, qa2, ka2, va2, qa3, ka3, va3, qb, kb, vb, qc, kc, vc, gl, cr) = _inproj(
            xt, row(norm_mix), cos, sin, w_packed, head_gains, ones_bd, l, batch, seq)
        oa, la = [], []
        for (window, d), (q, k, v) in zip(A_CONFIGS, ((qa1, ka1, va1), (qa2, ka2, va2), (qa3, ka3, va3))):
            o, lse = _attention(q, k, v, None, window=window // d, kv_div=1, batch=batch, seq=seq)
            oa.append(o.reshape(batch * seq, A_GROUP_WIDTH))
            la.append(lse.reshape(batch * seq, LANES))
        yb, _ = _attention(qb, kb, vb, b_sinks[l], window=B_WINDOW - 1, kv_div=2, batch=batch, seq=seq)
        yb = yb.reshape(batch * seq, A_GROUP_WIDTH)
        yc = _gla(qc, kc, vc, gl, cr, gate_up, gate_bias, row(c_out_norm), tri, l, batch, seq)
        xt = _merge(xt, row(norm_mix), oa, la, yb, yc, w_gate, wb, wo, expand, l)
        xt = _ffn(xt, row(norm_ffn2), w2_in, w2_out, l)
    return xt.reshape(batch, seq, D_MODEL)
```

```python
import functools

import numpy as np
import jax
import jax.numpy as jnp
from jax import lax
from jax.experimental import pallas as pl
from jax.experimental.pallas import tpu as pltpu

F32 = jnp.float32
BF16 = jnp.bfloat16

D_MODEL = 1024
D_FF = 2816
HEAD_DIM = 64
NORM_EPS = 1e-6
ROPE_THETA = 10000.0
LOG2E = float(np.log2(np.e))
Q_SCALE = HEAD_DIM ** -0.5 * LOG2E
LANES = 128
ROPE_PACK_LANES = HEAD_DIM // 2
ROPE_PACK = LANES // ROPE_PACK_LANES
BLOCK = 128
A_CONFIGS = ((128, 1), (512, 4), (2048, 16))
A_GROUP_WIDTH = 512
B_WINDOW = 128
C_HEADS = 4
C_DK = 64
C_DV = 128
C_GATE_RANK = 16
C_TAU = 16.0
C_CHUNK = 128
GLA_SUB = 256
GLA_TILE = 2048
N_BRANCH = 3

A_SEG = 3 * A_GROUP_WIDTH
OFF_A = 0
OFF_B = 3 * A_SEG
B_SEG = 1024
OFF_C = OFF_B + B_SEG
C_SEG = 1664
W_PACKED = OFF_C + C_SEG

TOKEN_TILE = 512
ATTN_CHUNK = 2048
MXU_COLS = 256
FFN_CHUNKS = ((0, 1536), (1536, D_FF))
VMEM_LIMIT = 56 * 2**20


def _params(n_axes):
    return pltpu.CompilerParams(dimension_semantics=("arbitrary",) * n_axes, vmem_limit_bytes=VMEM_LIMIT)


def _rms(x, gain):
    return x * lax.rsqrt(jnp.mean(x * x, axis=-1, keepdims=True) + NORM_EPS) * gain


def _dot_t(a, w_t):
    return lax.dot_general(a, w_t, (((1,), (1,)), ((), ())), preferred_element_type=F32)


def _sigmoid(x):
    return 0.5 * jnp.tanh(0.5 * x) + 0.5


def _resident(block, index_map):
    return pl.BlockSpec(block, index_map, pipeline_mode=pl.Buffered(1))


def _rope_kernel(pos_ref, inv_ref, sgn_ref, cos_ref, sin_ref):
    rows = pos_ref.shape[0]
    ang = pos_ref[...].astype(F32) * inv_ref[...]
    tables = (jnp.cos(ang), jnp.sin(ang))
    group = lax.broadcasted_iota(jnp.int32, (1, LANES), 1) // ROPE_PACK_LANES
    sgn = sgn_ref[...]
    for j in range(ROPE_PACK):
        for tab, ref, scale in zip(tables, (cos_ref, sin_ref), (None, sgn)):
            own = jnp.where(group == j, tab, 0.0)
            full = own
            for shift in range(1, ROPE_PACK):
                full = full + pltpu.roll(own, shift * ROPE_PACK_LANES, 1)
            ref[pl.ds(j, rows, stride=ROPE_PACK), :] = full if scale is None else full * scale


def _rope_tables(positions):
    t = positions.size
    inv = ROPE_THETA ** (-jnp.arange(0, HEAD_DIM, 2, dtype=F32) / HEAD_DIM)
    inv = jnp.tile(inv, LANES // (HEAD_DIM // 2))[None, :]
    sgn = jnp.tile(jnp.concatenate([-jnp.ones(HEAD_DIM // 2, F32), jnp.ones(HEAD_DIM // 2, F32)]),
                   LANES // HEAD_DIM)[None, :]
    tm = TOKEN_TILE
    tab = jax.ShapeDtypeStruct((t, LANES), F32)
    return pl.pallas_call(
        _rope_kernel,
        grid=(t // tm,),
        in_specs=[pl.BlockSpec((tm // ROPE_PACK, LANES), lambda i: (i, 0)),
                  pl.BlockSpec((1, LANES), lambda i: (0, 0)),
                  pl.BlockSpec((1, LANES), lambda i: (0, 0))],
        out_specs=[pl.BlockSpec((tm, LANES), lambda i: (i, 0))] * 2,
        out_shape=[tab, tab],
        name="rope",
        compiler_params=_params(1),
    )(jnp.repeat(positions.reshape(t // ROPE_PACK, ROPE_PACK), ROPE_PACK_LANES, axis=1), inv, sgn)


def _ffn_kernel(x_ref, g_ref, wg_ref, wu_ref, wo_ref, o_ref):
    x = x_ref[...]
    h = _rms(x, g_ref[...]).astype(BF16)
    acc = None
    for lo, hi in FFN_CHUNKS:
        sl = slice(lo, hi)
        g = jnp.dot(h, wg_ref[:, sl], preferred_element_type=F32)
        u = jnp.dot(h, wu_ref[:, sl], preferred_element_type=F32)
        a = (g * _sigmoid(g) * u).astype(BF16)
        part = jnp.dot(a, wo_ref[sl, :], preferred_element_type=F32)
        acc = part if acc is None else acc + part
    o_ref[...] = x + 0.5 * acc


def _ffn(x, gain, w_in, w_out, layer):
    t = x.shape[0]
    tm = TOKEN_TILE
    return pl.pallas_call(
        _ffn_kernel,
        grid=(t // tm,),
        in_specs=[pl.BlockSpec((tm, D_MODEL), lambda i: (i, 0)),
                  pl.BlockSpec((None, 1, D_MODEL), lambda i: (layer, 0, 0)),
                  _resident((None, D_MODEL, D_FF), lambda i: (layer, 0, 0)),
                  _resident((None, D_MODEL, D_FF), lambda i: (layer, 0, 1)),
                  _resident((None, D_FF, D_MODEL), lambda i: (layer, 0, 0))],
        out_specs=pl.BlockSpec((tm, D_MODEL), lambda i: (i, 0)),
        out_shape=jax.ShapeDtypeStruct((t, D_MODEL), F32),
        name="ffn",
        compiler_params=_params(1),
    )(x, gain, w_in, w_in, w_out)


def _inproj_kernel(x_ref, gm_ref, cos_ref, sin_ref, w_ref, gains_ref, ones_ref,
                   qa1, ka1, va1, qa2, ka2, va2, qa3, ka3, va3, qb, kb, vb, qc, kc, vc, gl, cr,
                   hs_ref, hp_ref, tb_ref):
    tm = x_ref.shape[0]
    n_slab = D_MODEL // LANES
    h = _rms(x_ref[...], gm_ref[...])
    h_bf = h.astype(BF16)
    cos0 = cos_ref[...]
    sin0 = sin_ref[...]
    ones = ones_ref[...]
    gains = gains_ref[...]

    first_half = (lax.broadcasted_iota(jnp.int32, (1, LANES), 1) % HEAD_DIM) < (HEAD_DIM // 2)

    def norm_rope(y, gain, cos, sin):
        ms = jnp.dot((y * y).astype(BF16), ones, preferred_element_type=F32)
        yn = y * lax.rsqrt(ms + NORM_EPS) * gain
        partner = jnp.where(first_half, pltpu.roll(yn, LANES - HEAD_DIM // 2, 1), pltpu.roll(yn, HEAD_DIM // 2, 1))
        return yn * cos + partner * sin

    for s in range(n_slab):
        hs_ref[s] = h[:, s * LANES:(s + 1) * LANES]
    tb_ref[0] = cos0
    tb_ref[1] = sin0

    def class_major(d):
        if d == 1:
            return h_bf, cos0, sin0
        rows = tm // d
        for c in range(d):
            for s in range(n_slab):
                hp_ref[c * rows:(c + 1) * rows, s * LANES:(s + 1) * LANES] = (
                    hs_ref[s, pl.ds(c, rows, stride=d), :].astype(BF16))
        cos = jnp.concatenate([tb_ref[0, pl.ds(c, rows, stride=d), :] for c in range(d)], axis=0)
        sin = jnp.concatenate([tb_ref[1, pl.ds(c, rows, stride=d), :] for c in range(d)], axis=0)
        return hp_ref[...], cos, sin

    def store_classes(ref, col, val, d):
        rows = tm // d
        for c in range(d):
            ref[c, :, col:col + LANES] = val[c * rows:(c + 1) * rows, :]

    for gi, (qo, ko, vo) in enumerate(((qa1, ka1, va1), (qa2, ka2, va2), (qa3, ka3, va3))):
        d = A_CONFIGS[gi][1]
        hp, cos, sin = class_major(d)
        yq, yk, yv = [_dot_t(hp, w_ref[pl.ds(OFF_A + part * A_SEG + gi * A_GROUP_WIDTH, A_GROUP_WIDTH), :])
                      for part in range(3)]
        for j in range(A_GROUP_WIDTH // LANES):
            c0 = j * LANES
            store_classes(qo, c0, norm_rope(yq[:, c0:c0 + LANES], gains[0:1], cos, sin).astype(BF16), d)
            store_classes(ko, c0, norm_rope(yk[:, c0:c0 + LANES], gains[1:2], cos, sin).astype(BF16), d)
            store_classes(vo, c0, yv[:, c0:c0 + LANES].astype(BF16), d)

    y = _dot_t(h_bf, w_ref[OFF_B:OFF_B + B_SEG, :])
    for j in range(4):
        c0 = j * LANES
        qb[0, :, c0:c0 + LANES] = norm_rope(y[:, c0:c0 + LANES], gains[2:3], cos0, sin0).astype(BF16)
    for j in range(2):
        c0 = j * LANES
        kb[0, :, c0:c0 + LANES] = norm_rope(y[:, 512 + c0:512 + c0 + LANES], gains[3:4], cos0, sin0).astype(BF16)
        vb[0, :, c0:c0 + LANES] = y[:, 768 + c0:768 + c0 + LANES].astype(BF16)

    y = _dot_t(h_bf, w_ref[OFF_C:OFF_C + C_SEG, :])
    qc[...] = y[:, 0:256].astype(BF16)
    kc[...] = y[:, 256:512].astype(BF16)
    vc[...] = y[:, 512:1024].astype(BF16)
    gl[...] = y[:, 1024:1152].astype(BF16)
    cr[...] = y[:, 1152:1664].astype(BF16)


def _inproj(x, gain, cos, sin, w_packed, head_gains, ones_bd, layer, batch, seq):
    t = x.shape[0]
    tm = TOKEN_TILE
    tiles_per_seq = seq // tm

    def cm_spec(d, width):
        return pl.BlockSpec((None, d, tm // d, width),
                            lambda i: (i // tiles_per_seq, 0, i % tiles_per_seq, 0))

    def cm_shape(d, width):
        return jax.ShapeDtypeStruct((batch, d, seq // d, width), BF16)

    def tok_spec(width):
        return pl.BlockSpec((tm, width), lambda i: (i, 0))

    out_specs, out_shape = [], []
    for _, d in A_CONFIGS:
        for _ in range(3):
            out_specs.append(cm_spec(d, A_GROUP_WIDTH))
            out_shape.append(cm_shape(d, A_GROUP_WIDTH))
    for width in (512, 256, 256):
        out_specs.append(cm_spec(1, width))
        out_shape.append(cm_shape(1, width))
    for width in (256, 256, 512, 128, 512):
        out_specs.append(tok_spec(width))
        out_shape.append(jax.ShapeDtypeStruct((t, width), BF16))

    return pl.pallas_call(
        _inproj_kernel,
        grid=(t // tm,),
        in_specs=[tok_spec(D_MODEL),
                  pl.BlockSpec((None, 1, D_MODEL), lambda i: (layer, 0, 0)),
                  tok_spec(LANES), tok_spec(LANES),
                  _resident((None, W_PACKED, D_MODEL), lambda i: (layer, 0, 0)),
                  pl.BlockSpec((None, 4, LANES), lambda i: (layer, 0, 0)),
                  pl.BlockSpec((LANES, LANES), lambda i: (0, 0))],
        out_specs=out_specs,
        out_shape=out_shape,
        scratch_shapes=[pltpu.VMEM((D_MODEL // LANES, tm, LANES), F32),
                        pltpu.VMEM((tm, D_MODEL), BF16),
                        pltpu.VMEM((2, tm, LANES), F32)],
        name="inproj",
        compiler_params=_params(1),
    )(x, gain, cos, sin, w_packed, head_gains, ones_bd)


def _attn_kernel(d, nb, window, kv_div, use_sink, *refs):
    if use_sink:
        sink_ref, q_ref, kc_ref, kp_ref, vc_ref, vp_ref, o_ref, lse_ref, osc, lsc, s_scr, p_scr, st_scr = refs
    else:
        q_ref, kc_ref, kp_ref, vc_ref, vp_ref, o_ref, lse_ref, osc, lsc, s_scr, p_scr, st_scr = refs
    chunk = pl.program_id(1)
    lane = lax.broadcasted_iota(jnp.int32, (1, LANES), 1)
    head_of_lane = lane // HEAD_DIM
    key = lax.broadcasted_iota(jnp.int32, (2 * BLOCK, 2 * BLOCK), 0)
    col = lax.broadcasted_iota(jnp.int32, (2 * BLOCK, 2 * BLOCK), 1)
    dist = (col % BLOCK) + BLOCK - key
    band = (dist >= 0) & (dist <= window)
    col_row = lax.broadcasted_iota(jnp.int32, (1, 2 * BLOCK), 1)
    sub8 = lax.broadcasted_iota(jnp.int32, (8, LANES), 0)
    nt = (((1,), (1,)), ((), ()))
    tn = (((0,), (0,)), ((), ()))

    n_units = d * nb

    def locate(u):
        c = u // nb if d > 1 else 0
        j = u % nb if nb > 1 else 0
        if isinstance(j, int):
            return c, j, j * BLOCK, max(j - 1, 0) * BLOCK
        return c, j, pl.multiple_of(j * BLOCK, BLOCK), pl.multiple_of(jnp.maximum(j - 1, 0) * BLOCK, BLOCK)

    def window_rows(cur_ref, prev_ref, u, p, in_loop=False):
        c, j, row0, prow = locate(u)
        kcol = (p // kv_div) * LANES
        ksl = slice(kcol, kcol + LANES)
        cur = cur_ref[c, pl.ds(row0, BLOCK), ksl]
        if isinstance(j, int):
            first = j == 0
        elif d == 1 and in_loop:
            first = False
        else:
            first = None
        if first is None:
            prev = jnp.where(j == 0, prev_ref[c, :, ksl], cur_ref[c, pl.ds(prow, BLOCK), ksl])
        else:
            prev = prev_ref[c, :, ksl] if first else cur_ref[c, pl.ds(prow, BLOCK), ksl]
        return jnp.concatenate([prev, cur], axis=0)

    def scores(u, in_loop=False):
        c, j, row0, _ = locate(u)
        seq_start = jnp.logical_and(chunk == 0, j == 0)
        valid = band & jnp.logical_or(key >= BLOCK, jnp.logical_not(seq_start))
        for p in range(4):
            qp = q_ref[c, pl.ds(row0, BLOCK), p * LANES:(p + 1) * LANES]
            q2 = jnp.concatenate([jnp.where(head_of_lane == e, qp, jnp.zeros_like(qp)) for e in range(2)], axis=0)
            s = lax.dot_general(window_rows(kc_ref, kp_ref, u, p, in_loop), q2, nt,
                                preferred_element_type=F32)
            s = jnp.where(valid, s, -jnp.inf)
            s_scr[p] = s
            m = jnp.max(s, axis=0, keepdims=True)
            if use_sink:
                m = jnp.maximum(m, sink_row(p))
            st_scr[p, 0:1, :] = m

    def sink_row(p):
        return jnp.where(col_row < BLOCK, sink_ref[2 * p], sink_ref[2 * p + 1]) * LOG2E

    ones8 = jnp.ones((8, 2 * BLOCK), BF16)

    def softmax(u):
        for p in range(4):
            m = st_scr[p, 0:1, :]
            p_scr[p] = jnp.exp2(s_scr[p] - m).astype(BF16)
            st_scr[p, 2:3, :] = m

    def outputs(u, in_loop=False):
        c, _, row0, _ = locate(u)
        start = row0 * d + c
        rows = pl.ds(start, BLOCK, stride=d) if d > 1 else pl.ds(start, BLOCK)
        lse8 = jnp.zeros((8, LANES), F32)
        for p in range(4):
            pr, m = p_scr[p], st_scr[p, 2:3, :]
            ob = lax.dot_general(window_rows(vc_ref, vp_ref, u, p, in_loop), pr, tn,
                                 preferred_element_type=F32)
            den = jnp.dot(ones8, pr, preferred_element_type=F32)[0:1, :]
            if use_sink:
                den = den + jnp.exp2(sink_row(p) - m)
            inv = 1.0 / den
            o_t = jnp.concatenate([ob[0:HEAD_DIM, 0:BLOCK] * inv[:, 0:BLOCK],
                                   ob[HEAD_DIM:LANES, BLOCK:2 * BLOCK] * inv[:, BLOCK:2 * BLOCK]], axis=0)
            if d > 1:
                osc[p, rows, :] = o_t.T
            else:
                o_ref[rows, p * LANES:(p + 1) * LANES] = o_t.T.astype(BF16)
            l2 = m + jnp.log2(den)
            lse8 = jnp.where(sub8 == 2 * p, l2[:, 0:BLOCK], lse8)
            lse8 = jnp.where(sub8 == 2 * p + 1, l2[:, BLOCK:2 * BLOCK], lse8)
        lse_t = jnp.concatenate([lse8, jnp.zeros((LANES - 8, LANES), F32)], axis=0)
        if d > 1:
            lsc[rows, :] = lse_t.T
        else:
            lse_ref[rows, :] = lse_t.T

    scores(0)
    softmax(0)
    scores(1)
    outputs(0)
    softmax(1)
    scores(2)

    def body(u, carry):
        outputs(u - 2, True)
        softmax(u - 1)
        scores(u, True)
        return carry

    lax.fori_loop(3, n_units, body, 0)
    outputs(n_units - 2)
    softmax(n_units - 1)
    outputs(n_units - 1)
    if d > 1:
        for p in range(4):
            o_ref[:, p * LANES:(p + 1) * LANES] = osc[p].astype(BF16)
        lse_ref[...] = lsc[...]


def _attention(q, k, v, sinks, *, window, kv_div, batch, seq):
    d = q.shape[1]
    chunk = ATTN_CHUNK
    nb = chunk // (d * BLOCK)
    kw = k.shape[-1]
    rows = nb * BLOCK
    cur = lambda b, i: (b, 0, i, 0)
    prev = lambda b, i: (b, 0, jnp.maximum(i * nb - 1, 0), 0)
    in_specs = [pl.BlockSpec((None, d, rows, A_GROUP_WIDTH), cur),
                pl.BlockSpec((None, d, rows, kw), cur),
                pl.BlockSpec((None, d, BLOCK, kw), prev),
                pl.BlockSpec((None, d, rows, kw), cur),
                pl.BlockSpec((None, d, BLOCK, kw), prev)]
    args = [q, k, k, v, v]
    use_sink = sinks is not None
    if use_sink:
        in_specs = [pl.BlockSpec(memory_space=pltpu.SMEM)] + in_specs
        args = [sinks] + args
    return pl.pallas_call(
        functools.partial(_attn_kernel, d, nb, window, kv_div, use_sink),
        grid=(batch, seq // chunk),
        in_specs=in_specs,
        out_specs=[pl.BlockSpec((None, chunk, A_GROUP_WIDTH), lambda b, i: (b, i, 0)),
                   pl.BlockSpec((None, chunk, LANES), lambda b, i: (b, i, 0))],
        out_shape=[jax.ShapeDtypeStruct((batch, seq, A_GROUP_WIDTH), BF16),
                   jax.ShapeDtypeStruct((batch, seq, LANES), F32)],
        scratch_shapes=[pltpu.VMEM((4, chunk if d > 1 else 8, LANES), F32),
                        pltpu.VMEM((chunk if d > 1 else 8, LANES), F32),
                        pltpu.VMEM((4, 2 * BLOCK, 2 * BLOCK), F32),
                        pltpu.VMEM((4, 2 * BLOCK, 2 * BLOCK), BF16),
                        pltpu.VMEM((4, 8, 2 * BLOCK), F32)],
        name=f"attn_d{d}_w{window}",
        compiler_params=_params(2),
    )(*args)


def _gla_kernel(q_ref, k_ref, v_ref, gl_ref, cr_ref, gu_ref, gb_ref, on_ref, tri_ref, y_ref,
                st_ref, qd_s, qs_s, ks_s, kd_s, dec_s):
    tm = q_ref.shape[0]
    width = C_HEADS * C_DK
    n_chunk = GLA_SUB // C_CHUNK

    @pl.when(pl.program_id(1) == 0)
    def _():
        st_ref[...] = jnp.zeros_like(st_ref)

    tri = tri_ref[...]
    keep = tri > 0
    on_gain = on_ref[...]
    head_of_lane = lax.broadcasted_iota(jnp.int32, (1, LANES), 1) // C_DK
    own_block = ((lax.broadcasted_iota(jnp.int32, (2 * C_DV, LANES), 0) // C_DV)
                 == (lax.broadcasted_iota(jnp.int32, (2 * C_DV, LANES), 1) // C_DK))
    nt = (((1,), (1,)), ((), ()))
    tn = (((0,), (0,)), ((), ()))

    def chunked(a):
        return a.reshape(n_chunk, C_CHUNK, width)

    def sub_rows(sub):
        start = sub * GLA_SUB
        return pl.ds(start if isinstance(sub, int) else pl.multiple_of(start, GLA_SUB), GLA_SUB)

    def prepare(sub):
        rows = sub_rows(sub)
        x = jnp.dot(gl_ref[rows, :], gu_ref[...], preferred_element_type=F32) + gb_ref[...]
        lg = (jnp.minimum(x, 0.0) - jnp.log(1.0 + jnp.exp(-jnp.abs(x)))) * (1.0 / C_TAU)
        g1 = lg.astype(BF16)
        g2 = (lg - g1.astype(F32)).astype(BF16)
        bs = jnp.dot(tri, jnp.concatenate([g1, g2], axis=1), preferred_element_type=F32)
        b = chunked(bs[:, 0:width] + bs[:, width:2 * width])
        b_mid = b[:, C_CHUNK // 2 - 1:C_CHUNK // 2, :]
        b_last = b[:, C_CHUNK - 1:C_CHUNK, :]
        q = chunked(q_ref[rows, :].astype(F32) * (C_DK ** -0.5))
        k = chunked(k_ref[rows, :].astype(F32))
        qd_s[...] = (q * jnp.exp(b)).astype(BF16)
        qs_s[...] = (q * jnp.exp(b - b_mid)).astype(BF16)
        ks_s[...] = (k * jnp.exp(b_mid - b)).astype(BF16)
        kd_s[...] = (k * jnp.exp(b_last - b)).astype(BF16)
        dec_s[...] = jnp.broadcast_to(jnp.exp(b_last), dec_s.shape)

    def finish(sub):
        rows = sub_rows(sub)
        v = v_ref[rows, :]
        qd, qs, ks, kd = qd_s[...], qs_s[...], ks_s[...], kd_s[...]
        decay = dec_s[:, 0:1, :]
        for t in range(C_HEADS // 2):
            kcols = slice(t * LANES, (t + 1) * LANES)
            vcols = slice(2 * t * LANES, 2 * (t + 1) * LANES)
            qs_t = qs[:, :, kcols].reshape(GLA_SUB, LANES)
            q_st = jnp.concatenate([jnp.where(head_of_lane == e, qs_t, jnp.zeros((), BF16)) for e in range(2)], axis=0)
            att = lax.dot_general(q_st, ks[:, :, kcols].reshape(GLA_SUB, LANES), nt, preferred_element_type=F32)
            v_t = v[:, vcols]
            o = jnp.concatenate(
                [jnp.dot(jnp.where(keep, att[e * GLA_SUB:(e + 1) * GLA_SUB], 0.0).astype(BF16),
                         v_t[:, e * LANES:(e + 1) * LANES], preferred_element_type=F32) for e in range(2)], axis=1)
            st = st_ref[t]
            inter = []
            for c in range(n_chunk):
                inter.append(lax.dot_general(qd[c, :, kcols], st.astype(BF16), nt, preferred_element_type=F32))
                inc = lax.dot_general(v_t[c * C_CHUNK:(c + 1) * C_CHUNK, :], kd[c, :, kcols], tn,
                                      preferred_element_type=F32)
                st = jnp.where(own_block, st * decay[c, :, kcols] + inc, 0.0)
            st_ref[t] = st
            o = o + jnp.concatenate(inter, axis=0)
            for e in range(2):
                cols = slice((2 * t + e) * LANES, (2 * t + e + 1) * LANES)
                gate = cr_ref[rows, cols].astype(F32)
                y_ref[rows, cols] = (_rms(o[:, e * LANES:(e + 1) * LANES], on_gain)
                                     * (gate * _sigmoid(gate))).astype(BF16)

    n_sub = tm // GLA_SUB
    prepare(0)

    def body(sub, carry):
        finish(sub - 1)
        prepare(sub)
        return carry

    lax.fori_loop(1, n_sub, body, 0)
    finish(n_sub - 1)


def _gla(q, k, v, gl, cr, gate_up, gate_bias, out_norm, tri, layer, batch, seq):
    t = q.shape[0]
    tm = GLA_TILE
    per_seq = seq // tm
    width = C_HEADS * C_DV
    kwidth = C_HEADS * C_DK
    tok = lambda w: pl.BlockSpec((tm, w), lambda b, i: (b * per_seq + i, 0))
    return pl.pallas_call(
        _gla_kernel,
        grid=(batch, per_seq),
        in_specs=[tok(kwidth), tok(kwidth), tok(width), tok(LANES), tok(width),
                  pl.BlockSpec((None, LANES, kwidth), lambda b, i: (layer, 0, 0)),
                  pl.BlockSpec((None, 1, kwidth), lambda b, i: (layer, 0, 0)),
                  pl.BlockSpec((None, 1, C_DV), lambda b, i: (layer, 0, 0)),
                  pl.BlockSpec((GLA_SUB, GLA_SUB), lambda b, i: (0, 0))],
        out_specs=tok(width),
        out_shape=jax.ShapeDtypeStruct((t, width), BF16),
        scratch_shapes=[pltpu.VMEM((C_HEADS // 2, 2 * C_DV, LANES), F32)]
        + [pltpu.VMEM((GLA_SUB // C_CHUNK, C_CHUNK, kwidth), BF16)] * 4
        + [pltpu.VMEM((GLA_SUB // C_CHUNK, 8, kwidth), F32)],
        name="gla",
        compiler_params=_params(2),
    )(q, k, v, gl, cr, gate_up, gate_bias, out_norm, tri)


def _merge_kernel(x_ref, gm_ref, o1, o2, o3, l1, l2, l3, yb_ref, yc_ref, wg_ref, wb_ref, wo_ref, ex_ref, out_ref):
    x = x_ref[...]
    h = _rms(x, gm_ref[...]).astype(BF16)
    la, lb, lc = l1[...], l2[...], l3[...]
    top = jnp.maximum(jnp.maximum(la, lb), lc)
    ea, eb, ec = jnp.exp2(la - top), jnp.exp2(lb - top), jnp.exp2(lc - top)
    inv = 1.0 / (ea + eb + ec)
    expand = ex_ref[...]
    low_lanes = lax.broadcasted_iota(jnp.int32, (1, LANES), 1) < LANES // 2
    ya = None
    for e, o in ((ea, o1), (eb, o2), (ec, o3)):
        w = e * inv
        w_hi = w.astype(BF16)
        w_lo = pltpu.roll(w - w_hi.astype(F32), LANES // 2, 1).astype(BF16)
        wx = jnp.dot(jnp.where(low_lanes, w_hi, w_lo), expand, preferred_element_type=F32)
        term = wx * o[...].astype(F32)
        ya = term if ya is None else ya + term
    merged = None
    for i, y in enumerate((ya.astype(BF16), yb_ref[...], yc_ref[...])):
        gate = _sigmoid(_dot_t(h, wg_ref[i * D_MODEL:(i + 1) * D_MODEL, :]))
        term = gate * jnp.dot(y, wb_ref[i], preferred_element_type=F32)
        merged = term if merged is None else merged + term
    out_ref[...] = x + jnp.dot(merged.astype(BF16), wo_ref[...], preferred_element_type=F32)


def _merge(x, gain, oa, la, yb, yc, w_gate, w_branch, w_out, expand, layer):
    t = x.shape[0]
    tm = TOKEN_TILE
    tok = lambda w: pl.BlockSpec((tm, w), lambda i: (i, 0))
    bw = A_GROUP_WIDTH
    return pl.pallas_call(
        _merge_kernel,
        grid=(t // tm,),
        in_specs=[tok(D_MODEL),
                  pl.BlockSpec((None, 1, D_MODEL), lambda i: (layer, 0, 0)),
                  tok(bw), tok(bw), tok(bw), tok(LANES), tok(LANES), tok(LANES), tok(bw), tok(bw),
                  _resident((None, N_BRANCH * D_MODEL, D_MODEL), lambda i: (layer, 0, 0)),
                  _resident((None, N_BRANCH, bw, D_MODEL), lambda i: (layer, 0, 0, 0)),
                  _resident((None, D_MODEL, D_MODEL), lambda i: (layer, 0, 0)),
                  pl.BlockSpec((LANES, bw), lambda i: (0, 0))],
        out_specs=tok(D_MODEL),
        out_shape=jax.ShapeDtypeStruct((t, D_MODEL), F32),
        name="merge",
        compiler_params=_params(1),
    )(x, gain, *oa, *la, yb, yc, w_gate, w_branch, w_out, expand)


def _pack_w_in(w_in):
    w_t = jnp.swapaxes(w_in, 1, 2)
    n_layer = w_t.shape[0]
    edges = np.cumsum([0, 4608, 512, 256, 256, 256, 512, C_GATE_RANK, 512, N_BRANCH * D_MODEL])
    seg = [w_t[:, edges[i]:edges[i + 1], :] for i in range(9)]
    a, b_q, b_kv, c_q, c_k, c_v, c_glow, c_r, gates = seg
    b_kv = b_kv.reshape(n_layer, 2, 2, 1, HEAD_DIM, D_MODEL)
    b_kv = jnp.broadcast_to(b_kv, (n_layer, 2, 2, 2, HEAD_DIM, D_MODEL)).reshape(n_layer, 512, D_MODEL)
    c_glow = jnp.pad(c_glow, ((0, 0), (0, LANES - C_GATE_RANK), (0, 0)))
    packed = jnp.concatenate([a, b_q, b_kv, c_q, c_k, c_v, c_glow, c_r], axis=1)
    assert packed.shape == (n_layer, W_PACKED, D_MODEL)
    return packed.astype(BF16), gates.astype(BF16)


def kernel(x, positions, norm_ffn1, w_ffn1_in, w_ffn1_out, norm_mix, w_in, a_q_norm, a_k_norm, b_q_norm, b_k_norm,
           b_sinks, c_gate_up, c_gate_bias, c_out_norm, w_branch, w_out, norm_ffn2, w_ffn2_in, w_ffn2_out):
    batch, seq, _ = x.shape
    n_layer = w_in.shape[0]
    assert seq % ATTN_CHUNK == 0 and seq % GLA_TILE == 0 and x.shape[2] == D_MODEL

    w_packed, w_gate = _pack_w_in(w_in)
    w1_in, w1_out = w_ffn1_in.astype(BF16), w_ffn1_out.astype(BF16)
    w2_in, w2_out = w_ffn2_in.astype(BF16), w_ffn2_out.astype(BF16)
    wb, wo = w_branch.astype(BF16), w_out.astype(BF16)
    row = lambda g: g[:, None, :]
    head_gains = jnp.stack([jnp.tile(g * s, (1, LANES // HEAD_DIM)) for g, s in
                            ((a_q_norm, Q_SCALE), (a_k_norm, 1.0), (b_q_norm, Q_SCALE), (b_k_norm, 1.0))], axis=1)
    gate_up = jnp.pad(c_gate_up, ((0, 0), (0, LANES - C_GATE_RANK), (0, 0))).astype(BF16)
    gate_bias = row(c_gate_bias)

    lane_head = np.arange(LANES) // HEAD_DIM
    ones_bd = jnp.asarray((lane_head[:, None] == lane_head[None, :]) / HEAD_DIM, BF16)
    expand = jnp.asarray((np.arange(LANES) % (LANES // 2))[:, None] == (np.arange(A_GROUP_WIDTH) // HEAD_DIM)[None, :],
                         BF16)
    idx = np.arange(GLA_SUB)
    tri = jnp.asarray((idx[:, None] >= idx[None, :]) & (idx[:, None] // C_CHUNK == idx[None, :] // C_CHUNK), BF16)

    cos, sin = _rope_tables(positions)
    xt = x.reshape(batch * seq, D_MODEL)
    for l in range(n_layer):
        xt = _ffn(xt, row(norm_ffn1), w1_in, w1_out, l)
        (qa1, ka1, va1, qa2, ka2, va2, qa3, ka3, va3, qb, kb, vb, qc, kc, vc, gl, cr) = _inproj(
            xt, row(norm_mix), cos, sin, w_packed, head_gains, ones_bd, l, batch, seq)
        oa, la = [], []
        for (window, d), (q, k, v) in zip(A_CONFIGS, ((qa1, ka1, va1), (qa2, ka2, va2), (qa3, ka3, va3))):
            o, lse = _attention(q, k, v, None, window=window // d, kv_div=1, batch=batch, seq=seq)
            oa.append(o.reshape(batch * seq, A_GROUP_WIDTH))
            la.append(lse.reshape(batch * seq, LANES))
        yb, _ = _attention(qb, kb, vb, b_sinks[l], window=B_WINDOW - 1, kv_div=2, batch=batch, seq=seq)
        yb = yb.reshape(batch * seq, A_GROUP_WIDTH)
        yc = _gla(qc, kc, vc, gl, cr, gate_up, gate_bias, row(c_out_norm), tri, l, batch, seq)
        xt = _merge(xt, row(norm_mix), oa, la, yb, yc, w_gate, wb, wo, expand, l)
        xt = _ffn(xt, row(norm_ffn2), w2_in, w2_out, l)
    return xt.reshape(batch, seq, D_MODEL)
```

```python
import functools

import numpy as np
import jax
import jax.numpy as jnp
from jax import lax
from jax.experimental import pallas as pl
from jax.experimental.pallas import tpu as pltpu

F32 = jnp.float32
BF16 = jnp.bfloat16

D_MODEL = 1024
D_FF = 2816
HEAD_DIM = 64
NORM_EPS = 1e-6
ROPE_THETA = 10000.0
LOG2E = float(np.log2(np.e))
Q_SCALE = HEAD_DIM ** -0.5 * LOG2E
LANES = 128
ROPE_PACK_LANES = HEAD_DIM // 2
ROPE_PACK = LANES // ROPE_PACK_LANES
BLOCK = 128
A_CONFIGS = ((128, 1), (512, 4), (2048, 16))
A_GROUP_WIDTH = 512
B_WINDOW = 128
C_HEADS = 4
C_DK = 64
C_DV = 128
C_GATE_RANK = 16
C_TAU = 16.0
C_CHUNK = 128
GLA_SUB = 256
GLA_TILE = 2048
N_BRANCH = 3

A_SEG = 3 * A_GROUP_WIDTH
OFF_A = 0
OFF_B = 3 * A_SEG
B_SEG = 1024
OFF_C = OFF_B + B_SEG
C_SEG = 1664
W_PACKED = OFF_C + C_SEG

TOKEN_TILE = 512
ATTN_CHUNK = 2048
MXU_COLS = 256
FFN_CHUNKS = ((0, 1536), (1536, D_FF))
assert all((hi - lo) % MXU_COLS == 0 for lo, hi in FFN_CHUNKS)
VMEM_LIMIT = 56 * 2**20


def _params(n_axes):
    return pltpu.CompilerParams(dimension_semantics=("arbitrary",) * n_axes, vmem_limit_bytes=VMEM_LIMIT)


def _rms(x, gain):
    return x * lax.rsqrt(jnp.mean(x * x, axis=-1, keepdims=True) + NORM_EPS) * gain


def _dot_t(a, w_t):
    return lax.dot_general(a, w_t, (((1,), (1,)), ((), ())), preferred_element_type=F32)


def _sigmoid(x):
    return 0.5 * jnp.tanh(0.5 * x) + 0.5


def _resident(block, index_map):
    return pl.BlockSpec(block, index_map, pipeline_mode=pl.Buffered(1))


def _rope_kernel(pos_ref, inv_ref, sgn_ref, cos_ref, sin_ref):
    rows = pos_ref.shape[0]
    ang = pos_ref[...].astype(F32) * inv_ref[...]
    tables = (jnp.cos(ang), jnp.sin(ang))
    group = lax.broadcasted_iota(jnp.int32, (1, LANES), 1) // ROPE_PACK_LANES
    sgn = sgn_ref[...]
    for j in range(ROPE_PACK):
        for tab, ref, scale in zip(tables, (cos_ref, sin_ref), (None, sgn)):
            own = jnp.where(group == j, tab, 0.0)
            full = own
            for shift in range(1, ROPE_PACK):
                full = full + pltpu.roll(own, shift * ROPE_PACK_LANES, 1)
            ref[pl.ds(j, rows, stride=ROPE_PACK), :] = full if scale is None else full * scale


def _rope_tables(positions):
    t = positions.size
    inv = ROPE_THETA ** (-jnp.arange(0, HEAD_DIM, 2, dtype=F32) / HEAD_DIM)
    inv = jnp.tile(inv, LANES // (HEAD_DIM // 2))[None, :]
    sgn = jnp.tile(jnp.concatenate([-jnp.ones(HEAD_DIM // 2, F32), jnp.ones(HEAD_DIM // 2, F32)]),
                   LANES // HEAD_DIM)[None, :]
    tm = TOKEN_TILE
    tab = jax.ShapeDtypeStruct((t, LANES), F32)
    return pl.pallas_call(
        _rope_kernel,
        grid=(t // tm,),
        in_specs=[pl.BlockSpec((tm // ROPE_PACK, LANES), lambda i: (i, 0)),
                  pl.BlockSpec((1, LANES), lambda i: (0, 0)),
                  pl.BlockSpec((1, LANES), lambda i: (0, 0))],
        out_specs=[pl.BlockSpec((tm, LANES), lambda i: (i, 0))] * 2,
        out_shape=[tab, tab],
        name="rope",
        compiler_params=_params(1),
    )(jnp.repeat(positions.reshape(t // ROPE_PACK, ROPE_PACK), ROPE_PACK_LANES, axis=1), inv, sgn)


def _ffn_kernel(x_ref, g_ref, wg_ref, wu_ref, wo_ref, o_ref):
    x = x_ref[...]
    h = _rms(x, g_ref[...]).astype(BF16)
    acc = None
    for lo, hi in FFN_CHUNKS:
        sl = slice(lo, hi)
        g = jnp.dot(h, wg_ref[:, sl], preferred_element_type=F32)
        u = jnp.dot(h, wu_ref[:, sl], preferred_element_type=F32)
        a = (g * _sigmoid(g) * u).astype(BF16)
        part = jnp.dot(a, wo_ref[sl, :], preferred_element_type=F32)
        acc = part if acc is None else acc + part
    o_ref[...] = x + 0.5 * acc


def _ffn(x, gain, w_in, w_out, layer):
    t = x.shape[0]
    tm = TOKEN_TILE
    return pl.pallas_call(
        _ffn_kernel,
        grid=(t // tm,),
        in_specs=[pl.BlockSpec((tm, D_MODEL), lambda i: (i, 0)),
                  pl.BlockSpec((None, 1, D_MODEL), lambda i: (layer, 0, 0)),
                  _resident((None, D_MODEL, D_FF), lambda i: (layer, 0, 0)),
                  _resident((None, D_MODEL, D_FF), lambda i: (layer, 0, 1)),
                  _resident((None, D_FF, D_MODEL), lambda i: (layer, 0, 0))],
        out_specs=pl.BlockSpec((tm, D_MODEL), lambda i: (i, 0)),
        out_shape=jax.ShapeDtypeStruct((t, D_MODEL), F32),
        name="ffn",
        compiler_params=_params(1),
    )(x, gain, w_in, w_in, w_out)


def _inproj_kernel(x_ref, gm_ref, cos_ref, sin_ref, w_ref, gains_ref, ones_ref,
                   qa1, ka1, va1, qa2, ka2, va2, qa3, ka3, va3, qb, kb, vb, qc, kc, vc, gl, cr,
                   hs_ref, hp_ref, tb_ref):
    tm = x_ref.shape[0]
    n_slab = D_MODEL // LANES
    h = _rms(x_ref[...], gm_ref[...])
    h_bf = h.astype(BF16)
    cos0 = cos_ref[...]
    sin0 = sin_ref[...]
    ones = ones_ref[...]
    gains = gains_ref[...]

    first_half = (lax.broadcasted_iota(jnp.int32, (1, LANES), 1) % HEAD_DIM) < (HEAD_DIM // 2)

    def norm_rope(y, gain, cos, sin):
        ms = jnp.dot((y * y).astype(BF16), ones, preferred_element_type=F32)
        yn = y * lax.rsqrt(ms + NORM_EPS) * gain
        partner = jnp.where(first_half, pltpu.roll(yn, LANES - HEAD_DIM // 2, 1), pltpu.roll(yn, HEAD_DIM // 2, 1))
        return yn * cos + partner * sin

    for s in range(n_slab):
        hs_ref[s] = h[:, s * LANES:(s + 1) * LANES]
    tb_ref[0] = cos0
    tb_ref[1] = sin0

    def class_major(d):
        if d == 1:
            return h_bf, cos0, sin0
        rows = tm // d
        for c in range(d):
            for s in range(n_slab):
                hp_ref[c * rows:(c + 1) * rows, s * LANES:(s + 1) * LANES] = (
                    hs_ref[s, pl.ds(c, rows, stride=d), :].astype(BF16))
        cos = jnp.concatenate([tb_ref[0, pl.ds(c, rows, stride=d), :] for c in range(d)], axis=0)
        sin = jnp.concatenate([tb_ref[1, pl.ds(c, rows, stride=d), :] for c in range(d)], axis=0)
        return hp_ref[...], cos, sin

    def store_classes(ref, col, val, d):
        rows = tm // d
        for c in range(d):
            ref[c, :, col:col + LANES] = val[c * rows:(c + 1) * rows, :]

    for gi, (qo, ko, vo) in enumerate(((qa1, ka1, va1), (qa2, ka2, va2), (qa3, ka3, va3))):
        d = A_CONFIGS[gi][1]
        hp, cos, sin = class_major(d)
        yq, yk, yv = [_dot_t(hp, w_ref[pl.ds(OFF_A + part * A_SEG + gi * A_GROUP_WIDTH, A_GROUP_WIDTH), :])
                      for part in range(3)]
        for j in range(A_GROUP_WIDTH // LANES):
            c0 = j * LANES
            store_classes(qo, c0, norm_rope(yq[:, c0:c0 + LANES], gains[0:1], cos, sin).astype(BF16), d)
            store_classes(ko, c0, norm_rope(yk[:, c0:c0 + LANES], gains[1:2], cos, sin).astype(BF16), d)
            store_classes(vo, c0, yv[:, c0:c0 + LANES].astype(BF16), d)

    y = _dot_t(h_bf, w_ref[OFF_B:OFF_B + B_SEG, :])
    for j in range(4):
        c0 = j * LANES
        qb[0, :, c0:c0 + LANES] = norm_rope(y[:, c0:c0 + LANES], gains[2:3], cos0, sin0).astype(BF16)
    for j in range(2):
        c0 = j * LANES
        kb[0, :, c0:c0 + LANES] = norm_rope(y[:, 512 + c0:512 + c0 + LANES], gains[3:4], cos0, sin0).astype(BF16)
        vb[0, :, c0:c0 + LANES] = y[:, 768 + c0:768 + c0 + LANES].astype(BF16)

    y = _dot_t(h_bf, w_ref[OFF_C:OFF_C + C_SEG, :])
    qc[...] = y[:, 0:256].astype(BF16)
    kc[...] = y[:, 256:512].astype(BF16)
    vc[...] = y[:, 512:1024].astype(BF16)
    gl[...] = y[:, 1024:1152].astype(BF16)
    cr[...] = y[:, 1152:1664].astype(BF16)


def _inproj(x, gain, cos, sin, w_packed, head_gains, ones_bd, layer, batch, seq):
    t = x.shape[0]
    tm = TOKEN_TILE
    tiles_per_seq = seq // tm

    def cm_spec(d, width):
        return pl.BlockSpec((None, d, tm // d, width),
                            lambda i: (i // tiles_per_seq, 0, i % tiles_per_seq, 0))

    def cm_shape(d, width):
        return jax.ShapeDtypeStruct((batch, d, seq // d, width), BF16)

    def tok_spec(width):
        return pl.BlockSpec((tm, width), lambda i: (i, 0))

    out_specs, out_shape = [], []
    for _, d in A_CONFIGS:
        for _ in range(3):
            out_specs.append(cm_spec(d, A_GROUP_WIDTH))
            out_shape.append(cm_shape(d, A_GROUP_WIDTH))
    for width in (512, 256, 256):
        out_specs.append(cm_spec(1, width))
        out_shape.append(cm_shape(1, width))
    for width in (256, 256, 512, 128, 512):
        out_specs.append(tok_spec(width))
        out_shape.append(jax.ShapeDtypeStruct((t, width), BF16))

    return pl.pallas_call(
        _inproj_kernel,
        grid=(t // tm,),
        in_specs=[tok_spec(D_MODEL),
                  pl.BlockSpec((None, 1, D_MODEL), lambda i: (layer, 0, 0)),
                  tok_spec(LANES), tok_spec(LANES),
                  _resident((None, W_PACKED, D_MODEL), lambda i: (layer, 0, 0)),
                  pl.BlockSpec((None, 4, LANES), lambda i: (layer, 0, 0)),
                  pl.BlockSpec((LANES, LANES), lambda i: (0, 0))],
        out_specs=out_specs,
        out_shape=out_shape,
        scratch_shapes=[pltpu.VMEM((D_MODEL // LANES, tm, LANES), F32),
                        pltpu.VMEM((tm, D_MODEL), BF16),
                        pltpu.VMEM((2, tm, LANES), F32)],
        name="inproj",
        compiler_params=_params(1),
    )(x, gain, cos, sin, w_packed, head_gains, ones_bd)


def _attn_kernel(d, nb, window, kv_div, use_sink, *refs):
    if use_sink:
        sink_ref, q_ref, kc_ref, kp_ref, vc_ref, vp_ref, o_ref, lse_ref, osc, lsc, s_scr, p_scr, st_scr = refs
    else:
        q_ref, kc_ref, kp_ref, vc_ref, vp_ref, o_ref, lse_ref, osc, lsc, s_scr, p_scr, st_scr = refs
    chunk = pl.program_id(1)
    lane = lax.broadcasted_iota(jnp.int32, (1, LANES), 1)
    head_of_lane = lane // HEAD_DIM
    key = lax.broadcasted_iota(jnp.int32, (2 * BLOCK, 2 * BLOCK), 0)
    col = lax.broadcasted_iota(jnp.int32, (2 * BLOCK, 2 * BLOCK), 1)
    dist = (col % BLOCK) + BLOCK - key
    band = (dist >= 0) & (dist <= window)
    col_row = lax.broadcasted_iota(jnp.int32, (1, 2 * BLOCK), 1)
    sub8 = lax.broadcasted_iota(jnp.int32, (8, LANES), 0)
    nt = (((1,), (1,)), ((), ()))
    tn = (((0,), (0,)), ((), ()))

    n_units = d * nb

    def locate(u):
        c = u // nb if d > 1 else 0
        j = u % nb if nb > 1 else 0
        if isinstance(j, int):
            return c, j, j * BLOCK, max(j - 1, 0) * BLOCK
        return c, j, pl.multiple_of(j * BLOCK, BLOCK), pl.multiple_of(jnp.maximum(j - 1, 0) * BLOCK, BLOCK)

    def window_rows(cur_ref, prev_ref, u, p, in_loop=False):
        c, j, row0, prow = locate(u)
        kcol = (p // kv_div) * LANES
        ksl = slice(kcol, kcol + LANES)
        cur = cur_ref[c, pl.ds(row0, BLOCK), ksl]
        if isinstance(j, int):
            first = j == 0
        elif d == 1 and in_loop:
            first = False
        else:
            first = None
        if first is None:
            prev = jnp.where(j == 0, prev_ref[c, :, ksl], cur_ref[c, pl.ds(prow, BLOCK), ksl])
        else:
            prev = prev_ref[c, :, ksl] if first else cur_ref[c, pl.ds(prow, BLOCK), ksl]
        return jnp.concatenate([prev, cur], axis=0)

    def scores(u, in_loop=False):
        c, j, row0, _ = locate(u)
        seq_start = jnp.logical_and(chunk == 0, j == 0)
        valid = band & jnp.logical_or(key >= BLOCK, jnp.logical_not(seq_start))
        for p in range(4):
            qp = q_ref[c, pl.ds(row0, BLOCK), p * LANES:(p + 1) * LANES]
            q2 = jnp.concatenate([jnp.where(head_of_lane == e, qp, jnp.zeros_like(qp)) for e in range(2)], axis=0)
            s = lax.dot_general(window_rows(kc_ref, kp_ref, u, p, in_loop), q2, nt,
                                preferred_element_type=F32)
            s = jnp.where(valid, s, -jnp.inf)
            s_scr[p] = s
            m = jnp.max(s, axis=0, keepdims=True)
            if use_sink:
                m = jnp.maximum(m, sink_row(p))
            st_scr[p, 0:1, :] = m

    def sink_row(p):
        return jnp.where(col_row < BLOCK, sink_ref[2 * p], sink_ref[2 * p + 1]) * LOG2E

    ones8 = jnp.ones((8, 2 * BLOCK), BF16)

    def softmax(u):
        for p in range(4):
            m = st_scr[p, 0:1, :]
            p_scr[p] = jnp.exp2(s_scr[p] - m).astype(BF16)
            st_scr[p, 2:3, :] = m

    def outputs(u, in_loop=False):
        c, _, row0, _ = locate(u)
        start = row0 * d + c
        rows = pl.ds(start, BLOCK, stride=d) if d > 1 else pl.ds(start, BLOCK)
        lse8 = jnp.zeros((8, LANES), F32)
        for p in range(4):
            pr, m = p_scr[p], st_scr[p, 2:3, :]
            ob = lax.dot_general(window_rows(vc_ref, vp_ref, u, p, in_loop), pr, tn,
                                 preferred_element_type=F32)
            den = jnp.dot(ones8, pr, preferred_element_type=F32)[0:1, :]
            if use_sink:
                den = den + jnp.exp2(sink_row(p) - m)
            inv = 1.0 / den
            o_t = jnp.concatenate([ob[0:HEAD_DIM, 0:BLOCK] * inv[:, 0:BLOCK],
                                   ob[HEAD_DIM:LANES, BLOCK:2 * BLOCK] * inv[:, BLOCK:2 * BLOCK]], axis=0)
            if d > 1:
                osc[p, rows, :] = o_t.T
            else:
                o_ref[rows, p * LANES:(p + 1) * LANES] = o_t.T.astype(BF16)
            l2 = m + jnp.log2(den)
            lse8 = jnp.where(sub8 == 2 * p, l2[:, 0:BLOCK], lse8)
            lse8 = jnp.where(sub8 == 2 * p + 1, l2[:, BLOCK:2 * BLOCK], lse8)
        lse_t = jnp.concatenate([lse8, jnp.zeros((LANES - 8, LANES), F32)], axis=0)
        if d > 1:
            lsc[rows, :] = lse_t.T
        else:
            lse_ref[rows, :] = lse_t.T

    scores(0)
    softmax(0)
    scores(1)
    outputs(0)
    softmax(1)
    scores(2)

    def body(u, carry):
        outputs(u - 2, True)
        softmax(u - 1)
        scores(u, True)
        return carry

    lax.fori_loop(3, n_units, body, 0)
    outputs(n_units - 2)
    softmax(n_units - 1)
    outputs(n_units - 1)
    if d > 1:
        for p in range(4):
            o_ref[:, p * LANES:(p + 1) * LANES] = osc[p].astype(BF16)
        lse_ref[...] = lsc[...]


def _attention(q, k, v, sinks, *, window, kv_div, batch, seq):
    d = q.shape[1]
    chunk = ATTN_CHUNK
    nb = chunk // (d * BLOCK)
    kw = k.shape[-1]
    rows = nb * BLOCK
    cur = lambda b, i: (b, 0, i, 0)
    prev = lambda b, i: (b, 0, jnp.maximum(i * nb - 1, 0), 0)
    in_specs = [pl.BlockSpec((None, d, rows, A_GROUP_WIDTH), cur),
                pl.BlockSpec((None, d, rows, kw), cur),
                pl.BlockSpec((None, d, BLOCK, kw), prev),
                pl.BlockSpec((None, d, rows, kw), cur),
                pl.BlockSpec((None, d, BLOCK, kw), prev)]
    args = [q, k, k, v, v]
    use_sink = sinks is not None
    if use_sink:
        in_specs = [pl.BlockSpec(memory_space=pltpu.SMEM)] + in_specs
        args = [sinks] + args
    return pl.pallas_call(
        functools.partial(_attn_kernel, d, nb, window, kv_div, use_sink),
        grid=(batch, seq // chunk),
        in_specs=in_specs,
        out_specs=[pl.BlockSpec((None, chunk, A_GROUP_WIDTH), lambda b, i: (b, i, 0)),
                   pl.BlockSpec((None, chunk, LANES), lambda b, i: (b, i, 0))],
        out_shape=[jax.ShapeDtypeStruct((batch, seq, A_GROUP_WIDTH), BF16),
                   jax.ShapeDtypeStruct((batch, seq, LANES), F32)],
        scratch_shapes=[pltpu.VMEM((4, chunk if d > 1 else 8, LANES), F32),
                        pltpu.VMEM((chunk if d > 1 else 8, LANES), F32),
                        pltpu.VMEM((4, 2 * BLOCK, 2 * BLOCK), F32),
                        pltpu.VMEM((4, 2 * BLOCK, 2 * BLOCK), BF16),
                        pltpu.VMEM((4, 8, 2 * BLOCK), F32)],
        name=f"attn_d{d}_w{window}",
        compiler_params=_params(2),
    )(*args)


def _gla_kernel(q_ref, k_ref, v_ref, gl_ref, cr_ref, gu_ref, gb_ref, on_ref, tri_ref, y_ref,
                st_ref, qd_s, qs_s, ks_s, kd_s, dec_s):
    tm = q_ref.shape[0]
    width = C_HEADS * C_DK
    n_chunk = GLA_SUB // C_CHUNK

    @pl.when(pl.program_id(1) == 0)
    def _():
        st_ref[...] = jnp.zeros_like(st_ref)

    tri = tri_ref[...]
    keep = tri > 0
    on_gain = on_ref[...]
    head_of_lane = lax.broadcasted_iota(jnp.int32, (1, LANES), 1) // C_DK
    own_block = ((lax.broadcasted_iota(jnp.int32, (2 * C_DV, LANES), 0) // C_DV)
                 == (lax.broadcasted_iota(jnp.int32, (2 * C_DV, LANES), 1) // C_DK))
    nt = (((1,), (1,)), ((), ()))
    tn = (((0,), (0,)), ((), ()))

    def chunked(a):
        return a.reshape(n_chunk, C_CHUNK, width)

    def sub_rows(sub):
        start = sub * GLA_SUB
        return pl.ds(start if isinstance(sub, int) else pl.multiple_of(start, GLA_SUB), GLA_SUB)

    def prepare(sub):
        rows = sub_rows(sub)
        x = jnp.dot(gl_ref[rows, :], gu_ref[...], preferred_element_type=F32) + gb_ref[...]
        lg = (jnp.minimum(x, 0.0) - jnp.log(1.0 + jnp.exp(-jnp.abs(x)))) * (1.0 / C_TAU)
        g1 = lg.astype(BF16)
        g2 = (lg - g1.astype(F32)).astype(BF16)
        bs = jnp.dot(tri, jnp.concatenate([g1, g2], axis=1), preferred_element_type=F32)
        b = chunked(bs[:, 0:width] + bs[:, width:2 * width])
        b_mid = b[:, C_CHUNK // 2 - 1:C_CHUNK // 2, :]
        b_last = b[:, C_CHUNK - 1:C_CHUNK, :]
        q = chunked(q_ref[rows, :].astype(F32) * (C_DK ** -0.5))
        k = chunked(k_ref[rows, :].astype(F32))
        qd_s[...] = (q * jnp.exp(b)).astype(BF16)
        qs_s[...] = (q * jnp.exp(b - b_mid)).astype(BF16)
        ks_s[...] = (k * jnp.exp(b_mid - b)).astype(BF16)
        kd_s[...] = (k * jnp.exp(b_last - b)).astype(BF16)
        dec_s[...] = jnp.broadcast_to(jnp.exp(b_last), dec_s.shape)

    def finish(sub):
        rows = sub_rows(sub)
        v = v_ref[rows, :]
        qd, qs, ks, kd = qd_s[...], qs_s[...], ks_s[...], kd_s[...]
        decay = dec_s[:, 0:1, :]
        for t in range(C_HEADS // 2):
            kcols = slice(t * LANES, (t + 1) * LANES)
            vcols = slice(2 * t * LANES, 2 * (t + 1) * LANES)
            qs_t = qs[:, :, kcols].reshape(GLA_SUB, LANES)
            q_st = jnp.concatenate([jnp.where(head_of_lane == e, qs_t, jnp.zeros((), BF16)) for e in range(2)], axis=0)
            att = lax.dot_general(q_st, ks[:, :, kcols].reshape(GLA_SUB, LANES), nt, preferred_element_type=F32)
            v_t = v[:, vcols]
            o = jnp.concatenate(
                [jnp.dot(jnp.where(keep, att[e * GLA_SUB:(e + 1) * GLA_SUB], 0.0).astype(BF16),
                         v_t[:, e * LANES:(e + 1) * LANES], preferred_element_type=F32) for e in range(2)], axis=1)
            st = st_ref[t]
            inter = []
            for c in range(n_chunk):
                inter.append(lax.dot_general(qd[c, :, kcols], st.astype(BF16), nt, preferred_element_type=F32))
                inc = lax.dot_general(v_t[c * C_CHUNK:(c + 1) * C_CHUNK, :], kd[c, :, kcols], tn,
                                      preferred_element_type=F32)
                st = jnp.where(own_block, st * decay[c, :, kcols] + inc, 0.0)
            st_ref[t] = st
            o = o + jnp.concatenate(inter, axis=0)
            for e in range(2):
                cols = slice((2 * t + e) * LANES, (2 * t + e + 1) * LANES)
                gate = cr_ref[rows, cols].astype(F32)
                y_ref[rows, cols] = (_rms(o[:, e * LANES:(e + 1) * LANES], on_gain)
                                     * (gate * _sigmoid(gate))).astype(BF16)

    n_sub = tm // GLA_SUB
    prepare(0)

    def body(sub, carry):
        finish(sub - 1)
        prepare(sub)
        return carry

    lax.fori_loop(1, n_sub, body, 0)
    finish(n_sub - 1)


def _gla(q, k, v, gl, cr, gate_up, gate_bias, out_norm, tri, layer, batch, seq):
    t = q.shape[0]
    tm = GLA_TILE
    per_seq = seq // tm
    width = C_HEADS * C_DV
    kwidth = C_HEADS * C_DK
    tok = lambda w: pl.BlockSpec((tm, w), lambda b, i: (b * per_seq + i, 0))
    return pl.pallas_call(
        _gla_kernel,
        grid=(batch, per_seq),
        in_specs=[tok(kwidth), tok(kwidth), tok(width), tok(LANES), tok(width),
                  pl.BlockSpec((None, LANES, kwidth), lambda b, i: (layer, 0, 0)),
                  pl.BlockSpec((None, 1, kwidth), lambda b, i: (layer, 0, 0)),
                  pl.BlockSpec((None, 1, C_DV), lambda b, i: (layer, 0, 0)),
                  pl.BlockSpec((GLA_SUB, GLA_SUB), lambda b, i: (0, 0))],
        out_specs=tok(width),
        out_shape=jax.ShapeDtypeStruct((t, width), BF16),
        scratch_shapes=[pltpu.VMEM((C_HEADS // 2, 2 * C_DV, LANES), F32)]
        + [pltpu.VMEM((GLA_SUB // C_CHUNK, C_CHUNK, kwidth), BF16)] * 4
        + [pltpu.VMEM((GLA_SUB // C_CHUNK, 8, kwidth), F32)],
        name="gla",
        compiler_params=_params(2),
    )(q, k, v, gl, cr, gate_up, gate_bias, out_norm, tri)


def _merge_kernel(x_ref, gm_ref, o1, o2, o3, l1, l2, l3, yb_ref, yc_ref, wg_ref, wb_ref, wo_ref, ex_ref, out_ref):
    x = x_ref[...]
    h = _rms(x, gm_ref[...]).astype(BF16)
    la, lb, lc = l1[...], l2[...], l3[...]
    top = jnp.maximum(jnp.maximum(la, lb), lc)
    ea, eb, ec = jnp.exp2(la - top), jnp.exp2(lb - top), jnp.exp2(lc - top)
    inv = 1.0 / (ea + eb + ec)
    expand = ex_ref[...]
    low_lanes = lax.broadcasted_iota(jnp.int32, (1, LANES), 1) < LANES // 2
    ya = None
    for e, o in ((ea, o1), (eb, o2), (ec, o3)):
        w = e * inv
        w_hi = w.astype(BF16)
        w_lo = pltpu.roll(w - w_hi.astype(F32), LANES // 2, 1).astype(BF16)
        wx = jnp.dot(jnp.where(low_lanes, w_hi, w_lo), expand, preferred_element_type=F32)
        term = wx * o[...].astype(F32)
        ya = term if ya is None else ya + term
    merged = None
    for i, y in enumerate((ya.astype(BF16), yb_ref[...], yc_ref[...])):
        gate = _sigmoid(_dot_t(h, wg_ref[i * D_MODEL:(i + 1) * D_MODEL, :]))
        term = gate * jnp.dot(y, wb_ref[i], preferred_element_type=F32)
        merged = term if merged is None else merged + term
    out_ref[...] = x + jnp.dot(merged.astype(BF16), wo_ref[...], preferred_element_type=F32)


def _merge(x, gain, oa, la, yb, yc, w_gate, w_branch, w_out, expand, layer):
    t = x.shape[0]
    tm = TOKEN_TILE
    tok = lambda w: pl.BlockSpec((tm, w), lambda i: (i, 0))
    bw = A_GROUP_WIDTH
    return pl.pallas_call(
        _merge_kernel,
        grid=(t // tm,),
        in_specs=[tok(D_MODEL),
                  pl.BlockSpec((None, 1, D_MODEL), lambda i: (layer, 0, 0)),
                  tok(bw), tok(bw), tok(bw), tok(LANES), tok(LANES), tok(LANES), tok(bw), tok(bw),
                  _resident((None, N_BRANCH * D_MODEL, D_MODEL), lambda i: (layer, 0, 0)),
                  _resident((None, N_BRANCH, bw, D_MODEL), lambda i: (layer, 0, 0, 0)),
                  _resident((None, D_MODEL, D_MODEL), lambda i: (layer, 0, 0)),
                  pl.BlockSpec((LANES, bw), lambda i: (0, 0))],
        out_specs=tok(D_MODEL),
        out_shape=jax.ShapeDtypeStruct((t, D_MODEL), F32),
        name="merge",
        compiler_params=_params(1),
    )(x, gain, *oa, *la, yb, yc, w_gate, w_branch, w_out, expand)


def _pack_w_in(w_in):
    w_t = jnp.swapaxes(w_in, 1, 2)
    n_layer = w_t.shape[0]
    edges = np.cumsum([0, 4608, 512, 256, 256, 256, 512, C_GATE_RANK, 512, N_BRANCH * D_MODEL])
    seg = [w_t[:, edges[i]:edges[i + 1], :] for i in range(9)]
    a, b_q, b_kv, c_q, c_k, c_v, c_glow, c_r, gates = seg
    b_kv = b_kv.reshape(n_layer, 2, 2, 1, HEAD_DIM, D_MODEL)
    b_kv = jnp.broadcast_to(b_kv, (n_layer, 2, 2, 2, HEAD_DIM, D_MODEL)).reshape(n_layer, 512, D_MODEL)
    c_glow = jnp.pad(c_glow, ((0, 0), (0, LANES - C_GATE_RANK), (0, 0)))
    packed = jnp.concatenate([a, b_q, b_kv, c_q, c_k, c_v, c_glow, c_r], axis=1)
    assert packed.shape == (n_layer, W_PACKED, D_MODEL)
    return packed.astype(BF16), gates.astype(BF16)


def kernel(x, positions, norm_ffn1, w_ffn1_in, w_ffn1_out, norm_mix, w_in, a_q_norm, a_k_norm, b_q_norm, b_k_norm,
           b_sinks, c_gate_up, c_gate_bias, c_out_norm, w_branch, w_out, norm_ffn2, w_ffn2_in, w_ffn2_out):
    batch, seq, _ = x.shape
    n_layer = w_in.shape[0]
    assert seq % ATTN_CHUNK == 0 and seq % GLA_TILE == 0 and x.shape[2] == D_MODEL

    w_packed, w_gate = _pack_w_in(w_in)
    w1_in, w1_out = w_ffn1_in.astype(BF16), w_ffn1_out.astype(BF16)
    w2_in, w2_out = w_ffn2_in.astype(BF16), w_ffn2_out.astype(BF16)
    wb, wo = w_branch.astype(BF16), w_out.astype(BF16)
    row = lambda g: g[:, None, :]
    head_gains = jnp.stack([jnp.tile(g * s, (1, LANES // HEAD_DIM)) for g, s in
                            ((a_q_norm, Q_SCALE), (a_k_norm, 1.0), (b_q_norm, Q_SCALE), (b_k_norm, 1.0))], axis=1)
    gate_up = jnp.pad(c_gate_up, ((0, 0), (0, LANES - C_GATE_RANK), (0, 0))).astype(BF16)
    gate_bias = row(c_gate_bias)

    lane_head = np.arange(LANES) // HEAD_DIM
    ones_bd = jnp.asarray((lane_head[:, None] == lane_head[None, :]) / HEAD_DIM, BF16)
    expand = jnp.asarray((np.arange(LANES) % (LANES // 2))[:, None] == (np.arange(A_GROUP_WIDTH) // HEAD_DIM)[None, :],
                         BF16)
    idx = np.arange(GLA_SUB)
    tri = jnp.asarray((idx[:, None] >= idx[None, :]) & (idx[:, None] // C_CHUNK == idx[None, :] // C_CHUNK), BF16)

    cos, sin = _rope_tables(positions)
    xt = x.reshape(batch * seq, D_MODEL)
    for l in range(n_layer):
        xt = _ffn(xt, row(norm_ffn1), w1_in, w1_out, l)
        (qa1, ka1, va1, qa2, ka2, va2, qa3, ka3, va3, qb, kb, vb, qc, kc, vc, gl, cr) = _inproj(
            xt, row(norm_mix), cos, sin, w_packed, head_gains, ones_bd, l, batch, seq)
        oa, la = [], []
        for (window, d), (q, k, v) in zip(A_CONFIGS, ((qa1, ka1, va1), (qa2, ka2, va2), (qa3, ka3, va3))):
            o, lse = _attention(q, k, v, None, window=window // d, kv_div=1, batch=batch, seq=seq)
            oa.append(o.reshape(batch * seq, A_GROUP_WIDTH))
            la.append(lse.reshape(batch * seq, LANES))
        yb, _ = _attention(qb, kb, vb, b_sinks[l], window=B_WINDOW - 1, kv_div=2, batch=batch, seq=seq)
        yb = yb.reshape(batch * seq, A_GROUP_WIDTH)
        yc = _gla(qc, kc, vc, gl, cr, gate_up, gate_bias, row(c_out_norm), tri, l, batch, seq)
        xt = _merge(xt, row(norm_mix), oa, la, yb, yc, w_gate, wb, wo, expand, l)
        xt = _ffn(xt, row(norm_ffn2), w2_in, w2_out, l)
    return xt.reshape(batch, seq, D_MODEL)
```

```python
import functools

import numpy as np
import jax
import jax.numpy as jnp
from jax import lax
from jax.experimental import pallas as pl
from jax.experimental.pallas import tpu as pltpu

F32 = jnp.float32
BF16 = jnp.bfloat16

D_MODEL = 1024
D_FF = 2816
HEAD_DIM = 64
NORM_EPS = 1e-6
ROPE_THETA = 10000.0
LOG2E = float(np.log2(np.e))
Q_SCALE = HEAD_DIM ** -0.5 * LOG2E
LANES = 128
ROPE_PACK_LANES = HEAD_DIM // 2
ROPE_PACK = LANES // ROPE_PACK_LANES
BLOCK = 128
A_CONFIGS = ((128, 1), (512, 4), (2048, 16))
A_GROUP_WIDTH = 512
B_WINDOW = 128
C_HEADS = 4
C_DK = 64
C_DV = 128
C_GATE_RANK = 16
C_TAU = 16.0
C_CHUNK = 128
GLA_SUB = 256
GLA_TILE = 2048
N_BRANCH = 3

A_SEG = 3 * A_GROUP_WIDTH
OFF_A = 0
OFF_B = 3 * A_SEG
B_SEG = 1024
OFF_C = OFF_B + B_SEG
C_SEG = 1664
W_PACKED = OFF_C + C_SEG

TOKEN_TILE = 512
ATTN_CHUNK = 2048
MXU_COLS = 256
FFN_CHUNKS = ((0, 1536), (1536, D_FF))
assert all((hi - lo) % MXU_COLS == 0 for lo, hi in FFN_CHUNKS)
VMEM_LIMIT = 56 * 2**20


def _params(n_axes):
    return pltpu.CompilerParams(dimension_semantics=("arbitrary",) * n_axes, vmem_limit_bytes=VMEM_LIMIT)


def _rms(x, gain):
    return x * lax.rsqrt(jnp.mean(x * x, axis=-1, keepdims=True) + NORM_EPS) * gain


def _dot_t(a, w_t):
    return lax.dot_general(a, w_t, (((1,), (1,)), ((), ())), preferred_element_type=F32)


def _sigmoid(x):
    return 0.5 * jnp.tanh(0.5 * x) + 0.5


def _resident(block, index_map):
    return pl.BlockSpec(block, index_map, pipeline_mode=pl.Buffered(1))


def _rope_kernel(pos_ref, inv_ref, sgn_ref, cos_ref, sin_ref):
    rows = pos_ref.shape[0]
    ang = pos_ref[...].astype(F32) * inv_ref[...]
    tables = (jnp.cos(ang), jnp.sin(ang))
    group = lax.broadcasted_iota(jnp.int32, (1, LANES), 1) // ROPE_PACK_LANES
    sgn = sgn_ref[...]
    for j in range(ROPE_PACK):
        for tab, ref, scale in zip(tables, (cos_ref, sin_ref), (None, sgn)):
            own = jnp.where(group == j, tab, 0.0)
            full = own
            for shift in range(1, ROPE_PACK):
                full = full + pltpu.roll(own, shift * ROPE_PACK_LANES, 1)
            ref[pl.ds(j, rows, stride=ROPE_PACK), :] = full if scale is None else full * scale


def _rope_tables(positions):
    t = positions.size
    inv = ROPE_THETA ** (-jnp.arange(0, HEAD_DIM, 2, dtype=F32) / HEAD_DIM)
    inv = jnp.tile(inv, LANES // (HEAD_DIM // 2))[None, :]
    sgn = jnp.tile(jnp.concatenate([-jnp.ones(HEAD_DIM // 2, F32), jnp.ones(HEAD_DIM // 2, F32)]),
                   LANES // HEAD_DIM)[None, :]
    tm = TOKEN_TILE
    tab = jax.ShapeDtypeStruct((t, LANES), F32)
    return pl.pallas_call(
        _rope_kernel,
        grid=(t // tm,),
        in_specs=[pl.BlockSpec((tm // ROPE_PACK, LANES), lambda i: (i, 0)),
                  pl.BlockSpec((1, LANES), lambda i: (0, 0)),
                  pl.BlockSpec((1, LANES), lambda i: (0, 0))],
        out_specs=[pl.BlockSpec((tm, LANES), lambda i: (i, 0))] * 2,
        out_shape=[tab, tab],
        name="rope",
        compiler_params=_params(1),
    )(jnp.repeat(positions.reshape(t // ROPE_PACK, ROPE_PACK), ROPE_PACK_LANES, axis=1), inv, sgn)


def _ffn_kernel(x_ref, g_ref, wg_ref, wu_ref, wo_ref, o_ref):
    x = x_ref[...]
    h = _rms(x, g_ref[...]).astype(BF16)
    acc = None
    for lo, hi in FFN_CHUNKS:
        sl = slice(lo, hi)
        g = jnp.dot(h, wg_ref[:, sl], preferred_element_type=F32)
        u = jnp.dot(h, wu_ref[:, sl], preferred_element_type=F32)
        a = (g * _sigmoid(g) * u).astype(BF16)
        part = jnp.dot(a, wo_ref[sl, :], preferred_element_type=F32)
        acc = part if acc is None else acc + part
    o_ref[...] = x + 0.5 * acc


def _ffn(x, gain, w_in, w_out, layer):
    t = x.shape[0]
    tm = TOKEN_TILE
    return pl.pallas_call(
        _ffn_kernel,
        grid=(t // tm,),
        in_specs=[pl.BlockSpec((tm, D_MODEL), lambda i: (i, 0)),
                  pl.BlockSpec((None, 1, D_MODEL), lambda i: (layer, 0, 0)),
                  _resident((None, D_MODEL, D_FF), lambda i: (layer, 0, 0)),
                  _resident((None, D_MODEL, D_FF), lambda i: (layer, 0, 1)),
                  _resident((None, D_FF, D_MODEL), lambda i: (layer, 0, 0))],
        out_specs=pl.BlockSpec((tm, D_MODEL), lambda i: (i, 0)),
        out_shape=jax.ShapeDtypeStruct((t, D_MODEL), F32),
        name="ffn",
        compiler_params=_params(1),
    )(x, gain, w_in, w_in, w_out)


def _inproj_kernel(x_ref, gm_ref, cos_ref, sin_ref, w_ref, gains_ref, ones_ref,
                   qa1, ka1, va1, qa2, ka2, va2, qa3, ka3, va3, qb, kb, vb, qc, kc, vc, gl, cr,
                   hs_ref, hp_ref, tb_ref):
    tm = x_ref.shape[0]
    n_slab = D_MODEL // LANES
    h = _rms(x_ref[...], gm_ref[...])
    h_bf = h.astype(BF16)
    cos0 = cos_ref[...]
    sin0 = sin_ref[...]
    ones = ones_ref[...]
    gains = gains_ref[...]

    first_half = (lax.broadcasted_iota(jnp.int32, (1, LANES), 1) % HEAD_DIM) < (HEAD_DIM // 2)

    def norm_rope(y, gain, cos, sin):
        ms = jnp.dot((y * y).astype(BF16), ones, preferred_element_type=F32)
        yn = y * lax.rsqrt(ms + NORM_EPS) * gain
        partner = jnp.where(first_half, pltpu.roll(yn, LANES - HEAD_DIM // 2, 1), pltpu.roll(yn, HEAD_DIM // 2, 1))
        return yn * cos + partner * sin

    for s in range(n_slab):
        hs_ref[s] = h[:, s * LANES:(s + 1) * LANES]
    tb_ref[0] = cos0
    tb_ref[1] = sin0

    def class_major(d):
        if d == 1:
            return h_bf, cos0, sin0
        rows = tm // d
        for c in range(d):
            for s in range(n_slab):
                hp_ref[c * rows:(c + 1) * rows, s * LANES:(s + 1) * LANES] = (
                    hs_ref[s, pl.ds(c, rows, stride=d), :].astype(BF16))
        cos = jnp.concatenate([tb_ref[0, pl.ds(c, rows, stride=d), :] for c in range(d)], axis=0)
        sin = jnp.concatenate([tb_ref[1, pl.ds(c, rows, stride=d), :] for c in range(d)], axis=0)
        return hp_ref[...], cos, sin

    def store_classes(ref, col, val, d):
        rows = tm // d
        for c in range(d):
            ref[c, :, col:col + LANES] = val[c * rows:(c + 1) * rows, :]

    for gi, (qo, ko, vo) in enumerate(((qa1, ka1, va1), (qa2, ka2, va2), (qa3, ka3, va3))):
        d = A_CONFIGS[gi][1]
        hp, cos, sin = class_major(d)
        yq, yk, yv = [_dot_t(hp, w_ref[pl.ds(OFF_A + part * A_SEG + gi * A_GROUP_WIDTH, A_GROUP_WIDTH), :])
                      for part in range(3)]
        for j in range(A_GROUP_WIDTH // LANES):
            c0 = j * LANES
            store_classes(qo, c0, norm_rope(yq[:, c0:c0 + LANES], gains[0:1], cos, sin).astype(BF16), d)
            store_classes(ko, c0, norm_rope(yk[:, c0:c0 + LANES], gains[1:2], cos, sin).astype(BF16), d)
            store_classes(vo, c0, yv[:, c0:c0 + LANES].astype(BF16), d)

    y = _dot_t(h_bf, w_ref[OFF_B:OFF_B + B_SEG, :])
    for j in range(4):
        c0 = j * LANES
        qb[0, :, c0:c0 + LANES] = norm_rope(y[:, c0:c0 + LANES], gains[2:3], cos0, sin0).astype(BF16)
    for j in range(2):
        c0 = j * LANES
        kb[0, :, c0:c0 + LANES] = norm_rope(y[:, 512 + c0:512 + c0 + LANES], gains[3:4], cos0, sin0).astype(BF16)
        vb[0, :, c0:c0 + LANES] = y[:, 768 + c0:768 + c0 + LANES].astype(BF16)

    y = _dot_t(h_bf, w_ref[OFF_C:OFF_C + C_SEG, :])
    qc[...] = y[:, 0:256].astype(BF16)
    kc[...] = y[:, 256:512].astype(BF16)
    vc[...] = y[:, 512:1024].astype(BF16)
    gl[...] = y[:, 1024:1152].astype(BF16)
    cr[...] = y[:, 1152:1664].astype(BF16)


def _inproj(x, gain, cos, sin, w_packed, head_gains, ones_bd, layer, batch, seq):
    t = x.shape[0]
    tm = TOKEN_TILE
    tiles_per_seq = seq // tm

    def cm_spec(d, width):
        return pl.BlockSpec((None, d, tm // d, width),
                            lambda i: (i // tiles_per_seq, 0, i % tiles_per_seq, 0))

    def cm_shape(d, width):
        return jax.ShapeDtypeStruct((batch, d, seq // d, width), BF16)

    def tok_spec(width):
        return pl.BlockSpec((tm, width), lambda i: (i, 0))

    out_specs, out_shape = [], []
    for _, d in A_CONFIGS:
        for _ in range(3):
            out_specs.append(cm_spec(d, A_GROUP_WIDTH))
            out_shape.append(cm_shape(d, A_GROUP_WIDTH))
    for width in (512, 256, 256):
        out_specs.append(cm_spec(1, width))
        out_shape.append(cm_shape(1, width))
    for width in (256, 256, 512, 128, 512):
        out_specs.append(tok_spec(width))
        out_shape.append(jax.ShapeDtypeStruct((t, width), BF16))

    return pl.pallas_call(
        _inproj_kernel,
        grid=(t // tm,),
        in_specs=[tok_spec(D_MODEL),
                  pl.BlockSpec((None, 1, D_MODEL), lambda i: (layer, 0, 0)),
                  tok_spec(LANES), tok_spec(LANES),
                  _resident((None, W_PACKED, D_MODEL), lambda i: (layer, 0, 0)),
                  pl.BlockSpec((None, 4, LANES), lambda i: (layer, 0, 0)),
                  pl.BlockSpec((LANES, LANES), lambda i: (0, 0))],
        out_specs=out_specs,
        out_shape=out_shape,
        scratch_shapes=[pltpu.VMEM((D_MODEL // LANES, tm, LANES), F32),
                        pltpu.VMEM((tm, D_MODEL), BF16),
                        pltpu.VMEM((2, tm, LANES), F32)],
        name="inproj",
        compiler_params=_params(1),
    )(x, gain, cos, sin, w_packed, head_gains, ones_bd)


def _attn_kernel(d, nb, window, kv_div, use_sink, *refs):
    if use_sink:
        sink_ref, q_ref, kc_ref, kp_ref, vc_ref, vp_ref, o_ref, lse_ref, osc, lsc, s_scr, p_scr, st_scr = refs
    else:
        q_ref, kc_ref, kp_ref, vc_ref, vp_ref, o_ref, lse_ref, osc, lsc, s_scr, p_scr, st_scr = refs
    chunk = pl.program_id(1)
    lane = lax.broadcasted_iota(jnp.int32, (1, LANES), 1)
    head_of_lane = lane // HEAD_DIM
    key = lax.broadcasted_iota(jnp.int32, (2 * BLOCK, 2 * BLOCK), 0)
    col = lax.broadcasted_iota(jnp.int32, (2 * BLOCK, 2 * BLOCK), 1)
    dist = (col % BLOCK) + BLOCK - key
    band = (dist >= 0) & (dist <= window)
    col_row = lax.broadcasted_iota(jnp.int32, (1, 2 * BLOCK), 1)
    sub8 = lax.broadcasted_iota(jnp.int32, (8, LANES), 0)
    nt = (((1,), (1,)), ((), ()))
    tn = (((0,), (0,)), ((), ()))

    n_units = d * nb

    def locate(u):
        c = u // nb if d > 1 else 0
        j = u % nb if nb > 1 else 0
        if isinstance(j, int):
            return c, j, j * BLOCK, max(j - 1, 0) * BLOCK
        return c, j, pl.multiple_of(j * BLOCK, BLOCK), pl.multiple_of(jnp.maximum(j - 1, 0) * BLOCK, BLOCK)

    def window_rows(cur_ref, prev_ref, u, p, in_loop=False):
        c, j, row0, prow = locate(u)
        kcol = (p // kv_div) * LANES
        ksl = slice(kcol, kcol + LANES)
        cur = cur_ref[c, pl.ds(row0, BLOCK), ksl]
        if isinstance(j, int):
            first = j == 0
        elif d == 1 and in_loop:
            first = False
        else:
            first = None
        if first is None:
            prev = jnp.where(j == 0, prev_ref[c, :, ksl], cur_ref[c, pl.ds(prow, BLOCK), ksl])
        else:
            prev = prev_ref[c, :, ksl] if first else cur_ref[c, pl.ds(prow, BLOCK), ksl]
        return jnp.concatenate([prev, cur], axis=0)

    def scores(u, in_loop=False, slot=0):
        c, j, row0, _ = locate(u)
        seq_start = jnp.logical_and(chunk == 0, j == 0)
        valid = band & jnp.logical_or(key >= BLOCK, jnp.logical_not(seq_start))
        for p in range(4):
            qp = q_ref[c, pl.ds(row0, BLOCK), p * LANES:(p + 1) * LANES]
            q2 = jnp.concatenate([jnp.where(head_of_lane == e, qp, jnp.zeros_like(qp)) for e in range(2)], axis=0)
            s = lax.dot_general(window_rows(kc_ref, kp_ref, u, p, in_loop), q2, nt,
                                preferred_element_type=F32)
            s = jnp.where(valid, s, -jnp.inf)
            s_scr[4 * slot + p] = s
            m = jnp.max(s, axis=0, keepdims=True)
            if use_sink:
                m = jnp.maximum(m, sink_row(p))
            st_scr[4 * slot + p, 0:1, :] = m

    def sink_row(p):
        return jnp.where(col_row < BLOCK, sink_ref[2 * p], sink_ref[2 * p + 1]) * LOG2E

    ones8 = jnp.ones((8, 2 * BLOCK), BF16)

    def softmax(u, slot=0):
        for p in range(4 * slot, 4 * slot + 4):
            m = st_scr[p, 0:1, :]
            p_scr[p] = jnp.exp2(s_scr[p] - m).astype(BF16)
            st_scr[p, 2:3, :] = m

    def outputs(u, in_loop=False, slot=0):
        c, _, row0, _ = locate(u)
        start = row0 * d + c
        rows = pl.ds(start, BLOCK, stride=d) if d > 1 else pl.ds(start, BLOCK)
        lse8 = jnp.zeros((8, LANES), F32)
        for p in range(4):
            pr, m = p_scr[4 * slot + p], st_scr[4 * slot + p, 2:3, :]
            ob = lax.dot_general(window_rows(vc_ref, vp_ref, u, p, in_loop), pr, tn,
                                 preferred_element_type=F32)
            den = jnp.dot(ones8, pr, preferred_element_type=F32)[0:1, :]
            if use_sink:
                den = den + jnp.exp2(sink_row(p) - m)
            inv = 1.0 / den
            o_t = jnp.concatenate([ob[0:HEAD_DIM, 0:BLOCK] * inv[:, 0:BLOCK],
                                   ob[HEAD_DIM:LANES, BLOCK:2 * BLOCK] * inv[:, BLOCK:2 * BLOCK]], axis=0)
            if d > 1:
                osc[p, rows, :] = o_t.T
            else:
                o_ref[rows, p * LANES:(p + 1) * LANES] = o_t.T.astype(BF16)
            l2 = m + jnp.log2(den)
            lse8 = jnp.where(sub8 == 2 * p, l2[:, 0:BLOCK], lse8)
            lse8 = jnp.where(sub8 == 2 * p + 1, l2[:, BLOCK:2 * BLOCK], lse8)
        lse_t = jnp.concatenate([lse8, jnp.zeros((LANES - 8, LANES), F32)], axis=0)
        if d > 1:
            lsc[rows, :] = lse_t.T
        else:
            lse_ref[rows, :] = lse_t.T

    def pair(stage, t, *args):
        for slot in range(2):
            stage(2 * t + slot, *args, slot=slot)

    n_pairs = n_units // 2
    pair(scores, 0)
    pair(softmax, 0)
    pair(scores, 1)
    pair(outputs, 0)
    pair(softmax, 1)
    pair(scores, 2)

    def body(t, carry):
        pair(outputs, t - 2, True)
        pair(softmax, t - 1)
        pair(scores, t, True)
        return carry

    lax.fori_loop(3, n_pairs, body, 0)
    pair(outputs, n_pairs - 2)
    pair(softmax, n_pairs - 1)
    pair(outputs, n_pairs - 1)
    if d > 1:
        for p in range(4):
            o_ref[:, p * LANES:(p + 1) * LANES] = osc[p].astype(BF16)
        lse_ref[...] = lsc[...]


def _attention(q, k, v, sinks, *, window, kv_div, batch, seq):
    d = q.shape[1]
    chunk = ATTN_CHUNK
    nb = chunk // (d * BLOCK)
    kw = k.shape[-1]
    rows = nb * BLOCK
    cur = lambda b, i: (b, 0, i, 0)
    prev = lambda b, i: (b, 0, jnp.maximum(i * nb - 1, 0), 0)
    in_specs = [pl.BlockSpec((None, d, rows, A_GROUP_WIDTH), cur),
                pl.BlockSpec((None, d, rows, kw), cur),
                pl.BlockSpec((None, d, BLOCK, kw), prev),
                pl.BlockSpec((None, d, rows, kw), cur),
                pl.BlockSpec((None, d, BLOCK, kw), prev)]
    args = [q, k, k, v, v]
    use_sink = sinks is not None
    if use_sink:
        in_specs = [pl.BlockSpec(memory_space=pltpu.SMEM)] + in_specs
        args = [sinks] + args
    return pl.pallas_call(
        functools.partial(_attn_kernel, d, nb, window, kv_div, use_sink),
        grid=(batch, seq // chunk),
        in_specs=in_specs,
        out_specs=[pl.BlockSpec((None, chunk, A_GROUP_WIDTH), lambda b, i: (b, i, 0)),
                   pl.BlockSpec((None, chunk, LANES), lambda b, i: (b, i, 0))],
        out_shape=[jax.ShapeDtypeStruct((batch, seq, A_GROUP_WIDTH), BF16),
                   jax.ShapeDtypeStruct((batch, seq, LANES), F32)],
        scratch_shapes=[pltpu.VMEM((4, chunk if d > 1 else 8, LANES), F32),
                        pltpu.VMEM((chunk if d > 1 else 8, LANES), F32),
                        pltpu.VMEM((8, 2 * BLOCK, 2 * BLOCK), F32),
                        pltpu.VMEM((8, 2 * BLOCK, 2 * BLOCK), BF16),
                        pltpu.VMEM((8, 8, 2 * BLOCK), F32)],
        name=f"attn_d{d}_w{window}",
        compiler_params=_params(2),
    )(*args)


def _gla_kernel(q_ref, k_ref, v_ref, gl_ref, cr_ref, gu_ref, gb_ref, on_ref, tri_ref, y_ref,
                st_ref, qd_s, qs_s, ks_s, kd_s, dec_s):
    tm = q_ref.shape[0]
    width = C_HEADS * C_DK
    n_chunk = GLA_SUB // C_CHUNK

    @pl.when(pl.program_id(1) == 0)
    def _():
        st_ref[...] = jnp.zeros_like(st_ref)

    tri = tri_ref[...]
    keep = tri > 0
    on_gain = on_ref[...]
    head_of_lane = lax.broadcasted_iota(jnp.int32, (1, LANES), 1) // C_DK
    own_block = ((lax.broadcasted_iota(jnp.int32, (2 * C_DV, LANES), 0) // C_DV)
                 == (lax.broadcasted_iota(jnp.int32, (2 * C_DV, LANES), 1) // C_DK))
    nt = (((1,), (1,)), ((), ()))
    tn = (((0,), (0,)), ((), ()))

    def chunked(a):
        return a.reshape(n_chunk, C_CHUNK, width)

    def sub_rows(sub):
        start = sub * GLA_SUB
        return pl.ds(start if isinstance(sub, int) else pl.multiple_of(start, GLA_SUB), GLA_SUB)

    def prepare(sub):
        rows = sub_rows(sub)
        x = jnp.dot(gl_ref[rows, :], gu_ref[...], preferred_element_type=F32) + gb_ref[...]
        lg = (jnp.minimum(x, 0.0) - jnp.log(1.0 + jnp.exp(-jnp.abs(x)))) * (1.0 / C_TAU)
        g1 = lg.astype(BF16)
        g2 = (lg - g1.astype(F32)).astype(BF16)
        bs = jnp.dot(tri, jnp.concatenate([g1, g2], axis=1), preferred_element_type=F32)
        b = chunked(bs[:, 0:width] + bs[:, width:2 * width])
        b_mid = b[:, C_CHUNK // 2 - 1:C_CHUNK // 2, :]
        b_last = b[:, C_CHUNK - 1:C_CHUNK, :]
        q = chunked(q_ref[rows, :].astype(F32) * (C_DK ** -0.5))
        k = chunked(k_ref[rows, :].astype(F32))
        qd_s[...] = (q * jnp.exp(b)).astype(BF16)
        qs_s[...] = (q * jnp.exp(b - b_mid)).astype(BF16)
        ks_s[...] = (k * jnp.exp(b_mid - b)).astype(BF16)
        kd_s[...] = (k * jnp.exp(b_last - b)).astype(BF16)
        dec_s[...] = jnp.broadcast_to(jnp.exp(b_last), dec_s.shape)

    def finish(sub):
        rows = sub_rows(sub)
        v = v_ref[rows, :]
        qd, qs, ks, kd = qd_s[...], qs_s[...], ks_s[...], kd_s[...]
        decay = dec_s[:, 0:1, :]
        for t in range(C_HEADS // 2):
            kcols = slice(t * LANES, (t + 1) * LANES)
            vcols = slice(2 * t * LANES, 2 * (t + 1) * LANES)
            qs_t = qs[:, :, kcols].reshape(GLA_SUB, LANES)
            q_st = jnp.concatenate([jnp.where(head_of_lane == e, qs_t, jnp.zeros((), BF16)) for e in range(2)], axis=0)
            att = lax.dot_general(q_st, ks[:, :, kcols].reshape(GLA_SUB, LANES), nt, preferred_element_type=F32)
            v_t = v[:, vcols]
            o = jnp.concatenate(
                [jnp.dot(jnp.where(keep, att[e * GLA_SUB:(e + 1) * GLA_SUB], 0.0).astype(BF16),
                         v_t[:, e * LANES:(e + 1) * LANES], preferred_element_type=F32) for e in range(2)], axis=1)
            st = st_ref[t]
            inter = []
            for c in range(n_chunk):
                inter.append(lax.dot_general(qd[c, :, kcols], st.astype(BF16), nt, preferred_element_type=F32))
                inc = lax.dot_general(v_t[c * C_CHUNK:(c + 1) * C_CHUNK, :], kd[c, :, kcols], tn,
                                      preferred_element_type=F32)
                st = jnp.where(own_block, st * decay[c, :, kcols] + inc, 0.0)
            st_ref[t] = st
            o = o + jnp.concatenate(inter, axis=0)
            for e in range(2):
                cols = slice((2 * t + e) * LANES, (2 * t + e + 1) * LANES)
                gate = cr_ref[rows, cols].astype(F32)
                y_ref[rows, cols] = (_rms(o[:, e * LANES:(e + 1) * LANES], on_gain)
                                     * (gate * _sigmoid(gate))).astype(BF16)

    n_sub = tm // GLA_SUB
    prepare(0)

    def body(sub, carry):
        finish(sub - 1)
        prepare(sub)
        return carry

    lax.fori_loop(1, n_sub, body, 0)
    finish(n_sub - 1)


def _gla(q, k, v, gl, cr, gate_up, gate_bias, out_norm, tri, layer, batch, seq):
    t = q.shape[0]
    tm = GLA_TILE
    per_seq = seq // tm
    width = C_HEADS * C_DV
    kwidth = C_HEADS * C_DK
    tok = lambda w: pl.BlockSpec((tm, w), lambda b, i: (b * per_seq + i, 0))
    return pl.pallas_call(
        _gla_kernel,
        grid=(batch, per_seq),
        in_specs=[tok(kwidth), tok(kwidth), tok(width), tok(LANES), tok(width),
                  pl.BlockSpec((None, LANES, kwidth), lambda b, i: (layer, 0, 0)),
                  pl.BlockSpec((None, 1, kwidth), lambda b, i: (layer, 0, 0)),
                  pl.BlockSpec((None, 1, C_DV), lambda b, i: (layer, 0, 0)),
                  pl.BlockSpec((GLA_SUB, GLA_SUB), lambda b, i: (0, 0))],
        out_specs=tok(width),
        out_shape=jax.ShapeDtypeStruct((t, width), BF16),
        scratch_shapes=[pltpu.VMEM((C_HEADS // 2, 2 * C_DV, LANES), F32)]
        + [pltpu.VMEM((GLA_SUB // C_CHUNK, C_CHUNK, kwidth), BF16)] * 4
        + [pltpu.VMEM((GLA_SUB // C_CHUNK, 8, kwidth), F32)],
        name="gla",
        compiler_params=_params(2),
    )(q, k, v, gl, cr, gate_up, gate_bias, out_norm, tri)


def _merge_kernel(x_ref, gm_ref, o1, o2, o3, l1, l2, l3, yb_ref, yc_ref, wg_ref, wb_ref, wo_ref, ex_ref, out_ref):
    x = x_ref[...]
    h = _rms(x, gm_ref[...]).astype(BF16)
    la, lb, lc = l1[...], l2[...], l3[...]
    top = jnp.maximum(jnp.maximum(la, lb), lc)
    ea, eb, ec = jnp.exp2(la - top), jnp.exp2(lb - top), jnp.exp2(lc - top)
    inv = 1.0 / (ea + eb + ec)
    expand = ex_ref[...]
    low_lanes = lax.broadcasted_iota(jnp.int32, (1, LANES), 1) < LANES // 2
    ya = None
    for e, o in ((ea, o1), (eb, o2), (ec, o3)):
        w = e * inv
        w_hi = w.astype(BF16)
        w_lo = pltpu.roll(w - w_hi.astype(F32), LANES // 2, 1).astype(BF16)
        wx = jnp.dot(jnp.where(low_lanes, w_hi, w_lo), expand, preferred_element_type=F32)
        term = wx * o[...].astype(F32)
        ya = term if ya is None else ya + term
    merged = None
    for i, y in enumerate((ya.astype(BF16), yb_ref[...], yc_ref[...])):
        gate = _sigmoid(_dot_t(h, wg_ref[i * D_MODEL:(i + 1) * D_MODEL, :]))
        term = gate * jnp.dot(y, wb_ref[i], preferred_element_type=F32)
        merged = term if merged is None else merged + term
    out_ref[...] = x + jnp.dot(merged.astype(BF16), wo_ref[...], preferred_element_type=F32)


def _merge(x, gain, oa, la, yb, yc, w_gate, w_branch, w_out, expand, layer):
    t = x.shape[0]
    tm = TOKEN_TILE
    tok = lambda w: pl.BlockSpec((tm, w), lambda i: (i, 0))
    bw = A_GROUP_WIDTH
    return pl.pallas_call(
        _merge_kernel,
        grid=(t // tm,),
        in_specs=[tok(D_MODEL),
                  pl.BlockSpec((None, 1, D_MODEL), lambda i: (layer, 0, 0)),
                  tok(bw), tok(bw), tok(bw), tok(LANES), tok(LANES), tok(LANES), tok(bw), tok(bw),
                  _resident((None, N_BRANCH * D_MODEL, D_MODEL), lambda i: (layer, 0, 0)),
                  _resident((None, N_BRANCH, bw, D_MODEL), lambda i: (layer, 0, 0, 0)),
                  _resident((None, D_MODEL, D_MODEL), lambda i: (layer, 0, 0)),
                  pl.BlockSpec((LANES, bw), lambda i: (0, 0))],
        out_specs=tok(D_MODEL),
        out_shape=jax.ShapeDtypeStruct((t, D_MODEL), F32),
        name="merge",
        compiler_params=_params(1),
    )(x, gain, *oa, *la, yb, yc, w_gate, w_branch, w_out, expand)


def _pack_w_in(w_in):
    w_t = jnp.swapaxes(w_in, 1, 2)
    n_layer = w_t.shape[0]
    edges = np.cumsum([0, 4608, 512, 256, 256, 256, 512, C_GATE_RANK, 512, N_BRANCH * D_MODEL])
    seg = [w_t[:, edges[i]:edges[i + 1], :] for i in range(9)]
    a, b_q, b_kv, c_q, c_k, c_v, c_glow, c_r, gates = seg
    b_kv = b_kv.reshape(n_layer, 2, 2, 1, HEAD_DIM, D_MODEL)
    b_kv = jnp.broadcast_to(b_kv, (n_layer, 2, 2, 2, HEAD_DIM, D_MODEL)).reshape(n_layer, 512, D_MODEL)
    c_glow = jnp.pad(c_glow, ((0, 0), (0, LANES - C_GATE_RANK), (0, 0)))
    packed = jnp.concatenate([a, b_q, b_kv, c_q, c_k, c_v, c_glow, c_r], axis=1)
    assert packed.shape == (n_layer, W_PACKED, D_MODEL)
    return packed.astype(BF16), gates.astype(BF16)


def kernel(x, positions, norm_ffn1, w_ffn1_in, w_ffn1_out, norm_mix, w_in, a_q_norm, a_k_norm, b_q_norm, b_k_norm,
           b_sinks, c_gate_up, c_gate_bias, c_out_norm, w_branch, w_out, norm_ffn2, w_ffn2_in, w_ffn2_out):
    batch, seq, _ = x.shape
    n_layer = w_in.shape[0]
    assert seq % ATTN_CHUNK == 0 and seq % GLA_TILE == 0 and x.shape[2] == D_MODEL

    w_packed, w_gate = _pack_w_in(w_in)
    w1_in, w1_out = w_ffn1_in.astype(BF16), w_ffn1_out.astype(BF16)
    w2_in, w2_out = w_ffn2_in.astype(BF16), w_ffn2_out.astype(BF16)
    wb, wo = w_branch.astype(BF16), w_out.astype(BF16)
    row = lambda g: g[:, None, :]
    head_gains = jnp.stack([jnp.tile(g * s, (1, LANES // HEAD_DIM)) for g, s in
                            ((a_q_norm, Q_SCALE), (a_k_norm, 1.0), (b_q_norm, Q_SCALE), (b_k_norm, 1.0))], axis=1)
    gate_up = jnp.pad(c_gate_up, ((0, 0), (0, LANES - C_GATE_RANK), (0, 0))).astype(BF16)
    gate_bias = row(c_gate_bias)

    lane_head = np.arange(LANES) // HEAD_DIM
    ones_bd = jnp.asarray((lane_head[:, None] == lane_head[None, :]) / HEAD_DIM, BF16)
    expand = jnp.asarray((np.arange(LANES) % (LANES // 2))[:, None] == (np.arange(A_GROUP_WIDTH) // HEAD_DIM)[None, :],
                         BF16)
    idx = np.arange(GLA_SUB)
    tri = jnp.asarray((idx[:, None] >= idx[None, :]) & (idx[:, None] // C_CHUNK == idx[None, :] // C_CHUNK), BF16)

    cos, sin = _rope_tables(positions)
    xt = x.reshape(batch * seq, D_MODEL)
    for l in range(n_layer):
        xt = _ffn(xt, row(norm_ffn1), w1_in, w1_out, l)
        (qa1, ka1, va1, qa2, ka2, va2, qa3, ka3, va3, qb, kb, vb, qc, kc, vc, gl, cr) = _inproj(
            xt, row(norm_mix), cos, sin, w_packed, head_gains, ones_bd, l, batch, seq)
        oa, la = [], []
        for (window, d), (q, k, v) in zip(A_CONFIGS, ((qa1, ka1, va1), (qa2, ka2, va2), (qa3, ka3, va3))):
            o, lse = _attention(q, k, v, None, window=window // d, kv_div=1, batch=batch, seq=seq)
            oa.append(o.reshape(batch * seq, A_GROUP_WIDTH))
            la.append(lse.reshape(batch * seq, LANES))
        yb, _ = _attention(qb, kb, vb, b_sinks[l], window=B_WINDOW - 1, kv_div=2, batch=batch, seq=seq)
        yb = yb.reshape(batch * seq, A_GROUP_WIDTH)
        yc = _gla(qc, kc, vc, gl, cr, gate_up, gate_bias, row(c_out_norm), tri, l, batch, seq)
        xt = _merge(xt, row(norm_mix), oa, la, yb, yc, w_gate, wb, wo, expand, l)
        xt = _ffn(xt, row(norm_ffn2), w2_in, w2_out, l)
    return xt.reshape(batch, seq, D_MODEL)
```

```python
import functools

import numpy as np
import jax
import jax.numpy as jnp
from jax import lax
from jax.experimental import pallas as pl
from jax.experimental.pallas import tpu as pltpu

F32 = jnp.float32
BF16 = jnp.bfloat16

D_MODEL = 1024
D_FF = 2816
HEAD_DIM = 64
NORM_EPS = 1e-6
ROPE_THETA = 10000.0
LOG2E = float(np.log2(np.e))
Q_SCALE = HEAD_DIM ** -0.5 * LOG2E
LANES = 128
ROPE_PACK_LANES = HEAD_DIM // 2
ROPE_PACK = LANES // ROPE_PACK_LANES
BLOCK = 128
A_CONFIGS = ((128, 1), (512, 4), (2048, 16))
A_GROUP_WIDTH = 512
B_WINDOW = 128
C_HEADS = 4
C_DK = 64
C_DV = 128
C_GATE_RANK = 16
C_TAU = 16.0
C_CHUNK = 128
GLA_SUB = 256
GLA_TILE = 2048
N_BRANCH = 3

A_SEG = 3 * A_GROUP_WIDTH
OFF_A = 0
OFF_B = 3 * A_SEG
B_SEG = 1024
OFF_C = OFF_B + B_SEG
C_SEG = 1664
W_PACKED = OFF_C + C_SEG

TOKEN_TILE = 512
ATTN_CHUNK = 2048
MXU_COLS = 256
FFN_CHUNKS = ((0, 1536), (1536, D_FF))
assert all((hi - lo) % MXU_COLS == 0 for lo, hi in FFN_CHUNKS)
VMEM_LIMIT = 56 * 2**20


def _params(n_axes):
    return pltpu.CompilerParams(dimension_semantics=("arbitrary",) * n_axes, vmem_limit_bytes=VMEM_LIMIT)


def _rms(x, gain):
    return x * lax.rsqrt(jnp.mean(x * x, axis=-1, keepdims=True) + NORM_EPS) * gain


def _dot_t(a, w_t):
    return lax.dot_general(a, w_t, (((1,), (1,)), ((), ())), preferred_element_type=F32)


def _sigmoid(x):
    return 0.5 * jnp.tanh(0.5 * x) + 0.5


def _resident(block, index_map):
    return pl.BlockSpec(block, index_map, pipeline_mode=pl.Buffered(1))


def _rope_kernel(pos_ref, inv_ref, sgn_ref, cos_ref, sin_ref):
    rows = pos_ref.shape[0]
    ang = pos_ref[...].astype(F32) * inv_ref[...]
    tables = (jnp.cos(ang), jnp.sin(ang))
    group = lax.broadcasted_iota(jnp.int32, (1, LANES), 1) // ROPE_PACK_LANES
    sgn = sgn_ref[...]
    for j in range(ROPE_PACK):
        for tab, ref, scale in zip(tables, (cos_ref, sin_ref), (None, sgn)):
            own = jnp.where(group == j, tab, 0.0)
            full = own
            for shift in range(1, ROPE_PACK):
                full = full + pltpu.roll(own, shift * ROPE_PACK_LANES, 1)
            ref[pl.ds(j, rows, stride=ROPE_PACK), :] = full if scale is None else full * scale


def _rope_tables(positions):
    t = positions.size
    inv = ROPE_THETA ** (-jnp.arange(0, HEAD_DIM, 2, dtype=F32) / HEAD_DIM)
    inv = jnp.tile(inv, LANES // (HEAD_DIM // 2))[None, :]
    sgn = jnp.tile(jnp.concatenate([-jnp.ones(HEAD_DIM // 2, F32), jnp.ones(HEAD_DIM // 2, F32)]),
                   LANES // HEAD_DIM)[None, :]
    tm = TOKEN_TILE
    tab = jax.ShapeDtypeStruct((t, LANES), F32)
    return pl.pallas_call(
        _rope_kernel,
        grid=(t // tm,),
        in_specs=[pl.BlockSpec((tm // ROPE_PACK, LANES), lambda i: (i, 0)),
                  pl.BlockSpec((1, LANES), lambda i: (0, 0)),
                  pl.BlockSpec((1, LANES), lambda i: (0, 0))],
        out_specs=[pl.BlockSpec((tm, LANES), lambda i: (i, 0))] * 2,
        out_shape=[tab, tab],
        name="rope",
        compiler_params=_params(1),
    )(jnp.repeat(positions.reshape(t // ROPE_PACK, ROPE_PACK), ROPE_PACK_LANES, axis=1), inv, sgn)


def _ffn_kernel(x_ref, g_ref, wg_ref, wu_ref, wo_ref, o_ref):
    x = x_ref[...]
    h = _rms(x, g_ref[...]).astype(BF16)
    acc = None
    for lo, hi in FFN_CHUNKS:
        sl = slice(lo, hi)
        g = jnp.dot(h, wg_ref[:, sl], preferred_element_type=F32)
        u = jnp.dot(h, wu_ref[:, sl], preferred_element_type=F32)
        a = (g * _sigmoid(g) * u).astype(BF16)
        part = jnp.dot(a, wo_ref[sl, :], preferred_element_type=F32)
        acc = part if acc is None else acc + part
    o_ref[...] = x + 0.5 * acc


def _ffn(x, gain, w_in, w_out, layer):
    t = x.shape[0]
    tm = TOKEN_TILE
    return pl.pallas_call(
        _ffn_kernel,
        grid=(t // tm,),
        in_specs=[pl.BlockSpec((tm, D_MODEL), lambda i: (i, 0)),
                  pl.BlockSpec((None, 1, D_MODEL), lambda i: (layer, 0, 0)),
                  _resident((None, D_MODEL, D_FF), lambda i: (layer, 0, 0)),
                  _resident((None, D_MODEL, D_FF), lambda i: (layer, 0, 1)),
                  _resident((None, D_FF, D_MODEL), lambda i: (layer, 0, 0))],
        out_specs=pl.BlockSpec((tm, D_MODEL), lambda i: (i, 0)),
        out_shape=jax.ShapeDtypeStruct((t, D_MODEL), F32),
        name="ffn",
        compiler_params=_params(1),
    )(x, gain, w_in, w_in, w_out)


def _inproj_kernel(x_ref, gm_ref, cos_ref, sin_ref, w_ref, gains_ref, ones_ref,
                   qa1, ka1, va1, qa2, ka2, va2, qa3, ka3, va3, qb, kb, vb, qc, kc, vc, gl, cr,
                   hs_ref, hp_ref, tb_ref):
    tm = x_ref.shape[0]
    n_slab = D_MODEL // LANES
    h = _rms(x_ref[...], gm_ref[...])
    h_bf = h.astype(BF16)
    cos0 = cos_ref[...]
    sin0 = sin_ref[...]
    ones = ones_ref[...]
    gains = gains_ref[...]

    first_half = (lax.broadcasted_iota(jnp.int32, (1, LANES), 1) % HEAD_DIM) < (HEAD_DIM // 2)

    def norm_rope(y, gain, cos, sin):
        ms = jnp.dot((y * y).astype(BF16), ones, preferred_element_type=F32)
        yn = y * lax.rsqrt(ms + NORM_EPS) * gain
        partner = jnp.where(first_half, pltpu.roll(yn, LANES - HEAD_DIM // 2, 1), pltpu.roll(yn, HEAD_DIM // 2, 1))
        return yn * cos + partner * sin

    for s in range(n_slab):
        hs_ref[s] = h[:, s * LANES:(s + 1) * LANES]
    tb_ref[0] = cos0
    tb_ref[1] = sin0

    def class_major(d):
        if d == 1:
            return h_bf, cos0, sin0
        rows = tm // d
        for c in range(d):
            for s in range(n_slab):
                hp_ref[c * rows:(c + 1) * rows, s * LANES:(s + 1) * LANES] = (
                    hs_ref[s, pl.ds(c, rows, stride=d), :].astype(BF16))
        cos = jnp.concatenate([tb_ref[0, pl.ds(c, rows, stride=d), :] for c in range(d)], axis=0)
        sin = jnp.concatenate([tb_ref[1, pl.ds(c, rows, stride=d), :] for c in range(d)], axis=0)
        return hp_ref[...], cos, sin

    def store_classes(ref, col, val, d):
        rows = tm // d
        for c in range(d):
            ref[c, :, col:col + LANES] = val[c * rows:(c + 1) * rows, :]

    for gi, (qo, ko, vo) in enumerate(((qa1, ka1, va1), (qa2, ka2, va2), (qa3, ka3, va3))):
        d = A_CONFIGS[gi][1]
        hp, cos, sin = class_major(d)
        yq, yk, yv = [_dot_t(hp, w_ref[pl.ds(OFF_A + part * A_SEG + gi * A_GROUP_WIDTH, A_GROUP_WIDTH), :])
                      for part in range(3)]
        for j in range(A_GROUP_WIDTH // LANES):
            c0 = j * LANES
            store_classes(qo, c0, norm_rope(yq[:, c0:c0 + LANES], gains[0:1], cos, sin).astype(BF16), d)
            store_classes(ko, c0, norm_rope(yk[:, c0:c0 + LANES], gains[1:2], cos, sin).astype(BF16), d)
            store_classes(vo, c0, yv[:, c0:c0 + LANES].astype(BF16), d)

    y = _dot_t(h_bf, w_ref[OFF_B:OFF_B + B_SEG, :])
    for j in range(4):
        c0 = j * LANES
        qb[0, :, c0:c0 + LANES] = norm_rope(y[:, c0:c0 + LANES], gains[2:3], cos0, sin0).astype(BF16)
    for j in range(2):
        c0 = j * LANES
        kb[0, :, c0:c0 + LANES] = norm_rope(y[:, 512 + c0:512 + c0 + LANES], gains[3:4], cos0, sin0).astype(BF16)
        vb[0, :, c0:c0 + LANES] = y[:, 768 + c0:768 + c0 + LANES].astype(BF16)

    y = _dot_t(h_bf, w_ref[OFF_C:OFF_C + C_SEG, :])
    qc[...] = y[:, 0:256].astype(BF16)
    kc[...] = y[:, 256:512].astype(BF16)
    vc[...] = y[:, 512:1024].astype(BF16)
    gl[...] = y[:, 1024:1152].astype(BF16)
    cr[...] = y[:, 1152:1664].astype(BF16)


def _inproj(x, gain, cos, sin, w_packed, head_gains, ones_bd, layer, batch, seq):
    t = x.shape[0]
    tm = TOKEN_TILE
    tiles_per_seq = seq // tm

    def cm_spec(d, width):
        return pl.BlockSpec((None, d, tm // d, width),
                            lambda i: (i // tiles_per_seq, 0, i % tiles_per_seq, 0))

    def cm_shape(d, width):
        return jax.ShapeDtypeStruct((batch, d, seq // d, width), BF16)

    def tok_spec(width):
        return pl.BlockSpec((tm, width), lambda i: (i, 0))

    out_specs, out_shape = [], []
    for _, d in A_CONFIGS:
        for _ in range(3):
            out_specs.append(cm_spec(d, A_GROUP_WIDTH))
            out_shape.append(cm_shape(d, A_GROUP_WIDTH))
    for width in (512, 256, 256):
        out_specs.append(cm_spec(1, width))
        out_shape.append(cm_shape(1, width))
    for width in (256, 256, 512, 128, 512):
        out_specs.append(tok_spec(width))
        out_shape.append(jax.ShapeDtypeStruct((t, width), BF16))

    return pl.pallas_call(
        _inproj_kernel,
        grid=(t // tm,),
        in_specs=[tok_spec(D_MODEL),
                  pl.BlockSpec((None, 1, D_MODEL), lambda i: (layer, 0, 0)),
                  tok_spec(LANES), tok_spec(LANES),
                  _resident((None, W_PACKED, D_MODEL), lambda i: (layer, 0, 0)),
                  pl.BlockSpec((None, 4, LANES), lambda i: (layer, 0, 0)),
                  pl.BlockSpec((LANES, LANES), lambda i: (0, 0))],
        out_specs=out_specs,
        out_shape=out_shape,
        scratch_shapes=[pltpu.VMEM((D_MODEL // LANES, tm, LANES), F32),
                        pltpu.VMEM((tm, D_MODEL), BF16),
                        pltpu.VMEM((2, tm, LANES), F32)],
        name="inproj",
        compiler_params=_params(1),
    )(x, gain, cos, sin, w_packed, head_gains, ones_bd)


def _attn_kernel(d, nb, window, kv_div, use_sink, *refs):
    if use_sink:
        sink_ref, q_ref, kc_ref, kp_ref, vc_ref, vp_ref, o_ref, lse_ref, osc, lsc, s_scr, p_scr, st_scr = refs
    else:
        q_ref, kc_ref, kp_ref, vc_ref, vp_ref, o_ref, lse_ref, osc, lsc, s_scr, p_scr, st_scr = refs
    chunk = pl.program_id(1)
    lane = lax.broadcasted_iota(jnp.int32, (1, LANES), 1)
    head_of_lane = lane // HEAD_DIM
    key = lax.broadcasted_iota(jnp.int32, (2 * BLOCK, 2 * BLOCK), 0)
    col = lax.broadcasted_iota(jnp.int32, (2 * BLOCK, 2 * BLOCK), 1)
    dist = (col % BLOCK) + BLOCK - key
    band = (dist >= 0) & (dist <= window)
    col_row = lax.broadcasted_iota(jnp.int32, (1, 2 * BLOCK), 1)
    sub8 = lax.broadcasted_iota(jnp.int32, (8, LANES), 0)
    nt = (((1,), (1,)), ((), ()))
    tn = (((0,), (0,)), ((), ()))

    n_units = d * nb

    def locate(u):
        c = u // nb if d > 1 else 0
        j = u % nb if nb > 1 else 0
        if isinstance(j, int):
            return c, j, j * BLOCK, max(j - 1, 0) * BLOCK
        return c, j, pl.multiple_of(j * BLOCK, BLOCK), pl.multiple_of(jnp.maximum(j - 1, 0) * BLOCK, BLOCK)

    def window_rows(cur_ref, prev_ref, u, p, in_loop=False):
        c, j, row0, prow = locate(u)
        kcol = (p // kv_div) * LANES
        ksl = slice(kcol, kcol + LANES)
        cur = cur_ref[c, pl.ds(row0, BLOCK), ksl]
        if isinstance(j, int):
            first = j == 0
        elif d == 1 and in_loop:
            first = False
        else:
            first = None
        if first is None:
            prev = jnp.where(j == 0, prev_ref[c, :, ksl], cur_ref[c, pl.ds(prow, BLOCK), ksl])
        else:
            prev = prev_ref[c, :, ksl] if first else cur_ref[c, pl.ds(prow, BLOCK), ksl]
        return jnp.concatenate([prev, cur], axis=0)

    def scores(u, in_loop=False, slot=0):
        c, j, row0, _ = locate(u)
        seq_start = jnp.logical_and(chunk == 0, j == 0)
        valid = band & jnp.logical_or(key >= BLOCK, jnp.logical_not(seq_start))
        for p in range(4):
            qp = q_ref[c, pl.ds(row0, BLOCK), p * LANES:(p + 1) * LANES]
            q2 = jnp.concatenate([jnp.where(head_of_lane == e, qp, jnp.zeros_like(qp)) for e in range(2)], axis=0)
            s = lax.dot_general(window_rows(kc_ref, kp_ref, u, p, in_loop), q2, nt,
                                preferred_element_type=F32)
            s = jnp.where(valid, s, -jnp.inf)
            s_scr[4 * slot + p] = s
            m = jnp.max(s, axis=0, keepdims=True)
            if use_sink:
                m = jnp.maximum(m, sink_row(p))
            st_scr[4 * slot + p, 0:1, :] = m

    def sink_row(p):
        return jnp.where(col_row < BLOCK, sink_ref[2 * p], sink_ref[2 * p + 1]) * LOG2E

    ones8 = jnp.ones((8, 2 * BLOCK), BF16)

    def softmax(u, slot=0):
        for p in range(4 * slot, 4 * slot + 4):
            m = st_scr[p, 0:1, :]
            p_scr[p] = jnp.exp2(s_scr[p] - m).astype(BF16)
            st_scr[p, 2:3, :] = m

    def outputs(u, in_loop=False, slot=0):
        c, _, row0, _ = locate(u)
        start = row0 * d + c
        rows = pl.ds(start, BLOCK, stride=d) if d > 1 else pl.ds(start, BLOCK)
        lse8 = jnp.zeros((8, LANES), F32)
        for p in range(4):
            pr, m = p_scr[4 * slot + p], st_scr[4 * slot + p, 2:3, :]
            ob = lax.dot_general(window_rows(vc_ref, vp_ref, u, p, in_loop), pr, tn,
                                 preferred_element_type=F32)
            den = jnp.dot(ones8, pr, preferred_element_type=F32)[0:1, :]
            if use_sink:
                den = den + jnp.exp2(sink_row(p) - m)
            inv = 1.0 / den
            o_t = jnp.concatenate([ob[0:HEAD_DIM, 0:BLOCK] * inv[:, 0:BLOCK],
                                   ob[HEAD_DIM:LANES, BLOCK:2 * BLOCK] * inv[:, BLOCK:2 * BLOCK]], axis=0)
            if d > 1:
                osc[p, rows, :] = o_t.T
            else:
                o_ref[rows, p * LANES:(p + 1) * LANES] = o_t.T.astype(BF16)
            l2 = m + jnp.log2(den)
            lse8 = jnp.where(sub8 == 2 * p, l2[:, 0:BLOCK], lse8)
            lse8 = jnp.where(sub8 == 2 * p + 1, l2[:, BLOCK:2 * BLOCK], lse8)
        lse_t = jnp.concatenate([lse8, jnp.zeros((LANES - 8, LANES), F32)], axis=0)
        if d > 1:
            lsc[rows, :] = lse_t.T
        else:
            lse_ref[rows, :] = lse_t.T

    def pair(stage, t, *args):
        for slot in range(2):
            stage(2 * t + slot, *args, slot=slot)

    n_pairs = n_units // 2
    pair(scores, 0)
    pair(softmax, 0)
    pair(scores, 1)
    pair(outputs, 0)
    pair(softmax, 1)
    pair(scores, 2)

    def body(t, carry):
        pair(outputs, t - 2, True)
        pair(softmax, t - 1)
        pair(scores, t, True)
        return carry

    lax.fori_loop(3, n_pairs, body, 0)
    pair(outputs, n_pairs - 2)
    pair(softmax, n_pairs - 1)
    pair(outputs, n_pairs - 1)
    if d > 1:
        for p in range(4):
            o_ref[:, p * LANES:(p + 1) * LANES] = osc[p].astype(BF16)
        lse_ref[...] = lsc[...]


def _attention(q, k, v, sinks, *, window, kv_div, batch, seq):
    d = q.shape[1]
    chunk = ATTN_CHUNK
    nb = chunk // (d * BLOCK)
    kw = k.shape[-1]
    rows = nb * BLOCK
    cur = lambda b, i: (b, 0, i, 0)
    prev = lambda b, i: (b, 0, jnp.maximum(i * nb - 1, 0), 0)
    in_specs = [pl.BlockSpec((None, d, rows, A_GROUP_WIDTH), cur),
                pl.BlockSpec((None, d, rows, kw), cur),
                pl.BlockSpec((None, d, BLOCK, kw), prev),
                pl.BlockSpec((None, d, rows, kw), cur),
                pl.BlockSpec((None, d, BLOCK, kw), prev)]
    args = [q, k, k, v, v]
    use_sink = sinks is not None
    if use_sink:
        in_specs = [pl.BlockSpec(memory_space=pltpu.SMEM)] + in_specs
        args = [sinks] + args
    return pl.pallas_call(
        functools.partial(_attn_kernel, d, nb, window, kv_div, use_sink),
        grid=(batch, seq // chunk),
        in_specs=in_specs,
        out_specs=[pl.BlockSpec((None, chunk, A_GROUP_WIDTH), lambda b, i: (b, i, 0)),
                   pl.BlockSpec((None, chunk, LANES), lambda b, i: (b, i, 0))],
        out_shape=[jax.ShapeDtypeStruct((batch, seq, A_GROUP_WIDTH), BF16),
                   jax.ShapeDtypeStruct((batch, seq, LANES), F32)],
        scratch_shapes=[pltpu.VMEM((4, chunk if d > 1 else 8, LANES), F32),
                        pltpu.VMEM((chunk if d > 1 else 8, LANES), F32),
                        pltpu.VMEM((8, 2 * BLOCK, 2 * BLOCK), F32),
                        pltpu.VMEM((8, 2 * BLOCK, 2 * BLOCK), BF16),
                        pltpu.VMEM((8, 8, 2 * BLOCK), F32)],
        name=f"attn_d{d}_w{window}",
        compiler_params=_params(2),
    )(*args)


def _gla_kernel(q_ref, k_ref, v_ref, gl_ref, cr_ref, gu_ref, gb_ref, on_ref, tri_ref, y_ref,
                st_ref, qd_s, qs_s, ks_s, kd_s, dec_s):
    tm = q_ref.shape[0]
    width = C_HEADS * C_DK
    n_chunk = GLA_SUB // C_CHUNK

    @pl.when(pl.program_id(1) == 0)
    def _():
        st_ref[...] = jnp.zeros_like(st_ref)

    tri = tri_ref[...]
    keep = tri > 0
    on_gain = on_ref[...]
    head_of_lane = lax.broadcasted_iota(jnp.int32, (1, LANES), 1) // C_DK
    own_block = ((lax.broadcasted_iota(jnp.int32, (2 * C_DV, LANES), 0) // C_DV)
                 == (lax.broadcasted_iota(jnp.int32, (2 * C_DV, LANES), 1) // C_DK))
    nt = (((1,), (1,)), ((), ()))
    tn = (((0,), (0,)), ((), ()))

    def chunked(a):
        return a.reshape(n_chunk, C_CHUNK, width)

    def sub_rows(sub):
        start = sub * GLA_SUB
        return pl.ds(start if isinstance(sub, int) else pl.multiple_of(start, GLA_SUB), GLA_SUB)

    def prepare(sub, slot):
        sl = slice(slot * n_chunk, (slot + 1) * n_chunk)
        rows = sub_rows(sub)
        x = jnp.dot(gl_ref[rows, :], gu_ref[...], preferred_element_type=F32) + gb_ref[...]
        lg = (jnp.minimum(x, 0.0) - jnp.log(1.0 + jnp.exp(-jnp.abs(x)))) * (1.0 / C_TAU)
        g1 = lg.astype(BF16)
        g2 = (lg - g1.astype(F32)).astype(BF16)
        bs = jnp.dot(tri, jnp.concatenate([g1, g2], axis=1), preferred_element_type=F32)
        b = chunked(bs[:, 0:width] + bs[:, width:2 * width])
        b_mid = b[:, C_CHUNK // 2 - 1:C_CHUNK // 2, :]
        b_last = b[:, C_CHUNK - 1:C_CHUNK, :]
        q = chunked(q_ref[rows, :].astype(F32) * (C_DK ** -0.5))
        k = chunked(k_ref[rows, :].astype(F32))
        qd_s[sl] = (q * jnp.exp(b)).astype(BF16)
        qs_s[sl] = (q * jnp.exp(b - b_mid)).astype(BF16)
        ks_s[sl] = (k * jnp.exp(b_mid - b)).astype(BF16)
        kd_s[sl] = (k * jnp.exp(b_last - b)).astype(BF16)
        dec_s[sl] = jnp.broadcast_to(jnp.exp(b_last), (n_chunk, 8, width))

    def finish(sub, slot, st_in):
        sl = slice(slot * n_chunk, (slot + 1) * n_chunk)
        st_out = []
        rows = sub_rows(sub)
        v = v_ref[rows, :]
        qd, qs, ks, kd = qd_s[sl], qs_s[sl], ks_s[sl], kd_s[sl]
        decay = dec_s[sl, 0:1, :]
        for t in range(C_HEADS // 2):
            kcols = slice(t * LANES, (t + 1) * LANES)
            vcols = slice(2 * t * LANES, 2 * (t + 1) * LANES)
            qs_t = qs[:, :, kcols].reshape(GLA_SUB, LANES)
            q_st = jnp.concatenate([jnp.where(head_of_lane == e, qs_t, jnp.zeros((), BF16)) for e in range(2)], axis=0)
            att = lax.dot_general(q_st, ks[:, :, kcols].reshape(GLA_SUB, LANES), nt, preferred_element_type=F32)
            v_t = v[:, vcols]
            o = jnp.concatenate(
                [jnp.dot(jnp.where(keep, att[e * GLA_SUB:(e + 1) * GLA_SUB], 0.0).astype(BF16),
                         v_t[:, e * LANES:(e + 1) * LANES], preferred_element_type=F32) for e in range(2)], axis=1)
            st = st_in[t]
            inter = []
            for c in range(n_chunk):
                inter.append(lax.dot_general(qd[c, :, kcols], st.astype(BF16), nt, preferred_element_type=F32))
                inc = lax.dot_general(v_t[c * C_CHUNK:(c + 1) * C_CHUNK, :], kd[c, :, kcols], tn,
                                      preferred_element_type=F32)
                st = jnp.where(own_block, st * decay[c, :, kcols] + inc, 0.0)
            st_out.append(st)
            o = o + jnp.concatenate(inter, axis=0)
            for e in range(2):
                cols = slice((2 * t + e) * LANES, (2 * t + e + 1) * LANES)
                gate = cr_ref[rows, cols].astype(F32)
                y_ref[rows, cols] = (_rms(o[:, e * LANES:(e + 1) * LANES], on_gain)
                                     * (gate * _sigmoid(gate))).astype(BF16)
        return st_out

    def prepare_pair(s):
        for slot in range(2):
            prepare(2 * s + slot, slot)

    def finish_pair(s):
        st = [st_ref[t] for t in range(C_HEADS // 2)]
        for slot in range(2):
            st = finish(2 * s + slot, slot, st)
        for t in range(C_HEADS // 2):
            st_ref[t] = st[t]

    n_pair = tm // (2 * GLA_SUB)
    prepare_pair(0)

    def body(s, carry):
        finish_pair(s - 1)
        prepare_pair(s)
        return carry

    lax.fori_loop(1, n_pair, body, 0)
    finish_pair(n_pair - 1)


def _gla(q, k, v, gl, cr, gate_up, gate_bias, out_norm, tri, layer, batch, seq):
    t = q.shape[0]
    tm = GLA_TILE
    per_seq = seq // tm
    width = C_HEADS * C_DV
    kwidth = C_HEADS * C_DK
    tok = lambda w: pl.BlockSpec((tm, w), lambda b, i: (b * per_seq + i, 0))
    return pl.pallas_call(
        _gla_kernel,
        grid=(batch, per_seq),
        in_specs=[tok(kwidth), tok(kwidth), tok(width), tok(LANES), tok(width),
                  pl.BlockSpec((None, LANES, kwidth), lambda b, i: (layer, 0, 0)),
                  pl.BlockSpec((None, 1, kwidth), lambda b, i: (layer, 0, 0)),
                  pl.BlockSpec((None, 1, C_DV), lambda b, i: (layer, 0, 0)),
                  pl.BlockSpec((GLA_SUB, GLA_SUB), lambda b, i: (0, 0))],
        out_specs=tok(width),
        out_shape=jax.ShapeDtypeStruct((t, width), BF16),
        scratch_shapes=[pltpu.VMEM((C_HEADS // 2, 2 * C_DV, LANES), F32)]
        + [pltpu.VMEM((2 * GLA_SUB // C_CHUNK, C_CHUNK, kwidth), BF16)] * 4
        + [pltpu.VMEM((2 * GLA_SUB // C_CHUNK, 8, kwidth), F32)],
        name="gla",
        compiler_params=_params(2),
    )(q, k, v, gl, cr, gate_up, gate_bias, out_norm, tri)


def _merge_kernel(x_ref, gm_ref, o1, o2, o3, l1, l2, l3, yb_ref, yc_ref, wg_ref, wb_ref, wo_ref, ex_ref, out_ref):
    x = x_ref[...]
    h = _rms(x, gm_ref[...]).astype(BF16)
    la, lb, lc = l1[...], l2[...], l3[...]
    top = jnp.maximum(jnp.maximum(la, lb), lc)
    ea, eb, ec = jnp.exp2(la - top), jnp.exp2(lb - top), jnp.exp2(lc - top)
    inv = 1.0 / (ea + eb + ec)
    expand = ex_ref[...]
    low_lanes = lax.broadcasted_iota(jnp.int32, (1, LANES), 1) < LANES // 2
    ya = None
    for e, o in ((ea, o1), (eb, o2), (ec, o3)):
        w = e * inv
        w_hi = w.astype(BF16)
        w_lo = pltpu.roll(w - w_hi.astype(F32), LANES // 2, 1).astype(BF16)
        wx = jnp.dot(jnp.where(low_lanes, w_hi, w_lo), expand, preferred_element_type=F32)
        term = wx * o[...].astype(F32)
        ya = term if ya is None else ya + term
    merged = None
    for i, y in enumerate((ya.astype(BF16), yb_ref[...], yc_ref[...])):
        gate = _sigmoid(_dot_t(h, wg_ref[i * D_MODEL:(i + 1) * D_MODEL, :]))
        term = gate * jnp.dot(y, wb_ref[i], preferred_element_type=F32)
        merged = term if merged is None else merged + term
    out_ref[...] = x + jnp.dot(merged.astype(BF16), wo_ref[...], preferred_element_type=F32)


def _merge(x, gain, oa, la, yb, yc, w_gate, w_branch, w_out, expand, layer):
    t = x.shape[0]
    tm = TOKEN_TILE
    tok = lambda w: pl.BlockSpec((tm, w), lambda i: (i, 0))
    bw = A_GROUP_WIDTH
    return pl.pallas_call(
        _merge_kernel,
        grid=(t // tm,),
        in_specs=[tok(D_MODEL),
                  pl.BlockSpec((None, 1, D_MODEL), lambda i: (layer, 0, 0)),
                  tok(bw), tok(bw), tok(bw), tok(LANES), tok(LANES), tok(LANES), tok(bw), tok(bw),
                  _resident((None, N_BRANCH * D_MODEL, D_MODEL), lambda i: (layer, 0, 0)),
                  _resident((None, N_BRANCH, bw, D_MODEL), lambda i: (layer, 0, 0, 0)),
                  _resident((None, D_MODEL, D_MODEL), lambda i: (layer, 0, 0)),
                  pl.BlockSpec((LANES, bw), lambda i: (0, 0))],
        out_specs=tok(D_MODEL),
        out_shape=jax.ShapeDtypeStruct((t, D_MODEL), F32),
        name="merge",
        compiler_params=_params(1),
    )(x, gain, *oa, *la, yb, yc, w_gate, w_branch, w_out, expand)


def _pack_w_in(w_in):
    w_t = jnp.swapaxes(w_in, 1, 2)
    n_layer = w_t.shape[0]
    edges = np.cumsum([0, 4608, 512, 256, 256, 256, 512, C_GATE_RANK, 512, N_BRANCH * D_MODEL])
    seg = [w_t[:, edges[i]:edges[i + 1], :] for i in range(9)]
    a, b_q, b_kv, c_q, c_k, c_v, c_glow, c_r, gates = seg
    b_kv = b_kv.reshape(n_layer, 2, 2, 1, HEAD_DIM, D_MODEL)
    b_kv = jnp.broadcast_to(b_kv, (n_layer, 2, 2, 2, HEAD_DIM, D_MODEL)).reshape(n_layer, 512, D_MODEL)
    c_glow = jnp.pad(c_glow, ((0, 0), (0, LANES - C_GATE_RANK), (0, 0)))
    packed = jnp.concatenate([a, b_q, b_kv, c_q, c_k, c_v, c_glow, c_r], axis=1)
    assert packed.shape == (n_layer, W_PACKED, D_MODEL)
    return packed.astype(BF16), gates.astype(BF16)


def kernel(x, positions, norm_ffn1, w_ffn1_in, w_ffn1_out, norm_mix, w_in, a_q_norm, a_k_norm, b_q_norm, b_k_norm,
           b_sinks, c_gate_up, c_gate_bias, c_out_norm, w_branch, w_out, norm_ffn2, w_ffn2_in, w_ffn2_out):
    batch, seq, _ = x.shape
    n_layer = w_in.shape[0]
    assert seq % ATTN_CHUNK == 0 and seq % GLA_TILE == 0 and x.shape[2] == D_MODEL

    w_packed, w_gate = _pack_w_in(w_in)
    w1_in, w1_out = w_ffn1_in.astype(BF16), w_ffn1_out.astype(BF16)
    w2_in, w2_out = w_ffn2_in.astype(BF16), w_ffn2_out.astype(BF16)
    wb, wo = w_branch.astype(BF16), w_out.astype(BF16)
    row = lambda g: g[:, None, :]
    head_gains = jnp.stack([jnp.tile(g * s, (1, LANES // HEAD_DIM)) for g, s in
                            ((a_q_norm, Q_SCALE), (a_k_norm, 1.0), (b_q_norm, Q_SCALE), (b_k_norm, 1.0))], axis=1)
    gate_up = jnp.pad(c_gate_up, ((0, 0), (0, LANES - C_GATE_RANK), (0, 0))).astype(BF16)
    gate_bias = row(c_gate_bias)

    lane_head = np.arange(LANES) // HEAD_DIM
    ones_bd = jnp.asarray((lane_head[:, None] == lane_head[None, :]) / HEAD_DIM, BF16)
    expand = jnp.asarray((np.arange(LANES) % (LANES // 2))[:, None] == (np.arange(A_GROUP_WIDTH) // HEAD_DIM)[None, :],
                         BF16)
    idx = np.arange(GLA_SUB)
    tri = jnp.asarray((idx[:, None] >= idx[None, :]) & (idx[:, None] // C_CHUNK == idx[None, :] // C_CHUNK), BF16)

    cos, sin = _rope_tables(positions)
    xt = x.reshape(batch * seq, D_MODEL)
    for l in range(n_layer):
        xt = _ffn(xt, row(norm_ffn1), w1_in, w1_out, l)
        (qa1, ka1, va1, qa2, ka2, va2, qa3, ka3, va3, qb, kb, vb, qc, kc, vc, gl, cr) = _inproj(
            xt, row(norm_mix), cos, sin, w_packed, head_gains, ones_bd, l, batch, seq)
        oa, la = [], []
        for (window, d), (q, k, v) in zip(A_CONFIGS, ((qa1, ka1, va1), (qa2, ka2, va2), (qa3, ka3, va3))):
            o, lse = _attention(q, k, v, None, window=window // d, kv_div=1, batch=batch, seq=seq)
            oa.append(o.reshape(batch * seq, A_GROUP_WIDTH))
            la.append(lse.reshape(batch * seq, LANES))
        yb, _ = _attention(qb, kb, vb, b_sinks[l], window=B_WINDOW - 1, kv_div=2, batch=batch, seq=seq)
        yb = yb.reshape(batch * seq, A_GROUP_WIDTH)
        yc = _gla(qc, kc, vc, gl, cr, gate_up, gate_bias, row(c_out_norm), tri, l, batch, seq)
        xt = _merge(xt, row(norm_mix), oa, la, yb, yc, w_gate, wb, wo, expand, l)
        xt = _ffn(xt, row(norm_ffn2), w2_in, w2_out, l)
    return xt.reshape(batch, seq, D_MODEL)
```
